```python
import math
import jax, jax.numpy as jnp
from jax import lax
import numpy as np

D_MODEL = 4096
BATCH = 4
SEQ = 2048
DEPTH = 4
DEC_BATCH = 8
DEC_SEQ = 8
PAST_LEN = 8192
PAGE_SIZE = 128

DH = 128
D_A = 3 * D_MODEL // 8
H_A = D_A // DH
A_BRANCHES = ((128, 1), (512, 4), (2048, 16))
WIN_MAX = 2048
BLK = 128
REL_BUCKETS = 32
REL_MAX_DIST = 2048
D_B = D_MODEL // 4
POOL_WINDOWS = (2, 4, 8, 16)
POOL_GROUPS = 4
CG = D_B // POOL_GROUPS
POOL_BUF = 15
D_C = D_MODEL - D_A - D_B
H_C = D_C // DH
DK = DH
DV = DH
GDN_CONV = 4
GDN_CHUNK = 64
D_MIX = D_A + D_B + D_C
D_FF = 256 * ((8 * D_MODEL // 3 + 255) // 256)
FFN_CONV = 3
N_IN = 3 * D_A + D_B + 4 * D_C + 2 * H_C
EPS = 1e-6
NEG_INF = -1e30

kernel_name = 'hymba_dilated_pool_gdn_decoder_step'


def rms_norm(x, gain):
    xf = x.astype(jnp.float32)
    y = xf * lax.rsqrt(jnp.mean(xf * xf, axis=-1, keepdims=True) + EPS)
    return (y * gain.astype(jnp.float32)).astype(x.dtype)


def rms_normalize(xf):
    return xf * lax.rsqrt(jnp.mean(xf * xf, axis=-1, keepdims=True) + EPS)


def l2_normalize(xf):
    return xf * lax.rsqrt(jnp.sum(xf * xf, axis=-1, keepdims=True) + EPS)


def causal_dwconv(ext, w):
    k = w.shape[0]
    t = ext.shape[1] - k + 1
    out = ext[:, 0:t] * w[0]
    for i in range(1, k):
        out = out + ext[:, i:i + t] * w[i]
    return out


def t5_bucket(dist):
    max_exact = REL_BUCKETS // 2
    d = jnp.maximum(dist, 1).astype(jnp.float32)
    large = max_exact + (jnp.log(d / max_exact) / math.log(REL_MAX_DIST / max_exact)
                         * (REL_BUCKETS - max_exact)).astype(jnp.int32)
    large = jnp.minimum(large, REL_BUCKETS - 1)
    return jnp.where(dist < max_exact, dist, large)


def branch_biases(rel_bias):
    out = []
    for window, dil in A_BRANCHES:
        nj = window // dil + 1
        buckets = t5_bucket(jnp.arange(nj, dtype=jnp.int32) * dil)
        out.append(rel_bias[buckets].T.astype(jnp.float32))
    return out


def band_dilated_attention(q, k, v, bias_hj, dil):
    b, s, h, dh = q.shape
    nj = bias_hj.shape[1]
    L = s // dil
    g = b * dil

    def to_classes(t):
        return t.reshape(b, L, dil, h, dh).transpose(0, 2, 1, 3, 4).reshape(g, L, h, dh)

    qc, kc, vc = to_classes(q), to_classes(k), to_classes(v)
    nb = -(-L // BLK)
    pad = nb * BLK - L
    qb = jnp.pad(qc, ((0, 0), (0, pad), (0, 0), (0, 0))).reshape(g, nb, BLK, h, dh)

    def band(t):
        tp = jnp.pad(t, ((0, 0), (BLK, pad), (0, 0), (0, 0))).reshape(g, nb + 1, BLK, h, dh)
        return jnp.concatenate([tp[:, :-1], tp[:, 1:]], axis=2)

    kb, vb = band(kc), band(vc)
    qi = jnp.arange(BLK)[:, None]
    ki = jnp.arange(2 * BLK)[None, :]
    rel = qi + BLK - ki
    key_pos = (jnp.arange(nb) * BLK)[:, None, None] - BLK + ki[None]
    valid = (rel >= 0) & (rel < nj) & (key_pos >= 0)
    bias = bias_hj[:, jnp.clip(rel, 0, nj - 1)]
    logits = jnp.einsum('gnqhd,gnkhd->gnhqk', qb, kb,
                        preferred_element_type=jnp.float32) * (DH ** -0.5) + bias[None, None]
    logits = jnp.where(valid[None, :, None], logits, NEG_INF)
    m = jnp.max(logits, axis=-1)
    p = jnp.exp(logits - m[..., None])
    den = jnp.sum(p, axis=-1)
    o = jnp.einsum('gnhqk,gnkhd->gnqhd', p, vb.astype(jnp.float32)) / jnp.swapaxes(den, 2, 3)[..., None]
    o = o.reshape(g, nb * BLK, h, dh)[:, :L].reshape(b, dil, L, h, dh).transpose(0, 2, 1, 3, 4).reshape(b, s, h, dh)

    def stat_back(t):
        t = jnp.swapaxes(t, 2, 3).reshape(g, nb * BLK, h)[:, :L]
        return t.reshape(b, dil, L, h).transpose(0, 2, 1, 3).reshape(b, s, h)

    return o, stat_back(m), stat_back(den)


def gathered_dilated_attention(q, k_all, v_all, bias_hj, dil, n_past):
    t = q.shape[1]
    nj = bias_hj.shape[1]
    idx = n_past + jnp.arange(t)[:, None] - jnp.arange(nj)[None, :] * dil
    valid = idx >= 0
    idx = jnp.maximum(idx, 0)
    kg = k_all[:, idx]
    vg = v_all[:, idx]
    logits = jnp.einsum('bthd,btjhd->bhtj', q, kg,
                        preferred_element_type=jnp.float32) * (DH ** -0.5) + bias_hj[None, :, None, :]
    logits = jnp.where(valid[None, None], logits, NEG_INF)
    m = jnp.max(logits, axis=-1)
    p = jnp.exp(logits - m[..., None])
    den = jnp.sum(p, axis=-1)
    o = jnp.einsum('bhtj,btjhd->bthd', p, vg.astype(jnp.float32)) / jnp.swapaxes(den, 1, 2)[..., None]
    return o, jnp.swapaxes(m, 1, 2), jnp.swapaxes(den, 1, 2)


def merge_branches(results):
    o_all = jnp.stack([r[0] for r in results])
    m_all = jnp.stack([r[1] for r in results])
    d_all = jnp.stack([r[2] for r in results])
    w = jnp.exp(m_all - jnp.max(m_all, axis=0, keepdims=True)) * d_all
    return jnp.sum(w[..., None] * o_all, axis=0) / jnp.sum(w, axis=0)[..., None]


def pooling_mixer(u, buf, n_valid, w_pool, scale):
    b, t, _ = u.shape
    p = buf.shape[1]
    ext = jnp.concatenate([buf, u], axis=1)
    cs = jnp.concatenate([jnp.zeros((b, 1, D_B), jnp.float32),
                          jnp.cumsum(ext.astype(jnp.float32), axis=1)], axis=1)
    pos = jnp.arange(t)
    hi = cs[:, p + 1:]
    diffs = []
    for gi, w in enumerate(POOL_WINDOWS):
        sl = slice(gi * CG, (gi + 1) * CG)
        lo = cs[:, p + 1 + pos - w, sl]
        cnt = jnp.minimum(w, n_valid + pos + 1).astype(jnp.float32)
        diffs.append((hi[..., sl] - lo) / cnt[:, None] - u[..., sl].astype(jnp.float32))
    d = jnp.stack(diffs, axis=2)
    y = jnp.einsum('btgc,gce->btge', d, w_pool.astype(jnp.float32))
    y = rms_normalize(y) * scale.astype(jnp.float32).reshape(POOL_GROUPS, CG)
    return y.reshape(b, t, D_B).astype(u.dtype), ext[:, -POOL_BUF:]


def chunk_gated_delta(q, k, v, beta, g, s0):
    b, t, h, _ = q.shape
    c = min(GDN_CHUNK, t)
    n = -(-t // c)
    pad = n * c - t

    def chunks(x):
        x = jnp.pad(x, [(0, 0), (0, pad)] + [(0, 0)] * (x.ndim - 2))
        x = x.reshape((b, n, c) + x.shape[2:])
        return jnp.moveaxis(x, 3, 1)

    qc, kc, vc, bc, gc = (chunks(x) for x in (q, k, v, beta, g))
    gcum = jnp.cumsum(gc, axis=-1)
    tril = jnp.tril(jnp.ones((c, c), bool))
    strict = jnp.tril(jnp.ones((c, c), bool), -1)
    decay = jnp.exp(jnp.where(tril, gcum[..., :, None] - gcum[..., None, :], -jnp.inf))
    kb = kc * bc[..., None]
    m_mat = jnp.where(strict, jnp.einsum('bhnid,bhnjd->bhnij', kb, kc) * decay, 0.0)
    rhs = jnp.concatenate([vc * bc[..., None], kb * jnp.exp(gcum)[..., None]], axis=-1)
    sol = lax.linalg.triangular_solve(m_mat + jnp.eye(c, dtype=jnp.float32), rhs,
                                      left_side=True, lower=True, unit_diagonal=True)
    u_c, w_c = sol[..., :DV], sol[..., DV:]
    a_qk = jnp.einsum('bhnid,bhnjd->bhnij', qc, kc) * decay
    q_dec = qc * jnp.exp(gcum)[..., None]
    k_dec = kc * jnp.exp(gcum[..., -1:] - gcum)[..., None]
    g_last = jnp.exp(gcum[..., -1])

    def step(state, xs):
        u_n, w_n, q_n, k_n, a_n, gl_n = xs
        v_new = u_n - jnp.einsum('bhcd,bhde->bhce', w_n, state)
        o_n = jnp.einsum('bhcd,bhde->bhce', q_n, state) + jnp.einsum('bhij,bhje->bhie', a_n, v_new)
        state = state * gl_n[..., None, None] + jnp.einsum('bhcd,bhce->bhde', k_n, v_new)
        return state, o_n

    xs = tuple(jnp.moveaxis(x, 2, 0) for x in (u_c, w_c, q_dec, k_dec, a_qk, g_last))
    s_fin, o = lax.scan(step, s0, xs)
    o = jnp.moveaxis(o, 0, 2).reshape(b, h, n * c, DV)[:, :, :t]
    return jnp.swapaxes(o, 1, 2), s_fin


def gdn_mixer(qkv_pre, gate, beta_raw, a_raw, conv_buf, s0, conv_w, a_log, dt_bias, out_gain):
    b, t, _ = qkv_pre.shape
    ext = jnp.concatenate([conv_buf, qkv_pre], axis=1)
    qkv = jax.nn.silu(causal_dwconv(ext, conv_w).astype(jnp.float32))
    q, k, v = jnp.split(qkv, 3, axis=-1)
    q = l2_normalize(q.reshape(b, t, H_C, DK)) * (DK ** -0.5)
    k = l2_normalize(k.reshape(b, t, H_C, DK))
    v = v.reshape(b, t, H_C, DV)
    beta = jax.nn.sigmoid(beta_raw.astype(jnp.float32))
    g = -jnp.exp(a_log.astype(jnp.float32)) * jax.nn.softplus(a_raw.astype(jnp.float32) + dt_bias.astype(jnp.float32))
    o, s_new = chunk_gated_delta(q, k, v, beta, g, s0.astype(jnp.float32))
    o = rms_normalize(o) * out_gain.astype(jnp.float32) * jax.nn.silu(gate.astype(jnp.float32).reshape(b, t, H_C, DV))
    return o.reshape(b, t, D_C).astype(qkv_pre.dtype), s_new.astype(s0.dtype), ext[:, -(GDN_CONV - 1):]


def trunk_layer(x, past_kv, pool_buf, pool_valid, gconv_buf, gstate, fconv_buf, lw, biases):
    b, t, _ = x.shape
    xn = rms_norm(x, lw['norm_mix'])
    proj = xn @ lw['w_in']
    cuts = [D_A, 2 * D_A, 3 * D_A, 3 * D_A + D_B, 3 * D_A + D_B + 3 * D_C,
            3 * D_A + D_B + 4 * D_C, 3 * D_A + D_B + 4 * D_C + H_C]
    aq, ak, av, pu, cqkv, cgate, cbeta, ca = jnp.split(proj, cuts, axis=-1)
    q = rms_norm(aq.reshape(b, t, H_A, DH), lw['a_q_norm'])
    k = rms_norm(ak.reshape(b, t, H_A, DH), lw['a_k_norm'])
    v = av.reshape(b, t, H_A, DH)
    if past_kv is None:
        res = [band_dilated_attention(q, k, v, bias, dil) for (win, dil), bias in zip(A_BRANCHES, biases)]
    else:
        n_past = past_kv.shape[1]
        k_all = jnp.concatenate([past_kv[:, :, 0].astype(k.dtype), k], axis=1)
        v_all = jnp.concatenate([past_kv[:, :, 1].astype(v.dtype), v], axis=1)
        res = [gathered_dilated_attention(q, k_all, v_all, bias, dil, n_past)
               for (win, dil), bias in zip(A_BRANCHES, biases)]
    oa = merge_branches(res)
    ya = rms_norm(oa, lw['a_out_norm'].reshape(H_A, DH)).reshape(b, t, D_A).astype(x.dtype)
    kv_rows = jnp.stack([k, v], axis=2)[:, -min(WIN_MAX, t):]
    yb, pool_new = pooling_mixer(pu, pool_buf, pool_valid, lw['pool_w'], lw['pool_scale'])
    yc, gstate_new, gconv_new = gdn_mixer(cqkv, cgate, cbeta, ca, gconv_buf, gstate, lw['gdn_conv_w'],
                                          lw['gdn_a_log'], lw['gdn_dt_bias'], lw['gdn_out_norm'])
    h = x + jnp.concatenate([ya, yb, yc], axis=-1) @ lw['w_out']
    up = rms_norm(h, lw['norm_ffn']) @ lw['ffn_up']
    ext = jnp.concatenate([fconv_buf, up], axis=1)
    cv = causal_dwconv(ext, lw['ffn_conv_w']) + lw['ffn_conv_b']
    gt, vl = jnp.split(cv, 2, axis=-1)
    y = h + (jax.nn.silu(gt) * vl) @ lw['ffn_down']
    return y, (kv_rows, pool_new, gconv_new, gstate_new, ext[:, -(FFN_CONV - 1):])


def setup_inputs(seed: int = 0) -> dict:
    key = jax.random.key(seed)
    ks = jax.random.split(key, 32)
    f32 = jnp.float32

    def nrm(k, shape, scale):
        return jax.random.normal(k, shape, f32) * scale

    a_buf = min(WIN_MAX, PAST_LEN)
    dt = jnp.exp(jax.random.uniform(ks[16], (DEPTH, H_C), f32, math.log(1e-3), math.log(1e-1)))
    return {
        'x_prompt': nrm(ks[0], (BATCH, SEQ, D_MODEL), 1.0),
        'x_sample': nrm(ks[1], (DEC_BATCH, DEC_SEQ, D_MODEL), 1.0),
        'cache_attn_kv': nrm(ks[2], (DEPTH, DEC_BATCH, a_buf, 2, H_A, DH), 1.0),
        'state_pool': nrm(ks[3], (DEPTH, DEC_BATCH, POOL_BUF, D_B), 1.0),
        'state_gdn_conv': nrm(ks[4], (DEPTH, DEC_BATCH, GDN_CONV - 1, 3 * D_C), 1.0),
        'state_gdn': nrm(ks[5], (DEPTH, DEC_BATCH, H_C, DK, DV), 0.3),
        'state_ffn_conv': nrm(ks[6], (DEPTH, DEC_BATCH, FFN_CONV - 1, 2 * D_FF), 1.0),
        'rel_bias': nrm(ks[7], (REL_BUCKETS, H_A), 0.5),
        'norm_mix': 1.0 + nrm(ks[8], (DEPTH, D_MODEL), 0.02),
        'w_in': nrm(ks[9], (DEPTH, D_MODEL, N_IN), D_MODEL ** -0.5),
        'a_q_norm': 1.0 + nrm(ks[10], (DEPTH, DH), 0.02),
        'a_k_norm': 1.0 + nrm(ks[11], (DEPTH, DH), 0.02),
        'a_out_norm': 1.0 + nrm(ks[12], (DEPTH, D_A), 0.02),
        'pool_w': nrm(ks[13], (DEPTH, POOL_GROUPS, CG, CG), CG ** -0.5),
        'pool_scale': 1.0 + nrm(ks[14], (DEPTH, D_B), 0.1),
        'gdn_conv_w': nrm(ks[15], (DEPTH, GDN_CONV, 3 * D_C), GDN_CONV ** -0.5),
        'gdn_a_log': jnp.log(jax.random.uniform(ks[17], (DEPTH, H_C), f32, 1.0, 16.0)),
        'gdn_dt_bias': dt + jnp.log(-jnp.expm1(-dt)),
        'gdn_out_norm': 1.0 + nrm(ks[18], (DEPTH, DV), 0.02),
        'w_out': nrm(ks[19], (DEPTH, D_MIX, D_MODEL), D_MIX ** -0.5),
        'norm_ffn': 1.0 + nrm(ks[20], (DEPTH, D_MODEL), 0.02),
        'ffn_up': nrm(ks[21], (DEPTH, D_MODEL, 2 * D_FF), D_MODEL ** -0.5),
        'ffn_conv_w': nrm(ks[22], (DEPTH, FFN_CONV, 2 * D_FF), FFN_CONV ** -0.5),
        'ffn_conv_b': nrm(ks[23], (DEPTH, 2 * D_FF), 0.01),
        'ffn_down': nrm(ks[24], (DEPTH, D_FF, D_MODEL), D_FF ** -0.5),
    }


def reference(x_prompt, x_sample, cache_attn_kv, state_pool, state_gdn_conv, state_gdn, state_ffn_conv,
              rel_bias, norm_mix, w_in, a_q_norm, a_k_norm, a_out_norm, pool_w, pool_scale,
              gdn_conv_w, gdn_a_log, gdn_dt_bias, gdn_out_norm, w_out, norm_ffn,
              ffn_up, ffn_conv_w, ffn_conv_b, ffn_down):
    biases = branch_biases(rel_bias)
    bp = x_prompt.shape[0]
    dtp = x_prompt.dtype
    xp, xs = x_prompt, x_sample
    p_kv, p_pool, p_gconv, p_gdn, p_fconv = [], [], [], [], []
    s_kv, s_pool, s_gconv, s_gdn, s_fconv = [], [], [], [], []
    for l in range(DEPTH):
        lw = {'norm_mix': norm_mix[l], 'w_in': w_in[l], 'a_q_norm': a_q_norm[l], 'a_k_norm': a_k_norm[l],
              'a_out_norm': a_out_norm[l], 'pool_w': pool_w[l], 'pool_scale': pool_scale[l],
              'gdn_conv_w': gdn_conv_w[l], 'gdn_a_log': gdn_a_log[l], 'gdn_dt_bias': gdn_dt_bias[l],
              'gdn_out_norm': gdn_out_norm[l], 'w_out': w_out[l], 'norm_ffn': norm_ffn[l],
              'ffn_up': ffn_up[l], 'ffn_conv_w': ffn_conv_w[l], 'ffn_conv_b': ffn_conv_b[l],
              'ffn_down': ffn_down[l]}
        xp, (kv, pl, gcv, gst, fcv) = trunk_layer(
            xp, None, jnp.zeros((bp, POOL_BUF, D_B), dtp), 0,
            jnp.zeros((bp, GDN_CONV - 1, 3 * D_C), dtp), jnp.zeros((bp, H_C, DK, DV), dtp),
            jnp.zeros((bp, FFN_CONV - 1, 2 * D_FF), dtp), lw, biases)
        p_kv.append(kv); p_pool.append(pl); p_gconv.append(gcv); p_gdn.append(gst); p_fconv.append(fcv)
        xs, (kv, pl, gcv, gst, fcv) = trunk_layer(
            xs, cache_attn_kv[l], state_pool[l], state_pool.shape[2],
            state_gdn_conv[l], state_gdn[l], state_ffn_conv[l], lw, biases)
        s_kv.append(kv); s_pool.append(pl); s_gconv.append(gcv); s_gdn.append(gst); s_fconv.append(fcv)
    return (xp, xs,
            jnp.stack(p_kv), jnp.stack(s_kv),
            jnp.stack(p_pool), jnp.stack(s_pool),
            jnp.stack(p_gconv), jnp.stack(s_gconv),
            jnp.stack(p_gdn), jnp.stack(s_gdn),
            jnp.stack(p_fconv), jnp.stack(s_fconv))
```

```python
import functools
import math

import numpy as np
import jax
import jax.numpy as jnp
from jax import lax
from jax.experimental import pallas as pl
from jax.experimental.pallas import tpu as pltpu

F32 = jnp.float32
BF16 = jnp.bfloat16
HIGHEST = lax.Precision.HIGHEST

DH = 128
H_A = 12
H_C = 12
A_BRANCHES = ((128, 1), (512, 4), (2048, 16))
BLK = 128
REL_BUCKETS = 32
REL_MAX_DIST = 2048
POOL_WINDOWS = (2, 4, 8, 16)
CG = 256
POOL_BUF = 15
GDN_CONV = 4
GDN_CHUNK = 64
GDN_SUB = 16
FFN_CONV = 3
EPS = 1e-6
NEG_INF = -1e30

D_A = H_A * DH
D_B = len(POOL_WINDOWS) * CG
D_C = H_C * DH
OFF_Q, OFF_K, OFF_V = 0, D_A, 2 * D_A
OFF_P = 3 * D_A
OFF_C = OFF_P + D_B
OFF_G = OFF_C + 3 * D_C
N_MAIN = OFF_G + D_C
N_TAIL = 2 * H_C

LANE = 128
SUBLANE = 8
VMEM_CAP = 56 * 1024 * 1024


def _cparams(sem, vmem_bytes):
    limit = int(min(max(vmem_bytes * 5 // 4 + (2 << 20), 16 << 20), VMEM_CAP))
    return pltpu.CompilerParams(dimension_semantics=sem, vmem_limit_bytes=limit)


def _rms_rows(x):
    return x * lax.rsqrt(jnp.mean(x * x, axis=-1, keepdims=True) + EPS)


def _silu(x):
    return x * jax.nn.sigmoid(x)


def _dot(a, b):
    return jnp.dot(a, b, preferred_element_type=F32)


def _dot_nt(a, b):
    return lax.dot_general(a, b, (((1,), (1,)), ((), ())), preferred_element_type=F32)


def _hdot(a, b):
    return jnp.dot(a, b, preferred_element_type=F32, precision=HIGHEST)


def _hdot_nt(a, b):
    return lax.dot_general(a, b, (((1,), (1,)), ((), ())), preferred_element_type=F32,
                           precision=HIGHEST)


def _hdot_tn(a, b):
    return lax.dot_general(a, b, (((0,), (0,)), ((), ())), preferred_element_type=F32,
                           precision=HIGHEST)


def _rmsnorm_kernel(x_ref, g_ref, o_ref):
    o_ref[...] = (_rms_rows(x_ref[...]) * g_ref[...]).astype(o_ref.dtype)


def rmsnorm_cast(x, gain):
    m, d = x.shape
    tm = min(m, 256)
    return pl.pallas_call(
        _rmsnorm_kernel,
        grid=(m // tm,),
        in_specs=[pl.BlockSpec((tm, d), lambda i: (i, 0)),
                  pl.BlockSpec((1, d), lambda i: (0, 0))],
        out_specs=pl.BlockSpec((tm, d), lambda i: (i, 0)),
        out_shape=jax.ShapeDtypeStruct((m, d), BF16),
        compiler_params=_cparams(("parallel",), 2 * tm * d * 6),
        name="rmsnorm_cast",
    )(x, gain.reshape(1, d))


def _proj_in_kernel(x_ref, w_ref, wt_ref, o_ref, ot_ref):
    o_ref[...] = _dot(x_ref[...], w_ref[...])

    @pl.when(pl.program_id(1) == 0)
    def _():
        ot_ref[...] = _dot(x_ref[...], wt_ref[...])


def proj_in(xn, w_main, w_tail):
    m, k = xn.shape
    n = w_main.shape[1]
    tm = min(m, 1024)
    tn = 512
    vmem = 2 * (tm * k * 2 + k * tn * 2 + tm * tn * 4 + k * LANE * 2 + tm * LANE * 4)
    return pl.pallas_call(
        _proj_in_kernel,
        grid=(m // tm, n // tn),
        in_specs=[pl.BlockSpec((tm, k), lambda i, j: (i, 0)),
                  pl.BlockSpec((k, tn), lambda i, j: (0, j)),
                  pl.BlockSpec((k, LANE), lambda i, j: (0, 0))],
        out_specs=[pl.BlockSpec((tm, tn), lambda i, j: (i, j)),
                   pl.BlockSpec((tm, LANE), lambda i, j: (i, 0))],
        out_shape=[jax.ShapeDtypeStruct((m, n), F32),
                   jax.ShapeDtypeStruct((m, LANE), F32)],
        compiler_params=_cparams(("parallel", "arbitrary"), vmem),
        name="proj_in",
    )(xn, w_main, w_tail)


def _strided_rows(start, dil):
    return pl.ds(start, BLK) if dil == 1 else pl.ds(start, BLK, stride=dil)


def _attn_prompt_kernel(q_ref, k_ref, v_ref, qg_ref, kg_ref, og_ref, bias_ref,
                        y_ref, kn_ref, qs_ref, acc_ref, m_ref, l_ref):
    seq = q_ref.shape[1]
    qs_ref[...] = _rms_rows(q_ref[0]) * qg_ref[...] * (DH ** -0.5)
    kn_ref[0] = _rms_rows(k_ref[0]) * kg_ref[...]
    acc_ref[...] = jnp.zeros(acc_ref.shape, F32)
    l_ref[...] = jnp.zeros(l_ref.shape, F32)
    m_ref[...] = jnp.full(m_ref.shape, NEG_INF, F32)

    for bi, (window, dil) in enumerate(A_BRANCHES):
        assert window // dil == BLK and seq % (dil * BLK) == 0
        nb = seq // (dil * BLK)

        def body(it, carry, bi=bi, dil=dil, nb=nb):
            r = it // nb
            n = it - r * nb
            rows = _strided_rows(n * (BLK * dil) + r, dil)
            qb = qs_ref[rows, :].astype(BF16)
            kc = kn_ref[0, rows, :]
            vc = v_ref[0, rows, :]
            bias_cur = bias_ref[bi, 0, :, BLK:]
            if nb > 1:
                prows = _strided_rows(jnp.maximum(n - 1, 0) * (BLK * dil) + r, dil)
                kk = jnp.concatenate([kn_ref[0, prows, :], kc], axis=0).astype(BF16)
                vv = jnp.concatenate([v_ref[0, prows, :], vc], axis=0).astype(BF16)
                bias_prev = jnp.where(n > 0, bias_ref[bi, 0, :, :BLK], NEG_INF)
                s = _dot_nt(qb, kk) + jnp.concatenate([bias_prev, bias_cur], axis=1)
            else:
                vv = vc.astype(BF16)
                s = _dot_nt(qb, kc.astype(BF16)) + bias_cur
            m_old = m_ref[rows, :]
            m_new = jnp.maximum(m_old, jnp.max(s, axis=1, keepdims=True))
            p = jnp.exp(s - m_new[:, :1])
            alpha = jnp.exp(m_old - m_new)
            l_ref[rows, :] = alpha * l_ref[rows, :] + jnp.sum(p, axis=1, keepdims=True)
            acc_ref[rows, :] = alpha * acc_ref[rows, :] + _dot(p.astype(BF16), vv)
            m_ref[rows, :] = m_new
            return carry

        lax.fori_loop(0, dil * nb, body, 0)

    o = acc_ref[...] / l_ref[...]
    y_ref[0] = (_rms_rows(o) * og_ref[0]).astype(y_ref.dtype)


def attn_prompt(proj3, q_gain, k_gain, o_gain, bias_blocks):
    b, s, _ = proj3.shape
    col = lambda off: (lambda i, h: (i, 0, off // DH + h))
    vmem = 2 * (3 * s * DH * 4 + s * DH * 2 + s * DH * 4 + 3 * BLK * 2 * BLK * 4) + 4 * s * DH * 4
    return pl.pallas_call(
        _attn_prompt_kernel,
        grid=(b, H_A),
        in_specs=[pl.BlockSpec((1, s, DH), col(OFF_Q)),
                  pl.BlockSpec((1, s, DH), col(OFF_K)),
                  pl.BlockSpec((1, s, DH), col(OFF_V)),
                  pl.BlockSpec((1, DH), lambda i, h: (0, 0)),
                  pl.BlockSpec((1, DH), lambda i, h: (0, 0)),
                  pl.BlockSpec((1, 1, DH), lambda i, h: (h, 0, 0)),
                  pl.BlockSpec((len(A_BRANCHES), 1, BLK, 2 * BLK), lambda i, h: (0, h, 0, 0))],
        out_specs=[pl.BlockSpec((1, s, DH), lambda i, h: (i, 0, h)),
                   pl.BlockSpec((1, s, DH), lambda i, h: (i, 0, h))],
        out_shape=[jax.ShapeDtypeStruct((b, s, D_A), BF16),
                   jax.ShapeDtypeStruct((b, s, D_A), F32)],
        scratch_shapes=[pltpu.VMEM((s, DH), F32)] * 4,
        compiler_params=_cparams(("parallel", "parallel"), vmem),
        name="attn_prompt",
    )(proj3, proj3, proj3, q_gain, k_gain, o_gain.reshape(H_A, 1, DH), bias_blocks)


def _attn_sample_kernel(q_ref, k_ref, v_ref, kc_ref, vc_ref, qg_ref, kg_ref, og_ref,
                        bc_ref, cc_ref, bn_ref, cn_ref, y_ref, kn_ref, kpad_ref, vpad_ref):
    t = q_ref.shape[1]
    qn = (_rms_rows(q_ref[0]) * qg_ref[...] * (DH ** -0.5)).astype(BF16)
    kn = _rms_rows(k_ref[0]) * kg_ref[...]
    kn_ref[0] = kn
    kpad_ref[...] = jnp.zeros(kpad_ref.shape, F32)
    vpad_ref[...] = jnp.zeros(vpad_ref.shape, F32)
    kpad_ref[0:t, :] = kn
    vpad_ref[0:t, :] = v_ref[0]
    s_c = _dot_nt(qn, kc_ref[0, 0].astype(BF16)) + bc_ref[0]
    s_n = _dot_nt(qn, kpad_ref[...].astype(BF16)) + bn_ref[0]
    m = jnp.maximum(jnp.max(s_c, axis=1, keepdims=True), jnp.max(s_n, axis=1, keepdims=True))
    p_c = jnp.exp(s_c - m) * cc_ref[...]
    p_n = jnp.exp(s_n - m) * cn_ref[...]
    den = jnp.sum(p_c, axis=1, keepdims=True) + jnp.sum(p_n, axis=1, keepdims=True)
    o = (_dot(p_c.astype(BF16), vc_ref[0, 0].astype(BF16))
         + _dot(p_n.astype(BF16), vpad_ref[...].astype(BF16))) / den
    y_ref[0] = _rms_rows(o) * og_ref[0]


def attn_sample(proj3, cache4, layer, q_gain, k_gain, o_gain, bias_c, cnt_c, bias_n, cnt_n):
    b, t, _ = proj3.shape
    n_past = cache4.shape[2]
    col = lambda off: (lambda i, h: (i, 0, off // DH + h))
    vmem = 2 * (2 * n_past * DH * 4 + 2 * t * n_past * 4) + 2 * BLK * DH * 4 + (1 << 20)
    return pl.pallas_call(
        _attn_sample_kernel,
        grid=(b, H_A),
        in_specs=[pl.BlockSpec((1, t, DH), col(OFF_Q)),
                  pl.BlockSpec((1, t, DH), col(OFF_K)),
                  pl.BlockSpec((1, t, DH), col(OFF_V)),
                  pl.BlockSpec((1, 1, n_past, DH), lambda i, h: (layer, i, 0, h)),
                  pl.BlockSpec((1, 1, n_past, DH), lambda i, h: (layer, i, 0, H_A + h)),
                  pl.BlockSpec((1, DH), lambda i, h: (0, 0)),
                  pl.BlockSpec((1, DH), lambda i, h: (0, 0)),
                  pl.BlockSpec((1, 1, DH), lambda i, h: (h, 0, 0)),
                  pl.BlockSpec((1, t, n_past), lambda i, h: (h, 0, 0)),
                  pl.BlockSpec((t, n_past), lambda i, h: (0, 0)),
                  pl.BlockSpec((1, t, BLK), lambda i, h: (h, 0, 0)),
                  pl.BlockSpec((t, BLK), lambda i, h: (0, 0))],
        out_specs=[pl.BlockSpec((1, t, DH), lambda i, h: (i, 0, h)),
                   pl.BlockSpec((1, t, DH), lambda i, h: (i, 0, h))],
        out_shape=[jax.ShapeDtypeStruct((b, t, D_A), F32),
                   jax.ShapeDtypeStruct((b, t, D_A), F32)],
        scratch_shapes=[pltpu.VMEM((BLK, DH), F32)] * 2,
        compiler_params=_cparams(("parallel", "parallel"), vmem),
        name="attn_sample",
    )(proj3, proj3, proj3, cache4, cache4, q_gain, k_gain, o_gain.reshape(H_A, 1, DH),
      bias_c, cnt_c, bias_n, cnt_n)


POOL_HALO = 16


def _pool_kernel(*refs, t, rows, n_valid, has_buf):
    ng = len(POOL_WINDOWS)
    u_refs = refs[:ng]
    refs = refs[ng:]
    if has_buf:
        buf_refs = refs[:ng]
        refs = refs[ng:]
    w_ref, scale_ref, y_ref, ext_ref = refs
    for g, win in enumerate(POOL_WINDOWS):
        ext_ref[0:POOL_HALO, :] = jnp.zeros((POOL_HALO, CG), F32)
        if has_buf:
            ext_ref[POOL_HALO - POOL_BUF:POOL_HALO, :] = buf_refs[g][0]
        ext_ref[POOL_HALO:, :] = u_refs[g][0]

        def chunk(ci, carry, g=g, win=win):
            base = pl.multiple_of(ci * rows, SUBLANE)
            w = ext_ref[pl.ds(base, rows + POOL_HALO), :]
            u = w[POOL_HALO:]
            tot = u
            for i in range(1, win):
                tot = tot + w[POOL_HALO - i:POOL_HALO - i + rows]
            pos = base + lax.broadcasted_iota(jnp.int32, (rows, 1), 0)
            cnt = jnp.minimum(win, n_valid + pos + 1).astype(F32)
            d = tot / cnt - u
            y = _rms_rows(_dot(d.astype(BF16), w_ref[g])) * scale_ref[:, g * CG:(g + 1) * CG]
            y_ref[0, pl.ds(base, rows), g * CG:(g + 1) * CG] = y.astype(y_ref.dtype)
            return carry

        lax.fori_loop(0, t // rows, chunk, 0)


def pool_mixer(proj3, bufs, w_pool, scale, out_dtype):
    b, t, _ = proj3.shape
    ng = len(POOL_WINDOWS)
    rows = min(t, 256)
    has_buf = bufs is not None
    in_specs = [pl.BlockSpec((1, t, CG), (lambda i, g=g: (i, 0, OFF_P // CG + g))) for g in range(ng)]
    args = [proj3] * ng
    if has_buf:
        in_specs += [pl.BlockSpec((1, POOL_BUF, CG), (lambda i, g=g: (i, 0, g))) for g in range(ng)]
        args += [bufs] * ng
    in_specs += [pl.BlockSpec((ng, CG, CG), lambda i: (0, 0, 0)),
                 pl.BlockSpec((1, D_B), lambda i: (0, 0))]
    args += [w_pool, scale]
    vmem = 2 * (ng * t * CG * 4 + t * D_B * 4 + ng * CG * CG * 2) + (t + POOL_HALO) * CG * 4
    return pl.pallas_call(
        functools.partial(_pool_kernel, t=t, rows=rows, n_valid=POOL_BUF if has_buf else 0,
                          has_buf=has_buf),
        grid=(b,),
        in_specs=in_specs,
        out_specs=pl.BlockSpec((1, t, D_B), lambda i: (i, 0, 0)),
        out_shape=jax.ShapeDtypeStruct((b, t, D_B), out_dtype),
        scratch_shapes=[pltpu.VMEM((t + POOL_HALO, CG), F32)],
        compiler_params=_cparams(("parallel",), vmem),
        name="pool_mixer",
    )(*args)


GDN_HEADS_PER_STEP = 4


def _unit_lower_inverse(mat, eye, sub_diag):
    assert GDN_CHUNK // GDN_SUB == 4 and GDN_SUB == 16
    d = jnp.where(sub_diag, mat, 0.0)
    low = mat - d
    x = eye - d
    p = _hdot(d, d)
    x = x + _hdot(x, p)
    p = _hdot(p, p)
    x = x + _hdot(x, p)
    p = _hdot(p, p)
    x = x + _hdot(x, p)
    n = _hdot(x, low)
    imn = eye - n
    y = imn + _hdot(imn, _hdot(n, n))
    return _hdot(y, x)


def _gdn_kernel(*refs, t, has_state):
    c = GDN_CHUNK
    hg = GDN_HEADS_PER_STEP
    (q_ref, k_ref, v_ref, gate_ref, ba_ref, cwq_ref, cwk_ref, cwv_ref,
     alog_ref, dt_ref, gain_ref) = refs[:11]
    refs = refs[11:]
    if has_state:
        cbq_ref, cbk_ref, cbv_ref, s0_ref = refs[:4]
        refs = refs[4:]
    y_ref, s_ref = refs
    n_chunks = -(-t // c)
    padded = t % c != 0
    assert (not padded) or n_chunks == 1
    wid = hg * DH

    if has_state:
        s_ref[0] = s0_ref[0]
    else:
        s_ref[0] = jnp.zeros(s_ref.shape[1:], F32)

    ii = lax.broadcasted_iota(jnp.int32, (c, c), 0)
    jj = lax.broadcasted_iota(jnp.int32, (c, c), 1)
    tril = ii >= jj
    strict = ii > jj
    eye = (ii == jj).astype(F32)
    ones_tril = tril.astype(F32)
    sub_diag = (ii // GDN_SUB) == (jj // GDN_SUB)
    lane = lax.broadcasted_iota(jnp.int32, (c, LANE), 1)
    row_id = lax.broadcasted_iota(jnp.int32, (c, 1), 0)
    head0 = pl.program_id(1) * hg

    def halo_rows(ref, cb_ref, ci):
        if n_chunks > 1:
            prev = ref[0, pl.ds(pl.multiple_of(jnp.maximum(ci * c - SUBLANE, 0), SUBLANE), SUBLANE), :]
        else:
            prev = jnp.zeros((SUBLANE, wid), F32)
        if has_state:
            pad = jnp.zeros((SUBLANE - (GDN_CONV - 1), wid), F32)
            first = jnp.concatenate([pad, cb_ref[0]], axis=0)
        else:
            first = jnp.zeros((SUBLANE, wid), F32)
        return jnp.where(ci > 0, prev, first)

    def conv_silu(ref, cb_ref, cw_ref, ci):
        if padded:
            cur = jnp.concatenate([ref[0], jnp.zeros((c - t, wid), F32)], axis=0)
        else:
            cur = ref[0, pl.ds(pl.multiple_of(ci * c, c), c), :]
        w = jnp.concatenate([halo_rows(ref, cb_ref, ci), cur], axis=0)
        out = cur * cw_ref[GDN_CONV - 1:GDN_CONV, :]
        for i in range(1, GDN_CONV):
            out = out + w[SUBLANE - i:SUBLANE - i + c] * cw_ref[GDN_CONV - 1 - i:GDN_CONV - i, :]
        return _silu(out)

    def body(ci, carry):
        cb = (cbq_ref, cbk_ref, cbv_ref) if has_state else (None, None, None)
        qa = conv_silu(q_ref, cb[0], cwq_ref, ci)
        ka = conv_silu(k_ref, cb[1], cwk_ref, ci)
        va = conv_silu(v_ref, cb[2], cwv_ref, ci)
        if padded:
            ba = jnp.concatenate([ba_ref[0], jnp.zeros((c - t, LANE), F32)], axis=0)
            gate = jnp.concatenate([gate_ref[0], jnp.zeros((c - t, wid), F32)], axis=0)
            live = row_id < t
        else:
            ba = ba_ref[0, pl.ds(pl.multiple_of(ci * c, c), c), :]
            gate = gate_ref[0, pl.ds(pl.multiple_of(ci * c, c), c), :]
        beta_all = jax.nn.sigmoid(ba)
        z = ba + dt_ref[...]
        softplus = jnp.maximum(z, 0.0) + jnp.log1p(jnp.exp(-jnp.abs(z)))
        g_all = -jnp.exp(alog_ref[...]) * softplus
        if padded:
            beta_all = jnp.where(live, beta_all, 0.0)
            g_all = jnp.where(live, g_all, 0.0)
        outs = []
        for i in range(hg):
            sl = slice(i * DH, (i + 1) * DH)
            q = qa[:, sl]
            k = ka[:, sl]
            v = va[:, sl]
            q = q * lax.rsqrt(jnp.sum(q * q, axis=-1, keepdims=True) + EPS) * (DH ** -0.5)
            k = k * lax.rsqrt(jnp.sum(k * k, axis=-1, keepdims=True) + EPS)
            if padded:
                q = jnp.where(live, q, 0.0)
                k = jnp.where(live, k, 0.0)
                v = jnp.where(live, v, 0.0)
            beta = jnp.sum(jnp.where(lane == head0 + i, beta_all, 0.0), axis=1, keepdims=True)
            g_row = _hdot_nt((lane == H_C + head0 + i).astype(F32), g_all)
            seg = _hdot(jnp.where(tril, g_row, 0.0), ones_tril)
            gcum = seg[:, 0:1]
            g_last = seg[c - 1:c, 0:1]
            decay = jnp.where(tril, jnp.exp(jnp.where(tril, seg - g_row, 0.0)), 0.0)
            kb = k * beta
            m_mat = jnp.where(strict, _hdot_nt(kb, k) * decay, 0.0)
            a_qk = _hdot_nt(q, k) * decay
            e_cum = jnp.exp(gcum)
            rhs = jnp.concatenate([v * beta, kb * e_cum], axis=1)
            sol = _hdot(_unit_lower_inverse(m_mat, eye, sub_diag), rhs)
            u = sol[:, :DH]
            w = sol[:, DH:]
            state = s_ref[0, i]
            v_new = u - _hdot(w, state)
            o = _hdot(q * e_cum, state) + _hdot(a_qk, v_new)
            k_dec = k * jnp.exp(g_last - gcum)
            s_ref[0, i] = state * jnp.exp(g_last) + _hdot_tn(k_dec, v_new)
            outs.append(_rms_rows(o) * gain_ref[...] * _silu(gate[:, sl]))
        y = jnp.concatenate(outs, axis=1)
        if padded:
            y_ref[0] = y[:t].astype(y_ref.dtype)
        else:
            y_ref[0, pl.ds(pl.multiple_of(ci * c, c), c), :] = y.astype(y_ref.dtype)
        return carry

    lax.fori_loop(0, n_chunks, body, 0)


def gdn_mixer(proj3, ba3, conv_w, alog_vec, dt_vec, out_gain, conv_buf, state0, out_dtype):
    b, t, _ = proj3.shape
    hg = GDN_HEADS_PER_STEP
    wid = hg * DH
    has_state = state0 is not None
    col = lambda off: (lambda i, j: (i, 0, off // wid + j))
    wcol = lambda off: (lambda i, j: (0, off // wid + j))
    in_specs = [pl.BlockSpec((1, t, wid), col(OFF_C)),
                pl.BlockSpec((1, t, wid), col(OFF_C + D_C)),
                pl.BlockSpec((1, t, wid), col(OFF_C + 2 * D_C)),
                pl.BlockSpec((1, t, wid), col(OFF_G)),
                pl.BlockSpec((1, t, LANE), lambda i, j: (i, 0, 0)),
                pl.BlockSpec((GDN_CONV, wid), wcol(0)),
                pl.BlockSpec((GDN_CONV, wid), wcol(D_C)),
                pl.BlockSpec((GDN_CONV, wid), wcol(2 * D_C)),
                pl.BlockSpec((1, LANE), lambda i, j: (0, 0)),
                pl.BlockSpec((1, LANE), lambda i, j: (0, 0)),
                pl.BlockSpec((1, DH), lambda i, j: (0, 0))]
    args = [proj3, proj3, proj3, proj3, ba3, conv_w, conv_w, conv_w, alog_vec, dt_vec, out_gain]
    if has_state:
        in_specs += [pl.BlockSpec((1, GDN_CONV - 1, wid), col(0)),
                     pl.BlockSpec((1, GDN_CONV - 1, wid), col(D_C)),
                     pl.BlockSpec((1, GDN_CONV - 1, wid), col(2 * D_C)),
                     pl.BlockSpec((1, hg, DH, DH), lambda i, j: (i, j, 0, 0))]
        args += [conv_buf, conv_buf, conv_buf, state0]
    vmem = 2 * (4 * t * wid * 4 + t * LANE * 4 + t * wid * 4 + 2 * hg * DH * DH * 4) + (4 << 20)
    return pl.pallas_call(
        functools.partial(_gdn_kernel, t=t, has_state=has_state),
        grid=(b, H_C // hg),
        in_specs=in_specs,
        out_specs=[pl.BlockSpec((1, t, wid), lambda i, j: (i, 0, j)),
                   pl.BlockSpec((1, hg, DH, DH), lambda i, j: (i, j, 0, 0))],
        out_shape=[jax.ShapeDtypeStruct((b, t, D_C), out_dtype),
                   jax.ShapeDtypeStruct((b, H_C, DH, DH), F32)],
        compiler_params=_cparams(("parallel", "parallel"), vmem),
        name="gdn_mixer",
    )(*args)


def _proj_out_kernel(ya_ref, yb_ref, yc_ref, wa_ref, wb_ref, wc_ref, x_ref, o_ref):
    acc = _dot(ya_ref[...], wa_ref[...])
    acc = acc + _dot(yb_ref[...], wb_ref[...])
    acc = acc + _dot(yc_ref[...], wc_ref[...])
    o_ref[...] = x_ref[...] + acc


def proj_out(ya, yb, yc, wa, wb, wc, x):
    m, n = x.shape
    tm = min(m, 1024)
    tn = 512
    kk = D_A + D_B + D_C
    vmem = 2 * (tm * kk * 2 + kk * tn * 2 + 2 * tm * tn * 4)
    row = lambda i, j: (i, 0)
    colw = lambda i, j: (0, j)
    return pl.pallas_call(
        _proj_out_kernel,
        grid=(m // tm, n // tn),
        in_specs=[pl.BlockSpec((tm, D_A), row), pl.BlockSpec((tm, D_B), row),
                  pl.BlockSpec((tm, D_C), row),
                  pl.BlockSpec((D_A, tn), colw), pl.BlockSpec((D_B, tn), colw),
                  pl.BlockSpec((D_C, tn), colw),
                  pl.BlockSpec((tm, tn), lambda i, j: (i, j))],
        out_specs=pl.BlockSpec((tm, tn), lambda i, j: (i, j)),
        out_shape=jax.ShapeDtypeStruct((m, n), F32),
        compiler_params=_cparams(("parallel", "parallel"), vmem),
        name="proj_out",
    )(ya, yb, yc, wa, wb, wc, x)


FFN_TN = 256


def _ffn_conv(cur, ext_ref, cw_ref, b_ref, rows):
    out = cur * cw_ref[FFN_CONV - 1:FFN_CONV, :] + b_ref[...]
    for i in range(1, FFN_CONV):
        out = out + ext_ref[SUBLANE - i:SUBLANE - i + rows, :] * cw_ref[FFN_CONV - 1 - i:FFN_CONV - i, :]
    return out


def _ffn_up_prompt_kernel(x_ref, wg_ref, wv_ref, cwg_ref, cwv_ref, bg_ref, bv_ref,
                          act_ref, tg_ref, tv_ref, extg_ref, extv_ref, halo_ref, *, tiles_per_seq):
    tm = x_ref.shape[0]
    mi = pl.program_id(0)
    ni = pl.program_id(1)
    x = x_ref[...]
    ug = _dot(x, wg_ref[...])
    uv = _dot(x, wv_ref[...])

    @pl.when(mi % tiles_per_seq == 0)
    def _():
        halo_ref[ni] = jnp.zeros(halo_ref.shape[1:], F32)

    extg_ref[0:SUBLANE, :] = halo_ref[ni, 0]
    extv_ref[0:SUBLANE, :] = halo_ref[ni, 1]
    extg_ref[SUBLANE:, :] = ug
    extv_ref[SUBLANE:, :] = uv
    tail_g = ug[tm - SUBLANE:]
    tail_v = uv[tm - SUBLANE:]
    halo_ref[ni, 0] = tail_g
    halo_ref[ni, 1] = tail_v
    tg_ref[0] = tail_g
    tv_ref[0] = tail_v
    gt = _ffn_conv(ug, extg_ref, cwg_ref, bg_ref, tm)
    vl = _ffn_conv(uv, extv_ref, cwv_ref, bv_ref, tm)
    act_ref[...] = (_silu(gt) * vl).astype(act_ref.dtype)


def ffn_up_prompt(xn, w_up, conv_w, conv_b, seq):
    m, k = xn.shape
    d_ff = w_up.shape[1] // 2
    tn = FFN_TN
    nt = d_ff // tn
    tm = min(seq, 1024)
    mt = m // tm
    lo = lambda i, j: (0, j)
    hi = lambda i, j: (0, nt + j)
    vmem = (2 * (tm * k * 2 + 2 * k * tn * 2 + tm * tn * 2) + 2 * (tm + SUBLANE) * tn * 4
            + nt * 2 * SUBLANE * tn * 4 + 6 * tm * tn * 4)
    return pl.pallas_call(
        functools.partial(_ffn_up_prompt_kernel, tiles_per_seq=seq // tm),
        grid=(mt, nt),
        in_specs=[pl.BlockSpec((tm, k), lambda i, j: (i, 0)),
                  pl.BlockSpec((k, tn), lo), pl.BlockSpec((k, tn), hi),
                  pl.BlockSpec((FFN_CONV, tn), lo), pl.BlockSpec((FFN_CONV, tn), hi),
                  pl.BlockSpec((1, tn), lo), pl.BlockSpec((1, tn), hi)],
        out_specs=[pl.BlockSpec((tm, tn), lambda i, j: (i, j)),
                   pl.BlockSpec((1, SUBLANE, tn), lambda i, j: (i, 0, j)),
                   pl.BlockSpec((1, SUBLANE, tn), lambda i, j: (i, 0, j))],
        out_shape=[jax.ShapeDtypeStruct((m, d_ff), BF16),
                   jax.ShapeDtypeStruct((mt, SUBLANE, d_ff), F32),
                   jax.ShapeDtypeStruct((mt, SUBLANE, d_ff), F32)],
        scratch_shapes=[pltpu.VMEM((tm + SUBLANE, tn), F32),
                        pltpu.VMEM((tm + SUBLANE, tn), F32),
                        pltpu.VMEM((nt, 2, SUBLANE, tn), F32)],
        compiler_params=_cparams(("arbitrary", "arbitrary"), vmem),
        name="ffn_up_prompt",
    )(xn, w_up, w_up, conv_w, conv_w, conv_b, conv_b)


def _ffn_up_sample_kernel(x_ref, wg_ref, wv_ref, cwg_ref, cwv_ref, bg_ref, bv_ref, sg_ref, sv_ref,
                          act_ref, ug_ref, uv_ref, ext_ref, *, t):
    nb = x_ref.shape[0] // t
    assert t == SUBLANE
    x = x_ref[...]
    ug = _dot(x, wg_ref[...])
    uv = _dot(x, wv_ref[...])
    ug_ref[...] = ug
    uv_ref[...] = uv

    def conv(u, st_ref, cw_ref, b_ref):
        outs = []
        for bi in range(nb):
            cur = u[bi * t:(bi + 1) * t]
            ext_ref[SUBLANE - (FFN_CONV - 1):SUBLANE, :] = st_ref[bi]
            ext_ref[SUBLANE:, :] = cur
            outs.append(_ffn_conv(cur, ext_ref, cw_ref, b_ref, t))
        return jnp.concatenate(outs, axis=0)

    gt = conv(ug, sg_ref, cwg_ref, bg_ref)
    vl = conv(uv, sv_ref, cwv_ref, bv_ref)
    act_ref[...] = (_silu(gt) * vl).astype(act_ref.dtype)


def ffn_up_sample(xn, w_up, conv_w, conv_b, conv_state, t):
    m, k = xn.shape
    d_ff = w_up.shape[1] // 2
    tn = FFN_TN
    nt = d_ff // tn
    nb = m // t
    lo = lambda j: (0, j)
    hi = lambda j: (0, nt + j)
    vmem = 2 * (m * k * 2 + 2 * k * tn * 2 + 4 * m * tn * 4) + (2 << 20)
    return pl.pallas_call(
        functools.partial(_ffn_up_sample_kernel, t=t),
        grid=(nt,),
        in_specs=[pl.BlockSpec((m, k), lambda j: (0, 0)),
                  pl.BlockSpec((k, tn), lo), pl.BlockSpec((k, tn), hi),
                  pl.BlockSpec((FFN_CONV, tn), lo), pl.BlockSpec((FFN_CONV, tn), hi),
                  pl.BlockSpec((1, tn), lo), pl.BlockSpec((1, tn), hi),
                  pl.BlockSpec((nb, FFN_CONV - 1, tn), lambda j: (0, 0, j)),
                  pl.BlockSpec((nb, FFN_CONV - 1, tn), lambda j: (0, 0, nt + j))],
        out_specs=[pl.BlockSpec((m, tn), lo), pl.BlockSpec((m, tn), lo), pl.BlockSpec((m, tn), lo)],
        out_shape=[jax.ShapeDtypeStruct((m, d_ff), BF16),
                   jax.ShapeDtypeStruct((m, d_ff), F32),
                   jax.ShapeDtypeStruct((m, d_ff), F32)],
        scratch_shapes=[pltpu.VMEM((SUBLANE + t, tn), F32)],
        compiler_params=_cparams(("parallel",), vmem),
        name="ffn_up_sample",
    )(xn, w_up, w_up, conv_w, conv_w, conv_b, conv_b, conv_state, conv_state)


def _ffn_down_kernel(a_ref, w_ref, h_ref, o_ref):
    o_ref[...] = h_ref[...] + _dot(a_ref[...], w_ref[...])


def ffn_down_proj(act, w_down, h):
    m, k = act.shape
    n = w_down.shape[1]
    tm = min(m, 512)
    tn = 512
    vmem = 2 * (tm * k * 2 + k * tn * 2 + 2 * tm * tn * 4)
    return pl.pallas_call(
        _ffn_down_kernel,
        grid=(m // tm, n // tn),
        in_specs=[pl.BlockSpec((tm, k), lambda i, j: (i, 0)),
                  pl.BlockSpec((k, tn), lambda i, j: (0, j)),
                  pl.BlockSpec((tm, tn), lambda i, j: (i, j))],
        out_specs=pl.BlockSpec((tm, tn), lambda i, j: (i, j)),
        out_shape=jax.ShapeDtypeStruct((m, n), F32),
        compiler_params=_cparams(("parallel", "parallel"), vmem),
        name="ffn_down",
    )(act, w_down, h)


def _t5_bucket(dist):
    max_exact = REL_BUCKETS // 2
    d = jnp.maximum(dist, 1).astype(F32)
    large = max_exact + (jnp.log(d / max_exact) / math.log(REL_MAX_DIST / max_exact)
                         * (REL_BUCKETS - max_exact)).astype(jnp.int32)
    large = jnp.minimum(large, REL_BUCKETS - 1)
    return jnp.where(dist < max_exact, dist, large)


def _prompt_bias_blocks(rel_bias):
    qi = np.arange(BLK)[:, None]
    ki = np.arange(2 * BLK)[None, :]
    rel = qi + BLK - ki
    out = []
    for window, dil in A_BRANCHES:
        nj = window // dil + 1
        valid = (rel >= 0) & (rel < nj)
        buckets = _t5_bucket(jnp.asarray(np.clip(rel, 0, nj - 1) * dil, jnp.int32))
        bias = rel_bias[buckets].astype(F32)
        out.append(jnp.where(jnp.asarray(valid)[..., None], bias, NEG_INF).transpose(2, 0, 1))
    return jnp.stack(out)


def _branch_count(dist):
    cnt = np.zeros(dist.shape, np.float32)
    for window, dil in A_BRANCHES:
        cnt += ((dist >= 0) & (dist % dil == 0) & (dist // dil <= window // dil)).astype(np.float32)
    return cnt


def _sample_bias_tables(rel_bias, t, n_past):
    tq = np.arange(t)[:, None]
    d_cache = n_past + tq - np.arange(n_past)[None, :]
    d_new = tq - np.arange(BLK)[None, :]
    d_new = np.where(np.arange(BLK)[None, :] < t, d_new, -1)
    tables = []
    for dist in (d_cache, d_new):
        cnt = _branch_count(dist)
        bias = rel_bias[_t5_bucket(jnp.asarray(np.maximum(dist, 0), jnp.int32))].astype(F32)
        bias = jnp.where(jnp.asarray(cnt > 0)[..., None], bias, NEG_INF).transpose(2, 0, 1)
        tables += [bias, jnp.asarray(cnt)]
    return tables


def _layer(x, lw, *, batch, seq, layer, cache4=None, pool_buf=None, gconv_buf=None, gstate=None,
           fconv_buf=None, bias_tabs=None):
    sample = cache4 is not None
    mix_dtype = F32 if sample else BF16
    xn = rmsnorm_cast(x, lw['norm_mix'])
    proj, ba = proj_in(xn, lw['w_in_main'], lw['w_in_tail'])
    proj3 = proj.reshape(batch, seq, N_MAIN)
    ba3 = ba.reshape(batch, seq, LANE)
    qg = lw['a_q_norm'].reshape(1, DH)
    kg = lw['a_k_norm'].reshape(1, DH)
    og = lw['a_out_norm'].reshape(H_A, DH)
    if sample:
        ya, kn = attn_sample(proj3, cache4, layer, qg, kg, og, *bias_tabs)
    else:
        ya, kn = attn_prompt(proj3, qg, kg, og, bias_tabs)
    kv_rows = jnp.concatenate([kn, proj3[:, :, OFF_V:OFF_V + D_A]], axis=-1)
    kv_rows = kv_rows.reshape(batch, seq, 2, H_A, DH)
    yb = pool_mixer(proj3, pool_buf, lw['pool_w'], lw['pool_scale'].reshape(1, D_B), mix_dtype)
    pu = proj3[:, :, OFF_P:OFF_P + D_B]
    if sample:
        pool_new = jnp.concatenate([pool_buf, pu], axis=1)[:, -POOL_BUF:]
    else:
        pool_new = pu[:, -POOL_BUF:]
    yc, gstate_new = gdn_mixer(proj3, ba3, lw['gdn_conv_w'], lw['alog_vec'], lw['dt_vec'],
                               lw['gdn_out_norm'].reshape(1, DH), gconv_buf, gstate, mix_dtype)
    gconv_new = proj3[:, -(GDN_CONV - 1):, OFF_C:OFF_C + 3 * D_C]
    m = batch * seq
    h = proj_out(ya.reshape(m, D_A).astype(BF16), yb.reshape(m, D_B).astype(BF16),
                 yc.reshape(m, D_C).astype(BF16), lw['w_out_a'], lw['w_out_b'], lw['w_out_c'], x)
    hn = rmsnorm_cast(h, lw['norm_ffn'])
    if sample:
        act, ug, uv = ffn_up_sample(hn, lw['ffn_up'], lw['ffn_conv_w'], lw['ffn_conv_b'], fconv_buf, seq)
        up = jnp.concatenate([ug, uv], axis=-1).reshape(batch, seq, -1)
        fconv_new = up[:, -(FFN_CONV - 1):]
    else:
        act, tg, tv = ffn_up_prompt(hn, lw['ffn_up'], lw['ffn_conv_w'], lw['ffn_conv_b'], seq)
        tails = jnp.concatenate([tg, tv], axis=-1)
        tiles_per_seq = tails.shape[0] // batch
        fconv_new = tails[tiles_per_seq - 1::tiles_per_seq, -(FFN_CONV - 1):]
    y = ffn_down_proj(act, lw['ffn_down'], h)
    return y, (kv_rows, pool_new, gconv_new, gstate_new, fconv_new)


def kernel(x_prompt, x_sample, cache_attn_kv, state_pool, state_gdn_conv, state_gdn, state_ffn_conv,
           rel_bias, norm_mix, w_in, a_q_norm, a_k_norm, a_out_norm, pool_w, pool_scale,
           gdn_conv_w, gdn_a_log, gdn_dt_bias, gdn_out_norm, w_out, norm_ffn,
           ffn_up, ffn_conv_w, ffn_conv_b, ffn_down):
    depth = w_in.shape[0]
    bp, sp, d_model = x_prompt.shape
    bs, ss, _ = x_sample.shape
    n_past = cache_attn_kv.shape[2]
    assert w_in.shape[2] == N_MAIN + N_TAIL and w_out.shape[1] == D_A + D_B + D_C

    prompt_bias = _prompt_bias_blocks(rel_bias)
    sample_tabs = _sample_bias_tables(rel_bias, ss, n_past)
    cache4 = cache_attn_kv.reshape(depth, bs, n_past, 2 * D_A)

    xp = x_prompt.reshape(bp * sp, d_model)
    xs = x_sample.reshape(bs * ss, d_model)
    p_out = [[] for _ in range(5)]
    s_out = [[] for _ in range(5)]
    for l in range(depth):
        tail = jnp.pad(w_in[l, :, N_MAIN:], ((0, 0), (0, LANE - N_TAIL)))
        lw = {
            'norm_mix': norm_mix[l], 'norm_ffn': norm_ffn[l],
            'w_in_main': w_in[l, :, :N_MAIN].astype(BF16), 'w_in_tail': tail.astype(BF16),
            'a_q_norm': a_q_norm[l], 'a_k_norm': a_k_norm[l], 'a_out_norm': a_out_norm[l],
            'pool_w': pool_w[l].astype(BF16), 'pool_scale': pool_scale[l],
            'gdn_conv_w': gdn_conv_w[l],
            'alog_vec': jnp.pad(gdn_a_log[l], (H_C, LANE - 2 * H_C)).reshape(1, LANE),
            'dt_vec': jnp.pad(gdn_dt_bias[l], (H_C, LANE - 2 * H_C)).reshape(1, LANE),
            'gdn_out_norm': gdn_out_norm[l],
            'w_out_a': w_out[l, :D_A].astype(BF16),
            'w_out_b': w_out[l, D_A:D_A + D_B].astype(BF16),
            'w_out_c': w_out[l, D_A + D_B:].astype(BF16),
            'ffn_up': ffn_up[l].astype(BF16), 'ffn_conv_w': ffn_conv_w[l],
            'ffn_conv_b': ffn_conv_b[l].reshape(1, -1), 'ffn_down': ffn_down[l].astype(BF16),
        }
        xp, outs = _layer(xp, lw, batch=bp, seq=sp, layer=l, bias_tabs=prompt_bias)
        for acc, o in zip(p_out, outs):
            acc.append(o)
        xs, outs = _layer(xs, lw, batch=bs, seq=ss, layer=l, cache4=cache4, pool_buf=state_pool[l],
                          gconv_buf=state_gdn_conv[l], gstate=state_gdn[l],
                          fconv_buf=state_ffn_conv[l], bias_tabs=sample_tabs)
        for acc, o in zip(s_out, outs):
            acc.append(o)
    res = [xp.reshape(bp, sp, d_model), xs.reshape(bs, ss, d_model)]
    for po, so in zip(p_out, s_out):
        res += [jnp.stack(po), jnp.stack(so)]
    return tuple(res)
```

```python
import functools
import math

import numpy as np
import jax
import jax.numpy as jnp
from jax import lax
from jax.experimental import pallas as pl
from jax.experimental.pallas import tpu as pltpu

F32 = jnp.float32
BF16 = jnp.bfloat16
HIGHEST = lax.Precision.HIGHEST

DH = 128
H_A = 12
H_C = 12
A_BRANCHES = ((128, 1), (512, 4), (2048, 16))
BLK = 128
REL_BUCKETS = 32
REL_MAX_DIST = 2048
POOL_WINDOWS = (2, 4, 8, 16)
CG = 256
POOL_BUF = 15
GDN_CONV = 4
GDN_CHUNK = 64
GDN_SUB = 16
FFN_CONV = 3
EPS = 1e-6
NEG_INF = -1e30

D_A = H_A * DH
D_B = len(POOL_WINDOWS) * CG
D_C = H_C * DH
OFF_Q, OFF_K, OFF_V = 0, D_A, 2 * D_A
OFF_P = 3 * D_A
OFF_C = OFF_P + D_B
OFF_G = OFF_C + 3 * D_C
N_MAIN = OFF_G + D_C
N_TAIL = 2 * H_C

LANE = 128
SUBLANE = 8
VMEM_CAP = 56 * 1024 * 1024


def _cparams(sem, vmem_bytes):
    limit = int(min(max(vmem_bytes * 5 // 4 + (2 << 20), 16 << 20), VMEM_CAP))
    return pltpu.CompilerParams(dimension_semantics=sem, vmem_limit_bytes=limit)


def _rms_rows(x):
    return x * lax.rsqrt(jnp.mean(x * x, axis=-1, keepdims=True) + EPS)


def _silu(x):
    return x * jax.nn.sigmoid(x)


def _dot(a, b):
    return jnp.dot(a, b, preferred_element_type=F32)


def _dot_nt(a, b):
    return lax.dot_general(a, b, (((1,), (1,)), ((), ())), preferred_element_type=F32)


def _hdot(a, b):
    return jnp.dot(a, b, preferred_element_type=F32, precision=HIGHEST)


def _hdot_nt(a, b):
    return lax.dot_general(a, b, (((1,), (1,)), ((), ())), preferred_element_type=F32,
                           precision=HIGHEST)


def _hdot_tn(a, b):
    return lax.dot_general(a, b, (((0,), (0,)), ((), ())), preferred_element_type=F32,
                           precision=HIGHEST)


def _rmsnorm_kernel(x_ref, g_ref, o_ref):
    o_ref[...] = (_rms_rows(x_ref[...]) * g_ref[...]).astype(o_ref.dtype)


def rmsnorm_cast(x, gain):
    m, d = x.shape
    tm = min(m, 256)
    return pl.pallas_call(
        _rmsnorm_kernel,
        grid=(m // tm,),
        in_specs=[pl.BlockSpec((tm, d), lambda i: (i, 0)),
                  pl.BlockSpec((1, d), lambda i: (0, 0))],
        out_specs=pl.BlockSpec((tm, d), lambda i: (i, 0)),
        out_shape=jax.ShapeDtypeStruct((m, d), BF16),
        compiler_params=_cparams(("parallel",), 2 * tm * d * 6),
        name="rmsnorm_cast",
    )(x, gain.reshape(1, d))


def _proj_in_kernel(x_ref, w_ref, wt_ref, o_ref, ot_ref):
    o_ref[...] = _dot(x_ref[...], w_ref[...])

    @pl.when(pl.program_id(1) == 0)
    def _():
        ot_ref[...] = _dot(x_ref[...], wt_ref[...])


def proj_in(xn, w_main, w_tail):
    m, k = xn.shape
    n = w_main.shape[1]
    tm = min(m, 1024)
    tn = 512
    vmem = 2 * (tm * k * 2 + k * tn * 2 + tm * tn * 4 + k * LANE * 2 + tm * LANE * 4)
    return pl.pallas_call(
        _proj_in_kernel,
        grid=(m // tm, n // tn),
        in_specs=[pl.BlockSpec((tm, k), lambda i, j: (i, 0)),
                  pl.BlockSpec((k, tn), lambda i, j: (0, j)),
                  pl.BlockSpec((k, LANE), lambda i, j: (0, 0))],
        out_specs=[pl.BlockSpec((tm, tn), lambda i, j: (i, j)),
                   pl.BlockSpec((tm, LANE), lambda i, j: (i, 0))],
        out_shape=[jax.ShapeDtypeStruct((m, n), F32),
                   jax.ShapeDtypeStruct((m, LANE), F32)],
        compiler_params=_cparams(("parallel", "arbitrary"), vmem),
        name="proj_in",
    )(xn, w_main, w_tail)


ATTN_BLOCKS_PER_TRIP = 4


def _strided_rows(start, dil):
    return pl.ds(start, BLK) if dil == 1 else pl.ds(start, BLK, stride=dil)


def _attn_prompt_kernel(q_ref, k_ref, v_ref, qg_ref, kg_ref, og_ref, bias_ref,
                        y_ref, kv_ref, qs_ref, ks_ref, acc_ref, m_ref, l_ref):
    seq = q_ref.shape[1]
    qs_ref[...] = _rms_rows(q_ref[0]) * qg_ref[...] * (DH ** -0.5)
    ks_ref[...] = _rms_rows(k_ref[0]) * kg_ref[...]
    kv_ref[0, :, 0:DH] = ks_ref[...]
    kv_ref[0, :, DH:] = v_ref[0]
    acc_ref[...] = jnp.zeros(acc_ref.shape, F32)
    l_ref[...] = jnp.zeros(l_ref.shape, F32)
    m_ref[...] = jnp.full(m_ref.shape, NEG_INF, F32)

    for bi, (window, dil) in enumerate(A_BRANCHES):
        assert window // dil == BLK and seq % (dil * BLK) == 0
        nb = seq // (dil * BLK)

        assert (dil * nb) % ATTN_BLOCKS_PER_TRIP == 0

        def body(it, carry, bi=bi, dil=dil, nb=nb):
            us = range(ATTN_BLOCKS_PER_TRIP)
            item = [it * ATTN_BLOCKS_PER_TRIP + u for u in us]
            r = [x // nb for x in item]
            n = [item[u] - r[u] * nb for u in us]
            rows = [_strided_rows(n[u] * (BLK * dil) + r[u], dil) for u in us]
            qb = [qs_ref[rw, :].astype(BF16) for rw in rows]
            kc = [ks_ref[rw, :] for rw in rows]
            vc = [v_ref[0, rw, :] for rw in rows]
            bias_cur = bias_ref[bi, 0, :, BLK:]
            if nb > 1:
                prows = [_strided_rows(jnp.maximum(n[u] - 1, 0) * (BLK * dil) + r[u], dil) for u in us]
                kk = [jnp.concatenate([ks_ref[prows[u], :], kc[u]], axis=0).astype(BF16) for u in us]
                vv = [jnp.concatenate([v_ref[0, prows[u], :], vc[u]], axis=0).astype(BF16) for u in us]
                bias_prev = bias_ref[bi, 0, :, :BLK]
                bias = [jnp.concatenate([jnp.where(n[u] > 0, bias_prev, NEG_INF), bias_cur], axis=1)
                        for u in us]
            else:
                kk = [x.astype(BF16) for x in kc]
                vv = [x.astype(BF16) for x in vc]
                bias = [bias_cur for _ in us]
            s = [_dot_nt(qb[u], kk[u]) + bias[u] for u in us]
            m_old = [m_ref[rw, :] for rw in rows]
            m_new = [jnp.maximum(m_old[u], jnp.max(s[u], axis=1, keepdims=True)) for u in us]
            p = [jnp.exp(s[u] - m_new[u][:, :1]) for u in us]
            alpha = [jnp.exp(m_old[u] - m_new[u]) for u in us]
            pv = [_dot(p[u].astype(BF16), vv[u]) for u in us]
            for u in us:
                l_ref[rows[u], :] = alpha[u] * l_ref[rows[u], :] + jnp.sum(p[u], axis=1, keepdims=True)
                acc_ref[rows[u], :] = alpha[u] * acc_ref[rows[u], :] + pv[u]
                m_ref[rows[u], :] = m_new[u]
            return carry

        lax.fori_loop(0, dil * nb // ATTN_BLOCKS_PER_TRIP, body, 0)

    o = acc_ref[...] / l_ref[...]
    y_ref[0] = (_rms_rows(o) * og_ref[0]).astype(y_ref.dtype)


def attn_prompt(proj3, q_gain, k_gain, o_gain, bias_blocks):
    b, s, _ = proj3.shape
    col = lambda off: (lambda i, h: (i, 0, off // DH + h))
    vmem = 2 * (3 * s * DH * 4 + s * DH * 2 + s * DH * 4 + 3 * BLK * 2 * BLK * 4) + 5 * s * DH * 4
    return pl.pallas_call(
        _attn_prompt_kernel,
        grid=(b, H_A),
        in_specs=[pl.BlockSpec((1, s, DH), col(OFF_Q)),
                  pl.BlockSpec((1, s, DH), col(OFF_K)),
                  pl.BlockSpec((1, s, DH), col(OFF_V)),
                  pl.BlockSpec((1, DH), lambda i, h: (0, 0)),
                  pl.BlockSpec((1, DH), lambda i, h: (0, 0)),
                  pl.BlockSpec((1, 1, DH), lambda i, h: (h, 0, 0)),
                  pl.BlockSpec((len(A_BRANCHES), 1, BLK, 2 * BLK), lambda i, h: (0, h, 0, 0))],
        out_specs=[pl.BlockSpec((1, s, DH), lambda i, h: (i, 0, h)),
                   pl.BlockSpec((1, s, 2 * DH), lambda i, h: (i, 0, h))],
        out_shape=[jax.ShapeDtypeStruct((b, s, D_A), BF16),
                   jax.ShapeDtypeStruct((b, s, 2 * D_A), F32)],
        scratch_shapes=[pltpu.VMEM((s, DH), F32)] * 5,
        compiler_params=_cparams(("parallel", "parallel"), vmem),
        name="attn_prompt",
    )(proj3, proj3, proj3, q_gain, k_gain, o_gain.reshape(H_A, 1, DH), bias_blocks)


def _attn_sample_kernel(q_ref, k_ref, v_ref, kc_ref, vc_ref, qg_ref, kg_ref, og_ref,
                        bc_ref, cc_ref, bn_ref, cn_ref, y_ref, kv_ref, kpad_ref, vpad_ref):
    t = q_ref.shape[1]
    qn = (_rms_rows(q_ref[0]) * qg_ref[...] * (DH ** -0.5)).astype(BF16)
    kn = _rms_rows(k_ref[0]) * kg_ref[...]
    kv_ref[0, :, 0:DH] = kn
    kv_ref[0, :, DH:] = v_ref[0]
    kpad_ref[...] = jnp.zeros(kpad_ref.shape, F32)
    vpad_ref[...] = jnp.zeros(vpad_ref.shape, F32)
    kpad_ref[0:t, :] = kn
    vpad_ref[0:t, :] = v_ref[0]
    s_c = _dot_nt(qn, kc_ref[0, 0, 0, 0].astype(BF16)) + bc_ref[0]
    s_n = _dot_nt(qn, kpad_ref[...].astype(BF16)) + bn_ref[0]
    m = jnp.maximum(jnp.max(s_c, axis=1, keepdims=True), jnp.max(s_n, axis=1, keepdims=True))
    p_c = jnp.exp(s_c - m) * cc_ref[...]
    p_n = jnp.exp(s_n - m) * cn_ref[...]
    den = jnp.sum(p_c, axis=1, keepdims=True) + jnp.sum(p_n, axis=1, keepdims=True)
    o = (_dot(p_c.astype(BF16), vc_ref[0, 0, 0, 0].astype(BF16))
         + _dot(p_n.astype(BF16), vpad_ref[...].astype(BF16))) / den
    y_ref[0] = _rms_rows(o) * og_ref[0]


def attn_sample(proj3, cache_hm, layer, q_gain, k_gain, o_gain, bias_c, cnt_c, bias_n, cnt_n):
    b, t, _ = proj3.shape
    n_past = cache_hm.shape[4]
    col = lambda off: (lambda i, h: (i, 0, off // DH + h))
    vmem = 2 * (2 * n_past * DH * 4 + 2 * t * n_past * 4) + 2 * BLK * DH * 4 + (1 << 20)
    return pl.pallas_call(
        _attn_sample_kernel,
        grid=(b, H_A),
        in_specs=[pl.BlockSpec((1, t, DH), col(OFF_Q)),
                  pl.BlockSpec((1, t, DH), col(OFF_K)),
                  pl.BlockSpec((1, t, DH), col(OFF_V)),
                  pl.BlockSpec((1, 1, 1, 1, n_past, DH), lambda i, h: (layer, i, 0, h, 0, 0)),
                  pl.BlockSpec((1, 1, 1, 1, n_past, DH), lambda i, h: (layer, i, 1, h, 0, 0)),
                  pl.BlockSpec((1, DH), lambda i, h: (0, 0)),
                  pl.BlockSpec((1, DH), lambda i, h: (0, 0)),
                  pl.BlockSpec((1, 1, DH), lambda i, h: (h, 0, 0)),
                  pl.BlockSpec((1, t, n_past), lambda i, h: (h, 0, 0)),
                  pl.BlockSpec((t, n_past), lambda i, h: (0, 0)),
                  pl.BlockSpec((1, t, BLK), lambda i, h: (h, 0, 0)),
                  pl.BlockSpec((t, BLK), lambda i, h: (0, 0))],
        out_specs=[pl.BlockSpec((1, t, DH), lambda i, h: (i, 0, h)),
                   pl.BlockSpec((1, t, 2 * DH), lambda i, h: (i, 0, h))],
        out_shape=[jax.ShapeDtypeStruct((b, t, D_A), F32),
                   jax.ShapeDtypeStruct((b, t, 2 * D_A), F32)],
        scratch_shapes=[pltpu.VMEM((BLK, DH), F32)] * 2,
        compiler_params=_cparams(("parallel", "parallel"), vmem),
        name="attn_sample",
    )(proj3, proj3, proj3, cache_hm, cache_hm, q_gain, k_gain, o_gain.reshape(H_A, 1, DH),
      bias_c, cnt_c, bias_n, cnt_n)


POOL_HALO = 16


def _pool_kernel(*refs, t, rows, n_valid, has_buf):
    ng = len(POOL_WINDOWS)
    u_refs = refs[:ng]
    refs = refs[ng:]
    if has_buf:
        buf_refs = refs[:ng]
        refs = refs[ng:]
    w_ref, scale_ref, y_ref, ext_ref = refs
    for g, win in enumerate(POOL_WINDOWS):
        ext_ref[0:POOL_HALO, :] = jnp.zeros((POOL_HALO, CG), F32)
        if has_buf:
            ext_ref[POOL_HALO - POOL_BUF:POOL_HALO, :] = buf_refs[g][0]
        ext_ref[POOL_HALO:, :] = u_refs[g][0]

        def chunk(ci, carry, g=g, win=win):
            base = pl.multiple_of(ci * rows, SUBLANE)
            w = ext_ref[pl.ds(base, rows + POOL_HALO), :]
            u = w[POOL_HALO:]
            tot = u
            for i in range(1, win):
                tot = tot + w[POOL_HALO - i:POOL_HALO - i + rows]
            pos = base + lax.broadcasted_iota(jnp.int32, (rows, 1), 0)
            cnt = jnp.minimum(win, n_valid + pos + 1).astype(F32)
            d = tot / cnt - u
            y = _rms_rows(_dot(d.astype(BF16), w_ref[g])) * scale_ref[:, g * CG:(g + 1) * CG]
            y_ref[0, pl.ds(base, rows), g * CG:(g + 1) * CG] = y.astype(y_ref.dtype)
            return carry

        lax.fori_loop(0, t // rows, chunk, 0)


def pool_mixer(proj3, bufs, w_pool, scale, out_dtype):
    b, t, _ = proj3.shape
    ng = len(POOL_WINDOWS)
    rows = min(t, 256)
    has_buf = bufs is not None
    in_specs = [pl.BlockSpec((1, t, CG), (lambda i, g=g: (i, 0, OFF_P // CG + g))) for g in range(ng)]
    args = [proj3] * ng
    if has_buf:
        in_specs += [pl.BlockSpec((1, POOL_BUF, CG), (lambda i, g=g: (i, 0, g))) for g in range(ng)]
        args += [bufs] * ng
    in_specs += [pl.BlockSpec((ng, CG, CG), lambda i: (0, 0, 0)),
                 pl.BlockSpec((1, D_B), lambda i: (0, 0))]
    args += [w_pool, scale]
    vmem = 2 * (ng * t * CG * 4 + t * D_B * 4 + ng * CG * CG * 2) + (t + POOL_HALO) * CG * 4
    return pl.pallas_call(
        functools.partial(_pool_kernel, t=t, rows=rows, n_valid=POOL_BUF if has_buf else 0,
                          has_buf=has_buf),
        grid=(b,),
        in_specs=in_specs,
        out_specs=pl.BlockSpec((1, t, D_B), lambda i: (i, 0, 0)),
        out_shape=jax.ShapeDtypeStruct((b, t, D_B), out_dtype),
        scratch_shapes=[pltpu.VMEM((t + POOL_HALO, CG), F32)],
        compiler_params=_cparams(("parallel",), vmem),
        name="pool_mixer",
    )(*args)


GDN_HEADS_PER_STEP = 4


def _split_bf16(a):
    hi = a.astype(BF16)
    return hi, (a - hi.astype(F32)).astype(BF16)


def _dot3(a, b):
    return _dot(a[0], b[0]) + (_dot(a[0], b[1]) + _dot(a[1], b[0]))


def _cumsum_rows(x):
    n = x.shape[0]
    row = lax.broadcasted_iota(jnp.int32, (n, 1), 0)
    k = 1
    while k < n:
        x = x + jnp.where(row >= k, pltpu.roll(x, k, axis=0), 0.0)
        k *= 2
    return x


def _unit_lower_solve(mats, rhss, eye, sub_diag):
    assert GDN_CHUNK // GDN_SUB == 4 and GDN_SUB == 16
    idx = range(len(mats))
    d = [jnp.where(sub_diag, m, 0.0) for m in mats]
    low = [_split_bf16(m - di) for m, di in zip(mats, d)]
    rs = [_split_bf16(r) for r in rhss]
    x = [eye - di for di in d]
    ps = [_split_bf16(di) for di in d]
    ps = [_split_bf16(_dot3(ps[i], ps[i])) for i in idx]
    for _ in range(2):
        xs = [_split_bf16(xi) for xi in x]
        x = [x[i] + _dot3(xs[i], ps[i]) for i in idx]
        ps = [_split_bf16(_dot3(ps[i], ps[i])) for i in idx]
    xs = [_split_bf16(xi) for xi in x]
    x = [x[i] + _dot3(xs[i], ps[i]) for i in idx]
    xs = [_split_bf16(xi) for xi in x]
    n = [_dot3(xs[i], low[i]) for i in idx]
    xr = [_dot3(xs[i], rs[i]) for i in idx]
    ns = [_split_bf16(ni) for ni in n]
    n2 = [_split_bf16(_dot3(ns[i], ns[i])) for i in idx]
    imn = [eye - ni for ni in n]
    y = [imn[i] + _dot3(_split_bf16(imn[i]), n2[i]) for i in idx]
    return [_dot3(_split_bf16(y[i]), _split_bf16(xr[i])) for i in idx]


def _gdn_kernel(*refs, t, has_state):
    c = GDN_CHUNK
    hg = GDN_HEADS_PER_STEP
    (q_ref, k_ref, v_ref, gate_ref, ba_ref, cwq_ref, cwk_ref, cwv_ref,
     alog_ref, dt_ref, gain_ref) = refs[:11]
    refs = refs[11:]
    if has_state:
        cbq_ref, cbk_ref, cbv_ref, s0_ref = refs[:4]
        refs = refs[4:]
    y_ref, s_ref = refs
    n_chunks = -(-t // c)
    padded = t % c != 0
    assert (not padded) or n_chunks == 1
    wid = hg * DH
    heads = range(hg)

    if has_state:
        s_ref[0] = s0_ref[0]
    else:
        s_ref[0] = jnp.zeros(s_ref.shape[1:], F32)

    ii = lax.broadcasted_iota(jnp.int32, (c, c), 0)
    jj = lax.broadcasted_iota(jnp.int32, (c, c), 1)
    tril = ii >= jj
    strict = ii > jj
    eye = (ii == jj).astype(F32)
    sub_diag = (ii // GDN_SUB) == (jj // GDN_SUB)
    row_id = lax.broadcasted_iota(jnp.int32, (c, 1), 0)

    def halo_rows(ref, cb_ref, ci):
        if n_chunks > 1:
            prev = ref[0, pl.ds(pl.multiple_of(jnp.maximum(ci * c - SUBLANE, 0), SUBLANE), SUBLANE), :]
        else:
            prev = jnp.zeros((SUBLANE, wid), F32)
        if has_state:
            pad = jnp.zeros((SUBLANE - (GDN_CONV - 1), wid), F32)
            first = jnp.concatenate([pad, cb_ref[0]], axis=0)
        else:
            first = jnp.zeros((SUBLANE, wid), F32)
        return jnp.where(ci > 0, prev, first)

    def conv_silu(ref, cb_ref, cw_ref, ci):
        if padded:
            cur = jnp.concatenate([ref[0], jnp.zeros((c - t, wid), F32)], axis=0)
        else:
            cur = ref[0, pl.ds(pl.multiple_of(ci * c, c), c), :]
        w = jnp.concatenate([halo_rows(ref, cb_ref, ci), cur], axis=0)
        out = cur * cw_ref[GDN_CONV - 1:GDN_CONV, :]
        for i in range(1, GDN_CONV):
            out = out + w[SUBLANE - i:SUBLANE - i + c] * cw_ref[GDN_CONV - 1 - i:GDN_CONV - i, :]
        return _silu(out)

    def body(ci, carry):
        cb = (cbq_ref, cbk_ref, cbv_ref) if has_state else (None, None, None)
        qa = conv_silu(q_ref, cb[0], cwq_ref, ci)
        ka = conv_silu(k_ref, cb[1], cwk_ref, ci)
        va = conv_silu(v_ref, cb[2], cwv_ref, ci)
        if padded:
            ba = jnp.concatenate([ba_ref[0, 0], jnp.zeros((c - t, LANE), F32)], axis=0)
            gate = jnp.concatenate([gate_ref[0], jnp.zeros((c - t, wid), F32)], axis=0)
            live = row_id < t
        else:
            ba = ba_ref[0, 0, pl.ds(pl.multiple_of(ci * c, c), c), :]
            gate = gate_ref[0, pl.ds(pl.multiple_of(ci * c, c), c), :]
        beta_all = jax.nn.sigmoid(ba)
        z = ba + dt_ref[0]
        softplus = jnp.maximum(z, 0.0) + jnp.log1p(jnp.exp(-jnp.abs(z)))
        g_all = -jnp.exp(alog_ref[0]) * softplus
        if padded:
            beta_all = jnp.where(live, beta_all, 0.0)
            g_all = jnp.where(live, g_all, 0.0)
        gcum_all = _cumsum_rows(g_all)
        gcum_t = gcum_all.T

        cols = [slice(i * DH, (i + 1) * DH) for i in heads]
        q = [qa[:, sl] for sl in cols]
        k = [ka[:, sl] for sl in cols]
        v = [va[:, sl] for sl in cols]
        q = [x * lax.rsqrt(jnp.sum(x * x, axis=-1, keepdims=True) + EPS) * (DH ** -0.5) for x in q]
        k = [x * lax.rsqrt(jnp.sum(x * x, axis=-1, keepdims=True) + EPS) for x in k]
        if padded:
            q = [jnp.where(live, x, 0.0) for x in q]
            k = [jnp.where(live, x, 0.0) for x in k]
            v = [jnp.where(live, x, 0.0) for x in v]
        beta = [beta_all[:, i:i + 1] for i in heads]
        gcum = [gcum_all[:, hg + i:hg + i + 1] for i in heads]
        g_last = [gcum_all[c - 1:c, hg + i:hg + i + 1] for i in heads]
        decay = [jnp.where(tril, jnp.exp(jnp.where(tril, gcum[i] - gcum_t[hg + i:hg + i + 1, :], 0.0)), 0.0)
                 for i in heads]
        e_cum = [jnp.exp(g) for g in gcum]
        kb = [k[i] * beta[i] for i in heads]
        k16 = [x.astype(BF16) for x in k]
        kk = [_dot_nt(kb[i].astype(BF16), k16[i]) for i in heads]
        qk = [_dot_nt(q[i].astype(BF16), k16[i]) for i in heads]
        m_mat = [jnp.where(strict, kk[i] * decay[i], 0.0) for i in heads]
        a_qk = [(qk[i] * decay[i]).astype(BF16) for i in heads]
        rhs = [jnp.concatenate([v[i] * beta[i], kb[i] * e_cum[i]], axis=1) for i in heads]
        sol = _unit_lower_solve(m_mat, rhs, eye, sub_diag)
        state = [s_ref[0, i] for i in heads]
        s16 = [x.astype(BF16) for x in state]
        w_s = [_dot(sol[i][:, DH:].astype(BF16), s16[i]) for i in heads]
        q_s = [_dot((q[i] * e_cum[i]).astype(BF16), s16[i]) for i in heads]
        v_new = [sol[i][:, :DH] - w_s[i] for i in heads]
        vn16 = [x.astype(BF16) for x in v_new]
        o = [q_s[i] + _dot(a_qk[i], vn16[i]) for i in heads]
        k_dec_t = [(k[i] * jnp.exp(g_last[i] - gcum[i])).T.astype(BF16) for i in heads]
        upd = [_dot(k_dec_t[i], vn16[i]) for i in heads]
        for i in heads:
            s_ref[0, i] = state[i] * jnp.exp(g_last[i]) + upd[i]
        y = jnp.concatenate([_rms_rows(o[i]) * gain_ref[...] * _silu(gate[:, cols[i]]) for i in heads],
                            axis=1)
        if padded:
            y_ref[0] = y[:t].astype(y_ref.dtype)
        else:
            y_ref[0, pl.ds(pl.multiple_of(ci * c, c), c), :] = y.astype(y_ref.dtype)
        return carry

    lax.fori_loop(0, n_chunks, body, 0)


def gdn_group_tail(ba, batch, seq):
    hg = GDN_HEADS_PER_STEP
    groups = H_C // hg
    br = ba[:, :H_C].reshape(batch, seq, groups, hg)
    ar = ba[:, H_C:2 * H_C].reshape(batch, seq, groups, hg)
    cat = jnp.concatenate([br, ar], axis=-1).transpose(0, 2, 1, 3)
    return jnp.pad(cat, ((0, 0), (0, 0), (0, 0), (0, LANE - 2 * hg)))


def gdn_group_vec(v):
    hg = GDN_HEADS_PER_STEP
    return jnp.pad(v.reshape(H_C // hg, 1, hg), ((0, 0), (0, 0), (hg, LANE - 2 * hg)))


def gdn_mixer(proj3, bag, conv_w, alog_vec, dt_vec, out_gain, conv_buf, state0, out_dtype):
    b, t, _ = proj3.shape
    hg = GDN_HEADS_PER_STEP
    wid = hg * DH
    has_state = state0 is not None
    col = lambda off: (lambda i, j: (i, 0, off // wid + j))
    wcol = lambda off: (lambda i, j: (0, off // wid + j))
    in_specs = [pl.BlockSpec((1, t, wid), col(OFF_C)),
                pl.BlockSpec((1, t, wid), col(OFF_C + D_C)),
                pl.BlockSpec((1, t, wid), col(OFF_C + 2 * D_C)),
                pl.BlockSpec((1, t, wid), col(OFF_G)),
                pl.BlockSpec((1, 1, t, LANE), lambda i, j: (i, j, 0, 0)),
                pl.BlockSpec((GDN_CONV, wid), wcol(0)),
                pl.BlockSpec((GDN_CONV, wid), wcol(D_C)),
                pl.BlockSpec((GDN_CONV, wid), wcol(2 * D_C)),
                pl.BlockSpec((1, 1, LANE), lambda i, j: (j, 0, 0)),
                pl.BlockSpec((1, 1, LANE), lambda i, j: (j, 0, 0)),
                pl.BlockSpec((1, DH), lambda i, j: (0, 0))]
    args = [proj3, proj3, proj3, proj3, bag, conv_w, conv_w, conv_w, alog_vec, dt_vec, out_gain]
    if has_state:
        in_specs += [pl.BlockSpec((1, GDN_CONV - 1, wid), col(0)),
                     pl.BlockSpec((1, GDN_CONV - 1, wid), col(D_C)),
                     pl.BlockSpec((1, GDN_CONV - 1, wid), col(2 * D_C)),
                     pl.BlockSpec((1, hg, DH, DH), lambda i, j: (i, j, 0, 0))]
        args += [conv_buf, conv_buf, conv_buf, state0]
    vmem = 2 * (4 * t * wid * 4 + t * LANE * 4 + t * wid * 4 + 2 * hg * DH * DH * 4) + (4 << 20)
    return pl.pallas_call(
        functools.partial(_gdn_kernel, t=t, has_state=has_state),
        grid=(b, H_C // hg),
        in_specs=in_specs,
        out_specs=[pl.BlockSpec((1, t, wid), lambda i, j: (i, 0, j)),
                   pl.BlockSpec((1, hg, DH, DH), lambda i, j: (i, j, 0, 0))],
        out_shape=[jax.ShapeDtypeStruct((b, t, D_C), out_dtype),
                   jax.ShapeDtypeStruct((b, H_C, DH, DH), F32)],
        compiler_params=_cparams(("parallel", "parallel"), vmem),
        name="gdn_mixer",
    )(*args)


def _proj_out_kernel(ya_ref, yb_ref, yc_ref, wa_ref, wb_ref, wc_ref, x_ref, o_ref):
    acc = _dot(ya_ref[...], wa_ref[...])
    acc = acc + _dot(yb_ref[...], wb_ref[...])
    acc = acc + _dot(yc_ref[...], wc_ref[...])
    o_ref[...] = x_ref[...] + acc


def proj_out(ya, yb, yc, wa, wb, wc, x):
    m, n = x.shape
    tm = min(m, 1024)
    tn = 512
    kk = D_A + D_B + D_C
    vmem = 2 * (tm * kk * 2 + kk * tn * 2 + 2 * tm * tn * 4)
    row = lambda i, j: (i, 0)
    colw = lambda i, j: (0, j)
    return pl.pallas_call(
        _proj_out_kernel,
        grid=(m // tm, n // tn),
        in_specs=[pl.BlockSpec((tm, D_A), row), pl.BlockSpec((tm, D_B), row),
                  pl.BlockSpec((tm, D_C), row),
                  pl.BlockSpec((D_A, tn), colw), pl.BlockSpec((D_B, tn), colw),
                  pl.BlockSpec((D_C, tn), colw),
                  pl.BlockSpec((tm, tn), lambda i, j: (i, j))],
        out_specs=pl.BlockSpec((tm, tn), lambda i, j: (i, j)),
        out_shape=jax.ShapeDtypeStruct((m, n), F32),
        compiler_params=_cparams(("parallel", "parallel"), vmem),
        name="proj_out",
    )(ya, yb, yc, wa, wb, wc, x)


FFN_TN = 256


def _ffn_conv(cur, ext_ref, cw_ref, b_ref, rows):
    out = cur * cw_ref[FFN_CONV - 1:FFN_CONV, :] + b_ref[...]
    for i in range(1, FFN_CONV):
        out = out + ext_ref[SUBLANE - i:SUBLANE - i + rows, :] * cw_ref[FFN_CONV - 1 - i:FFN_CONV - i, :]
    return out


def _shift_rows(u, halo, i):
    n, w = u.shape
    rot = pltpu.roll(u.reshape(n // SUBLANE, SUBLANE, w), i, axis=1)
    above = jnp.concatenate([pltpu.roll(halo, i, axis=0)[None], rot[:-1]], axis=0)
    sub = lax.broadcasted_iota(jnp.int32, (1, SUBLANE, 1), 1)
    return jnp.where(sub < i, above, rot).reshape(n, w)


def _ffn_conv_rows(u, halo, cw_ref, b_ref):
    out = u * cw_ref[FFN_CONV - 1:FFN_CONV, :] + b_ref[...]
    for i in range(1, FFN_CONV):
        out = out + _shift_rows(u, halo, i) * cw_ref[FFN_CONV - 1 - i:FFN_CONV - i, :]
    return out


FFN_ROW_SPLIT = 2


def _ffn_up_prompt_kernel(x_ref, wg_ref, wv_ref, cwg_ref, cwv_ref, bg_ref, bv_ref,
                          act_ref, tg_ref, tv_ref, halo_ref, *, tiles_per_seq):
    tm = x_ref.shape[0]
    rows = tm // FFN_ROW_SPLIT
    mi = pl.program_id(0)
    ni = pl.program_id(1)

    @pl.when(mi % tiles_per_seq == 0)
    def _():
        halo_ref[ni] = jnp.zeros(halo_ref.shape[1:], F32)

    halo_g = halo_ref[ni, 0]
    halo_v = halo_ref[ni, 1]
    ug = []
    uv = []
    for s in range(FFN_ROW_SPLIT):
        x = x_ref[s * rows:(s + 1) * rows, :]
        ug.append(_dot(x, wg_ref[...]))
        uv.append(_dot(x, wv_ref[...]))
    for s in range(FFN_ROW_SPLIT):
        gt = _ffn_conv_rows(ug[s], halo_g, cwg_ref, bg_ref)
        vl = _ffn_conv_rows(uv[s], halo_v, cwv_ref, bv_ref)
        act_ref[s * rows:(s + 1) * rows, :] = (_silu(gt) * vl).astype(act_ref.dtype)
        halo_g = ug[s][rows - SUBLANE:]
        halo_v = uv[s][rows - SUBLANE:]
    halo_ref[ni, 0] = halo_g
    halo_ref[ni, 1] = halo_v
    tg_ref[0] = halo_g
    tv_ref[0] = halo_v


def ffn_up_prompt(xn, w_up, conv_w, conv_b, seq):
    m, k = xn.shape
    d_ff = w_up.shape[1] // 2
    tn = FFN_TN
    nt = d_ff // tn
    tm = min(seq, 1024)
    mt = m // tm
    lo = lambda i, j: (0, j)
    hi = lambda i, j: (0, nt + j)
    vmem = (2 * (tm * k * 2 + 2 * k * tn * 2 + tm * tn * 2) + nt * 2 * SUBLANE * tn * 4
            + 8 * tm * tn * 4)
    return pl.pallas_call(
        functools.partial(_ffn_up_prompt_kernel, tiles_per_seq=seq // tm),
        grid=(mt, nt),
        in_specs=[pl.BlockSpec((tm, k), lambda i, j: (i, 0)),
                  pl.BlockSpec((k, tn), lo), pl.BlockSpec((k, tn), hi),
                  pl.BlockSpec((FFN_CONV, tn), lo), pl.BlockSpec((FFN_CONV, tn), hi),
                  pl.BlockSpec((1, tn), lo), pl.BlockSpec((1, tn), hi)],
        out_specs=[pl.BlockSpec((tm, tn), lambda i, j: (i, j)),
                   pl.BlockSpec((1, SUBLANE, tn), lambda i, j: (i, 0, j)),
                   pl.BlockSpec((1, SUBLANE, tn), lambda i, j: (i, 0, j))],
        out_shape=[jax.ShapeDtypeStruct((m, d_ff), BF16),
                   jax.ShapeDtypeStruct((mt, SUBLANE, d_ff), F32),
                   jax.ShapeDtypeStruct((mt, SUBLANE, d_ff), F32)],
        scratch_shapes=[pltpu.VMEM((nt, 2, SUBLANE, tn), F32)],
        compiler_params=_cparams(("arbitrary", "arbitrary"), vmem),
        name="ffn_up_prompt",
    )(xn, w_up, w_up, conv_w, conv_w, conv_b, conv_b)


def _ffn_up_sample_kernel(x_ref, wg_ref, wv_ref, cwg_ref, cwv_ref, bg_ref, bv_ref, sg_ref, sv_ref,
                          act_ref, ug_ref, uv_ref, ext_ref, *, t):
    nb = x_ref.shape[0] // t
    assert t == SUBLANE
    x = x_ref[...]
    ug = _dot(x, wg_ref[...])
    uv = _dot(x, wv_ref[...])
    ug_ref[...] = ug
    uv_ref[...] = uv

    def conv(u, st_ref, cw_ref, b_ref):
        outs = []
        for bi in range(nb):
            cur = u[bi * t:(bi + 1) * t]
            ext_ref[SUBLANE - (FFN_CONV - 1):SUBLANE, :] = st_ref[bi]
            ext_ref[SUBLANE:, :] = cur
            outs.append(_ffn_conv(cur, ext_ref, cw_ref, b_ref, t))
        return jnp.concatenate(outs, axis=0)

    gt = conv(ug, sg_ref, cwg_ref, bg_ref)
    vl = conv(uv, sv_ref, cwv_ref, bv_ref)
    act_ref[...] = (_silu(gt) * vl).astype(act_ref.dtype)


def ffn_up_sample(xn, w_up, conv_w, conv_b, conv_state, t):
    m, k = xn.shape
    d_ff = w_up.shape[1] // 2
    tn = FFN_TN
    nt = d_ff // tn
    nb = m // t
    lo = lambda j: (0, j)
    hi = lambda j: (0, nt + j)
    vmem = 2 * (m * k * 2 + 2 * k * tn * 2 + 4 * m * tn * 4) + (2 << 20)
    return pl.pallas_call(
        functools.partial(_ffn_up_sample_kernel, t=t),
        grid=(nt,),
        in_specs=[pl.BlockSpec((m, k), lambda j: (0, 0)),
                  pl.BlockSpec((k, tn), lo), pl.BlockSpec((k, tn), hi),
                  pl.BlockSpec((FFN_CONV, tn), lo), pl.BlockSpec((FFN_CONV, tn), hi),
                  pl.BlockSpec((1, tn), lo), pl.BlockSpec((1, tn), hi),
                  pl.BlockSpec((nb, FFN_CONV - 1, tn), lambda j: (0, 0, j)),
                  pl.BlockSpec((nb, FFN_CONV - 1, tn), lambda j: (0, 0, nt + j))],
        out_specs=[pl.BlockSpec((m, tn), lo), pl.BlockSpec((m, tn), lo), pl.BlockSpec((m, tn), lo)],
        out_shape=[jax.ShapeDtypeStruct((m, d_ff), BF16),
                   jax.ShapeDtypeStruct((m, d_ff), F32),
                   jax.ShapeDtypeStruct((m, d_ff), F32)],
        scratch_shapes=[pltpu.VMEM((SUBLANE + t, tn), F32)],
        compiler_params=_cparams(("parallel",), vmem),
        name="ffn_up_sample",
    )(xn, w_up, w_up, conv_w, conv_w, conv_b, conv_b, conv_state, conv_state)


def _ffn_down_kernel(a_ref, w_ref, h_ref, o_ref):
    o_ref[...] = h_ref[...] + _dot(a_ref[...], w_ref[...])


def ffn_down_proj(act, w_down, h):
    m, k = act.shape
    n = w_down.shape[1]
    tm = min(m, 512)
    tn = 512
    vmem = 2 * (tm * k * 2 + k * tn * 2 + 2 * tm * tn * 4)
    return pl.pallas_call(
        _ffn_down_kernel,
        grid=(m // tm, n // tn),
        in_specs=[pl.BlockSpec((tm, k), lambda i, j: (i, 0)),
                  pl.BlockSpec((k, tn), lambda i, j: (0, j)),
                  pl.BlockSpec((tm, tn), lambda i, j: (i, j))],
        out_specs=pl.BlockSpec((tm, tn), lambda i, j: (i, j)),
        out_shape=jax.ShapeDtypeStruct((m, n), F32),
        compiler_params=_cparams(("parallel", "parallel"), vmem),
        name="ffn_down",
    )(act, w_down, h)


def _t5_bucket(dist):
    max_exact = REL_BUCKETS // 2
    d = jnp.maximum(dist, 1).astype(F32)
    large = max_exact + (jnp.log(d / max_exact) / math.log(REL_MAX_DIST / max_exact)
                         * (REL_BUCKETS - max_exact)).astype(jnp.int32)
    large = jnp.minimum(large, REL_BUCKETS - 1)
    return jnp.where(dist < max_exact, dist, large)


def _bias_lookup(rel_bias, buckets):
    onehot = jax.nn.one_hot(buckets, REL_BUCKETS, dtype=F32)
    return jnp.einsum('...k,kh->...h', onehot, rel_bias.astype(F32), precision=HIGHEST)


def _prompt_bias_blocks(rel_bias):
    qi = np.arange(BLK)[:, None]
    ki = np.arange(2 * BLK)[None, :]
    rel = qi + BLK - ki
    out = []
    for window, dil in A_BRANCHES:
        nj = window // dil + 1
        valid = (rel >= 0) & (rel < nj)
        buckets = _t5_bucket(jnp.asarray(np.clip(rel, 0, nj - 1) * dil, jnp.int32))
        bias = _bias_lookup(rel_bias, buckets)
        out.append(jnp.where(jnp.asarray(valid)[..., None], bias, NEG_INF).transpose(2, 0, 1))
    return jnp.stack(out)


def _branch_count(dist):
    cnt = np.zeros(dist.shape, np.float32)
    for window, dil in A_BRANCHES:
        cnt += ((dist >= 0) & (dist % dil == 0) & (dist // dil <= window // dil)).astype(np.float32)
    return cnt


def _sample_bias_tables(rel_bias, t, n_past):
    tq = np.arange(t)[:, None]
    d_cache = n_past + tq - np.arange(n_past)[None, :]
    d_new = tq - np.arange(BLK)[None, :]
    d_new = np.where(np.arange(BLK)[None, :] < t, d_new, -1)
    tables = []
    for dist in (d_cache, d_new):
        cnt = _branch_count(dist)
        bias = _bias_lookup(rel_bias, _t5_bucket(jnp.asarray(np.maximum(dist, 0), jnp.int32)))
        bias = jnp.where(jnp.asarray(cnt > 0)[..., None], bias, NEG_INF).transpose(2, 0, 1)
        tables += [bias, jnp.asarray(cnt)]
    return tables


def _layer(x, lw, *, batch, seq, layer, cache_hm=None, pool_buf=None, gconv_buf=None, gstate=None,
           fconv_buf=None, bias_tabs=None):
    sample = cache_hm is not None
    mix_dtype = F32 if sample else BF16
    xn = rmsnorm_cast(x, lw['norm_mix'])
    proj, ba = proj_in(xn, lw['w_in_main'], lw['w_in_tail'])
    proj3 = proj.reshape(batch, seq, N_MAIN)
    bag = gdn_group_tail(ba, batch, seq)
    qg = lw['a_q_norm'].reshape(1, DH)
    kg = lw['a_k_norm'].reshape(1, DH)
    og = lw['a_out_norm'].reshape(H_A, DH)
    if sample:
        ya, kv = attn_sample(proj3, cache_hm, layer, qg, kg, og, *bias_tabs)
    else:
        ya, kv = attn_prompt(proj3, qg, kg, og, bias_tabs)
    kv_rows = kv.reshape(batch, seq, H_A, 2, DH).transpose(0, 1, 3, 2, 4)
    yb = pool_mixer(proj3, pool_buf, lw['pool_w'], lw['pool_scale'].reshape(1, D_B), mix_dtype)
    pu = proj3[:, :, OFF_P:OFF_P + D_B]
    if sample:
        pool_new = jnp.concatenate([pool_buf, pu], axis=1)[:, -POOL_BUF:]
    else:
        pool_new = pu[:, -POOL_BUF:]
    yc, gstate_new = gdn_mixer(proj3, bag, lw['gdn_conv_w'], lw['alog_vec'], lw['dt_vec'],
                               lw['gdn_out_norm'].reshape(1, DH), gconv_buf, gstate, mix_dtype)
    gconv_new = proj3[:, -(GDN_CONV - 1):, OFF_C:OFF_C + 3 * D_C]
    m = batch * seq
    h = proj_out(ya.reshape(m, D_A).astype(BF16), yb.reshape(m, D_B).astype(BF16),
                 yc.reshape(m, D_C).astype(BF16), lw['w_out_a'], lw['w_out_b'], lw['w_out_c'], x)
    hn = rmsnorm_cast(h, lw['norm_ffn'])
    if sample:
        act, ug, uv = ffn_up_sample(hn, lw['ffn_up'], lw['ffn_conv_w'], lw['ffn_conv_b'], fconv_buf, seq)
        up = jnp.concatenate([ug, uv], axis=-1).reshape(batch, seq, -1)
        fconv_new = up[:, -(FFN_CONV - 1):]
    else:
        act, tg, tv = ffn_up_prompt(hn, lw['ffn_up'], lw['ffn_conv_w'], lw['ffn_conv_b'], seq)
        tails = jnp.concatenate([tg, tv], axis=-1)
        tiles_per_seq = tails.shape[0] // batch
        fconv_new = tails[tiles_per_seq - 1::tiles_per_seq, -(FFN_CONV - 1):]
    y = ffn_down_proj(act, lw['ffn_down'], h)
    return y, (kv_rows, pool_new, gconv_new, gstate_new, fconv_new)


def kernel(x_prompt, x_sample, cache_attn_kv, state_pool, state_gdn_conv, state_gdn, state_ffn_conv,
           rel_bias, norm_mix, w_in, a_q_norm, a_k_norm, a_out_norm, pool_w, pool_scale,
           gdn_conv_w, gdn_a_log, gdn_dt_bias, gdn_out_norm, w_out, norm_ffn,
           ffn_up, ffn_conv_w, ffn_conv_b, ffn_down):
    depth = w_in.shape[0]
    bp, sp, d_model = x_prompt.shape
    bs, ss, _ = x_sample.shape
    n_past = cache_attn_kv.shape[2]
    assert w_in.shape[2] == N_MAIN + N_TAIL and w_out.shape[1] == D_A + D_B + D_C

    prompt_bias = _prompt_bias_blocks(rel_bias)
    sample_tabs = _sample_bias_tables(rel_bias, ss, n_past)
    cache_hm = jnp.transpose(cache_attn_kv, (0, 1, 3, 4, 2, 5))

    xp = x_prompt.reshape(bp * sp, d_model)
    xs = x_sample.reshape(bs * ss, d_model)
    p_out = [[] for _ in range(5)]
    s_out = [[] for _ in range(5)]
    for l in range(depth):
        tail = jnp.pad(w_in[l, :, N_MAIN:], ((0, 0), (0, LANE - N_TAIL)))
        lw = {
            'norm_mix': norm_mix[l], 'norm_ffn': norm_ffn[l],
            'w_in_main': w_in[l, :, :N_MAIN].astype(BF16), 'w_in_tail': tail.astype(BF16),
            'a_q_norm': a_q_norm[l], 'a_k_norm': a_k_norm[l], 'a_out_norm': a_out_norm[l],
            'pool_w': pool_w[l].astype(BF16), 'pool_scale': pool_scale[l],
            'gdn_conv_w': gdn_conv_w[l],
            'alog_vec': gdn_group_vec(gdn_a_log[l]), 'dt_vec': gdn_group_vec(gdn_dt_bias[l]),
            'gdn_out_norm': gdn_out_norm[l],
            'w_out_a': w_out[l, :D_A].astype(BF16),
            'w_out_b': w_out[l, D_A:D_A + D_B].astype(BF16),
            'w_out_c': w_out[l, D_A + D_B:].astype(BF16),
            'ffn_up': ffn_up[l].astype(BF16), 'ffn_conv_w': ffn_conv_w[l],
            'ffn_conv_b': ffn_conv_b[l].reshape(1, -1), 'ffn_down': ffn_down[l].astype(BF16),
        }
        xp, outs = _layer(xp, lw, batch=bp, seq=sp, layer=l, bias_tabs=prompt_bias)
        for acc, o in zip(p_out, outs):
            acc.append(o)
        xs, outs = _layer(xs, lw, batch=bs, seq=ss, layer=l, cache_hm=cache_hm, pool_buf=state_pool[l],
                          gconv_buf=state_gdn_conv[l], gstate=state_gdn[l],
                          fconv_buf=state_ffn_conv[l], bias_tabs=sample_tabs)
        for acc, o in zip(s_out, outs):
            acc.append(o)
    res = [xp.reshape(bp, sp, d_model), xs.reshape(bs, ss, d_model)]
    for po, so in zip(p_out, s_out):
        res += [jnp.stack(po), jnp.stack(so)]
    return tuple(res)
```

```python
import functools
import math

import numpy as np
import jax
import jax.numpy as jnp
from jax import lax
from jax.experimental import pallas as pl
from jax.experimental.pallas import tpu as pltpu

F32 = jnp.float32
BF16 = jnp.bfloat16
HIGHEST = lax.Precision.HIGHEST

DH = 128
H_A = 12
H_C = 12
A_BRANCHES = ((128, 1), (512, 4), (2048, 16))
BLK = 128
REL_BUCKETS = 32
REL_MAX_DIST = 2048
POOL_WINDOWS = (2, 4, 8, 16)
CG = 256
POOL_BUF = 15
GDN_CONV = 4
GDN_CHUNK = 64
GDN_SUB = 16
FFN_CONV = 3
EPS = 1e-6
NEG_INF = -1e30

D_A = H_A * DH
D_B = len(POOL_WINDOWS) * CG
D_C = H_C * DH
OFF_Q, OFF_K, OFF_V = 0, D_A, 2 * D_A
OFF_P = 3 * D_A
OFF_C = OFF_P + D_B
OFF_G = OFF_C + 3 * D_C
N_MAIN = OFF_G + D_C
N_TAIL = 2 * H_C

LANE = 128
SUBLANE = 8
VMEM_CAP = 56 * 1024 * 1024


def _cparams(sem, vmem_bytes):
    limit = int(min(max(vmem_bytes * 5 // 4 + (2 << 20), 16 << 20), VMEM_CAP))
    return pltpu.CompilerParams(dimension_semantics=sem, vmem_limit_bytes=limit)


def _rms_rows(x):
    return x * lax.rsqrt(jnp.mean(x * x, axis=-1, keepdims=True) + EPS)


def _silu(x):
    return x * jax.nn.sigmoid(x)


def _dot(a, b):
    return jnp.dot(a, b, preferred_element_type=F32)


def _dot_nt(a, b):
    return lax.dot_general(a, b, (((1,), (1,)), ((), ())), preferred_element_type=F32)


def _hdot(a, b):
    return jnp.dot(a, b, preferred_element_type=F32, precision=HIGHEST)


def _hdot_nt(a, b):
    return lax.dot_general(a, b, (((1,), (1,)), ((), ())), preferred_element_type=F32,
                           precision=HIGHEST)


def _hdot_tn(a, b):
    return lax.dot_general(a, b, (((0,), (0,)), ((), ())), preferred_element_type=F32,
                           precision=HIGHEST)


def _rmsnorm_kernel(x_ref, g_ref, o_ref):
    o_ref[...] = (_rms_rows(x_ref[...]) * g_ref[...]).astype(o_ref.dtype)


def rmsnorm_cast(x, gain):
    m, d = x.shape
    tm = min(m, 256)
    return pl.pallas_call(
        _rmsnorm_kernel,
        grid=(m // tm,),
        in_specs=[pl.BlockSpec((tm, d), lambda i: (i, 0)),
                  pl.BlockSpec((1, d), lambda i: (0, 0))],
        out_specs=pl.BlockSpec((tm, d), lambda i: (i, 0)),
        out_shape=jax.ShapeDtypeStruct((m, d), BF16),
        compiler_params=_cparams(("parallel",), 2 * tm * d * 6),
        name="rmsnorm_cast",
    )(x, gain.reshape(1, d))


def _proj_in_kernel(x_ref, w_ref, wt_ref, o_ref, ot_ref):
    o_ref[...] = _dot(x_ref[...], w_ref[...])

    @pl.when(pl.program_id(1) == 0)
    def _():
        ot_ref[...] = _dot(x_ref[...], wt_ref[...])


def proj_in(xn, w_all, w_tail, layer):
    m, k = xn.shape
    n = N_MAIN
    tm = min(m, 1024)
    tn = 512
    vmem = 2 * (tm * k * 2 + k * tn * 2 + tm * tn * 4 + k * LANE * 2 + tm * LANE * 4)
    return pl.pallas_call(
        _proj_in_kernel,
        grid=(m // tm, n // tn),
        in_specs=[pl.BlockSpec((tm, k), lambda i, j: (i, 0)),
                  pl.BlockSpec((None, k, tn), lambda i, j: (layer, 0, j)),
                  pl.BlockSpec((None, k, LANE), lambda i, j: (layer, 0, 0))],
        out_specs=[pl.BlockSpec((tm, tn), lambda i, j: (i, j)),
                   pl.BlockSpec((tm, LANE), lambda i, j: (i, 0))],
        out_shape=[jax.ShapeDtypeStruct((m, n), F32),
                   jax.ShapeDtypeStruct((m, LANE), F32)],
        compiler_params=_cparams(("parallel", "arbitrary"), vmem),
        name="proj_in",
    )(xn, w_all, w_tail)


ATTN_BLOCKS_PER_TRIP = 4


def _strided_rows(start, dil):
    return pl.ds(start, BLK) if dil == 1 else pl.ds(start, BLK, stride=dil)


def _attn_prompt_kernel(q_ref, k_ref, v_ref, qg_ref, kg_ref, og_ref, bias_ref,
                        y_ref, kv_ref, qs_ref, ks_ref, acc_ref, m_ref, l_ref):
    seq = q_ref.shape[1]
    qs_ref[...] = _rms_rows(q_ref[0]) * qg_ref[...] * (DH ** -0.5)
    ks_ref[...] = _rms_rows(k_ref[0]) * kg_ref[...]
    kv_ref[0, :, 0:DH] = ks_ref[...]
    kv_ref[0, :, DH:] = v_ref[0]
    acc_ref[...] = jnp.zeros(acc_ref.shape, F32)
    l_ref[...] = jnp.zeros(l_ref.shape, F32)
    m_ref[...] = jnp.full(m_ref.shape, NEG_INF, F32)

    for bi, (window, dil) in enumerate(A_BRANCHES):
        assert window // dil == BLK and seq % (dil * BLK) == 0
        nb = seq // (dil * BLK)

        assert (dil * nb) % ATTN_BLOCKS_PER_TRIP == 0

        def body(it, carry, bi=bi, dil=dil, nb=nb):
            us = range(ATTN_BLOCKS_PER_TRIP)
            item = [it * ATTN_BLOCKS_PER_TRIP + u for u in us]
            r = [x // nb for x in item]
            n = [item[u] - r[u] * nb for u in us]
            rows = [_strided_rows(n[u] * (BLK * dil) + r[u], dil) for u in us]
            qb = [qs_ref[rw, :].astype(BF16) for rw in rows]
            kc = [ks_ref[rw, :] for rw in rows]
            vc = [v_ref[0, rw, :] for rw in rows]
            bias_cur = bias_ref[bi, 0, :, BLK:]
            if nb > 1:
                prows = [_strided_rows(jnp.maximum(n[u] - 1, 0) * (BLK * dil) + r[u], dil) for u in us]
                kk = [jnp.concatenate([ks_ref[prows[u], :], kc[u]], axis=0).astype(BF16) for u in us]
                vv = [jnp.concatenate([v_ref[0, prows[u], :], vc[u]], axis=0).astype(BF16) for u in us]
                bias_prev = bias_ref[bi, 0, :, :BLK]
                bias = [jnp.concatenate([jnp.where(n[u] > 0, bias_prev, NEG_INF), bias_cur], axis=1)
                        for u in us]
            else:
                kk = [x.astype(BF16) for x in kc]
                vv = [x.astype(BF16) for x in vc]
                bias = [bias_cur for _ in us]
            s = [_dot_nt(qb[u], kk[u]) + bias[u] for u in us]
            m_old = [m_ref[rw, :] for rw in rows]
            m_new = [jnp.maximum(m_old[u], jnp.max(s[u], axis=1, keepdims=True)) for u in us]
            p = [jnp.exp(s[u] - m_new[u][:, :1]) for u in us]
            alpha = [jnp.exp(m_old[u] - m_new[u]) for u in us]
            pv = [_dot(p[u].astype(BF16), vv[u]) for u in us]
            for u in us:
                l_ref[rows[u], :] = alpha[u] * l_ref[rows[u], :] + jnp.sum(p[u], axis=1, keepdims=True)
                acc_ref[rows[u], :] = alpha[u] * acc_ref[rows[u], :] + pv[u]
                m_ref[rows[u], :] = m_new[u]
            return carry

        lax.fori_loop(0, dil * nb // ATTN_BLOCKS_PER_TRIP, body, 0)

    o = acc_ref[...] / l_ref[...]
    y_ref[0] = (_rms_rows(o) * og_ref[0]).astype(y_ref.dtype)


def attn_prompt(proj3, q_gain, k_gain, o_gain, bias_blocks):
    b, s, _ = proj3.shape
    col = lambda off: (lambda i, h: (i, 0, off // DH + h))
    vmem = 2 * (3 * s * DH * 4 + s * DH * 2 + s * DH * 4 + 3 * BLK * 2 * BLK * 4) + 5 * s * DH * 4
    return pl.pallas_call(
        _attn_prompt_kernel,
        grid=(b, H_A),
        in_specs=[pl.BlockSpec((1, s, DH), col(OFF_Q)),
                  pl.BlockSpec((1, s, DH), col(OFF_K)),
                  pl.BlockSpec((1, s, DH), col(OFF_V)),
                  pl.BlockSpec((1, DH), lambda i, h: (0, 0)),
                  pl.BlockSpec((1, DH), lambda i, h: (0, 0)),
                  pl.BlockSpec((1, 1, DH), lambda i, h: (h, 0, 0)),
                  pl.BlockSpec((len(A_BRANCHES), 1, BLK, 2 * BLK), lambda i, h: (0, h, 0, 0))],
        out_specs=[pl.BlockSpec((1, s, DH), lambda i, h: (i, 0, h)),
                   pl.BlockSpec((1, s, 2 * DH), lambda i, h: (i, 0, h))],
        out_shape=[jax.ShapeDtypeStruct((b, s, D_A), BF16),
                   jax.ShapeDtypeStruct((b, s, 2 * D_A), F32)],
        scratch_shapes=[pltpu.VMEM((s, DH), F32)] * 5,
        compiler_params=_cparams(("parallel", "parallel"), vmem),
        name="attn_prompt",
    )(proj3, proj3, proj3, q_gain, k_gain, o_gain.reshape(H_A, 1, DH), bias_blocks)


def _attn_sample_kernel(q_ref, k_ref, v_ref, kc_ref, vc_ref, qg_ref, kg_ref, og_ref,
                        bc_ref, cc_ref, bn_ref, cn_ref, y_ref, kv_ref, kpad_ref, vpad_ref):
    t = q_ref.shape[1]
    qn = (_rms_rows(q_ref[0]) * qg_ref[...] * (DH ** -0.5)).astype(BF16)
    kn = _rms_rows(k_ref[0]) * kg_ref[...]
    kv_ref[0, :, 0:DH] = kn
    kv_ref[0, :, DH:] = v_ref[0]
    kpad_ref[...] = jnp.zeros(kpad_ref.shape, F32)
    vpad_ref[...] = jnp.zeros(vpad_ref.shape, F32)
    kpad_ref[0:t, :] = kn
    vpad_ref[0:t, :] = v_ref[0]
    s_c = _dot_nt(qn, kc_ref[0, 0, 0, 0].astype(BF16)) + bc_ref[0]
    s_n = _dot_nt(qn, kpad_ref[...].astype(BF16)) + bn_ref[0]
    m = jnp.maximum(jnp.max(s_c, axis=1, keepdims=True), jnp.max(s_n, axis=1, keepdims=True))
    p_c = jnp.exp(s_c - m) * cc_ref[...]
    p_n = jnp.exp(s_n - m) * cn_ref[...]
    den = jnp.sum(p_c, axis=1, keepdims=True) + jnp.sum(p_n, axis=1, keepdims=True)
    o = (_dot(p_c.astype(BF16), vc_ref[0, 0, 0, 0].astype(BF16))
         + _dot(p_n.astype(BF16), vpad_ref[...].astype(BF16))) / den
    y_ref[0] = _rms_rows(o) * og_ref[0]


def attn_sample(proj3, cache_hm, layer, q_gain, k_gain, o_gain, bias_c, cnt_c, bias_n, cnt_n):
    b, t, _ = proj3.shape
    n_past = cache_hm.shape[4]
    col = lambda off: (lambda i, h: (i, 0, off // DH + h))
    vmem = 2 * (2 * n_past * DH * 4 + 2 * t * n_past * 4) + 2 * BLK * DH * 4 + (1 << 20)
    return pl.pallas_call(
        _attn_sample_kernel,
        grid=(b, H_A),
        in_specs=[pl.BlockSpec((1, t, DH), col(OFF_Q)),
                  pl.BlockSpec((1, t, DH), col(OFF_K)),
                  pl.BlockSpec((1, t, DH), col(OFF_V)),
                  pl.BlockSpec((1, 1, 1, 1, n_past, DH), lambda i, h: (layer, i, 0, h, 0, 0)),
                  pl.BlockSpec((1, 1, 1, 1, n_past, DH), lambda i, h: (layer, i, 1, h, 0, 0)),
                  pl.BlockSpec((1, DH), lambda i, h: (0, 0)),
                  pl.BlockSpec((1, DH), lambda i, h: (0, 0)),
                  pl.BlockSpec((1, 1, DH), lambda i, h: (h, 0, 0)),
                  pl.BlockSpec((1, t, n_past), lambda i, h: (h, 0, 0)),
                  pl.BlockSpec((t, n_past), lambda i, h: (0, 0)),
                  pl.BlockSpec((1, t, BLK), lambda i, h: (h, 0, 0)),
                  pl.BlockSpec((t, BLK), lambda i, h: (0, 0))],
        out_specs=[pl.BlockSpec((1, t, DH), lambda i, h: (i, 0, h)),
                   pl.BlockSpec((1, t, 2 * DH), lambda i, h: (i, 0, h))],
        out_shape=[jax.ShapeDtypeStruct((b, t, D_A), F32),
                   jax.ShapeDtypeStruct((b, t, 2 * D_A), F32)],
        scratch_shapes=[pltpu.VMEM((BLK, DH), F32)] * 2,
        compiler_params=_cparams(("parallel", "parallel"), vmem),
        name="attn_sample",
    )(proj3, proj3, proj3, cache_hm, cache_hm, q_gain, k_gain, o_gain.reshape(H_A, 1, DH),
      bias_c, cnt_c, bias_n, cnt_n)


POOL_HALO = 16


def _pool_kernel(*refs, t, rows, n_valid, has_buf):
    ng = len(POOL_WINDOWS)
    u_refs = refs[:ng]
    refs = refs[ng:]
    if has_buf:
        buf_refs = refs[:ng]
        refs = refs[ng:]
    w_ref, scale_ref, y_ref, ext_ref = refs
    for g, win in enumerate(POOL_WINDOWS):
        ext_ref[0:POOL_HALO, :] = jnp.zeros((POOL_HALO, CG), F32)
        if has_buf:
            ext_ref[POOL_HALO - POOL_BUF:POOL_HALO, :] = buf_refs[g][0]
        ext_ref[POOL_HALO:, :] = u_refs[g][0]

        def chunk(ci, carry, g=g, win=win):
            base = pl.multiple_of(ci * rows, SUBLANE)
            w = ext_ref[pl.ds(base, rows + POOL_HALO), :]
            u = w[POOL_HALO:]
            tot = u
            for i in range(1, win):
                tot = tot + w[POOL_HALO - i:POOL_HALO - i + rows]
            pos = base + lax.broadcasted_iota(jnp.int32, (rows, 1), 0)
            cnt = jnp.minimum(win, n_valid + pos + 1).astype(F32)
            d = tot / cnt - u
            y = _rms_rows(_dot(d.astype(BF16), w_ref[g])) * scale_ref[:, g * CG:(g + 1) * CG]
            y_ref[0, pl.ds(base, rows), g * CG:(g + 1) * CG] = y.astype(y_ref.dtype)
            return carry

        lax.fori_loop(0, t // rows, chunk, 0)


def pool_mixer(proj3, bufs, w_pool, scale, out_dtype):
    b, t, _ = proj3.shape
    ng = len(POOL_WINDOWS)
    rows = min(t, 256)
    has_buf = bufs is not None
    in_specs = [pl.BlockSpec((1, t, CG), (lambda i, g=g: (i, 0, OFF_P // CG + g))) for g in range(ng)]
    args = [proj3] * ng
    if has_buf:
        in_specs += [pl.BlockSpec((1, POOL_BUF, CG), (lambda i, g=g: (i, 0, g))) for g in range(ng)]
        args += [bufs] * ng
    in_specs += [pl.BlockSpec((ng, CG, CG), lambda i: (0, 0, 0)),
                 pl.BlockSpec((1, D_B), lambda i: (0, 0))]
    args += [w_pool, scale]
    vmem = 2 * (ng * t * CG * 4 + t * D_B * 4 + ng * CG * CG * 2) + (t + POOL_HALO) * CG * 4
    return pl.pallas_call(
        functools.partial(_pool_kernel, t=t, rows=rows, n_valid=POOL_BUF if has_buf else 0,
                          has_buf=has_buf),
        grid=(b,),
        in_specs=in_specs,
        out_specs=pl.BlockSpec((1, t, D_B), lambda i: (i, 0, 0)),
        out_shape=jax.ShapeDtypeStruct((b, t, D_B), out_dtype),
        scratch_shapes=[pltpu.VMEM((t + POOL_HALO, CG), F32)],
        compiler_params=_cparams(("parallel",), vmem),
        name="pool_mixer",
    )(*args)


GDN_HEADS_PER_STEP = 4
GDN_CHUNKS_PER_TRIP = 4


def _split_bf16(a):
    hi = a.astype(BF16)
    return hi, (a - hi.astype(F32)).astype(BF16)


def _dot3(a, b):
    return _dot(a[0], b[0]) + (_dot(a[0], b[1]) + _dot(a[1], b[0]))


def _cumsum_rows(x):
    n = x.shape[0]
    row = lax.broadcasted_iota(jnp.int32, (n, 1), 0)
    k = 1
    while k < n:
        x = x + jnp.where(row >= k, pltpu.roll(x, k, axis=0), 0.0)
        k *= 2
    return x


def _unit_lower_solve(mats, rhss, eye, sub_diag):
    assert GDN_CHUNK // GDN_SUB == 4 and GDN_SUB == 16
    idx = range(len(mats))
    d = [jnp.where(sub_diag, m, 0.0) for m in mats]
    low = [_split_bf16(m - di) for m, di in zip(mats, d)]
    rs = [_split_bf16(r) for r in rhss]
    x = [eye - di for di in d]
    ps = [_split_bf16(di) for di in d]
    ps = [_split_bf16(_dot3(ps[i], ps[i])) for i in idx]
    for _ in range(2):
        xs = [_split_bf16(xi) for xi in x]
        x = [x[i] + _dot3(xs[i], ps[i]) for i in idx]
        ps = [_split_bf16(_dot3(ps[i], ps[i])) for i in idx]
    xs = [_split_bf16(xi) for xi in x]
    x = [x[i] + _dot3(xs[i], ps[i]) for i in idx]
    xs = [_split_bf16(xi) for xi in x]
    n = [_dot3(xs[i], low[i]) for i in idx]
    xr = [_dot3(xs[i], rs[i]) for i in idx]
    ns = [_split_bf16(ni) for ni in n]
    n2 = [_split_bf16(_dot3(ns[i], ns[i])) for i in idx]
    imn = [eye - ni for ni in n]
    y = [imn[i] + _dot3(_split_bf16(imn[i]), n2[i]) for i in idx]
    return [_dot3(_split_bf16(y[i]), _split_bf16(xr[i])) for i in idx]


def _gdn_kernel(*refs, t, has_state):
    c = GDN_CHUNK
    hg = GDN_HEADS_PER_STEP
    (q_ref, k_ref, v_ref, gate_ref, ba_ref, cwq_ref, cwk_ref, cwv_ref,
     alog_ref, dt_ref, gain_ref) = refs[:11]
    refs = refs[11:]
    if has_state:
        cbq_ref, cbk_ref, cbv_ref, s0_ref = refs[:4]
        refs = refs[4:]
    y_ref, s_ref = refs
    n_chunks = -(-t // c)
    padded = t % c != 0
    assert (not padded) or n_chunks == 1
    wid = hg * DH
    heads = range(hg)

    if has_state:
        s_ref[0] = s0_ref[0]
    else:
        s_ref[0] = jnp.zeros(s_ref.shape[1:], F32)

    ii = lax.broadcasted_iota(jnp.int32, (c, c), 0)
    jj = lax.broadcasted_iota(jnp.int32, (c, c), 1)
    tril = ii >= jj
    strict = ii > jj
    eye = (ii == jj).astype(F32)
    sub_diag = (ii // GDN_SUB) == (jj // GDN_SUB)
    row_id = lax.broadcasted_iota(jnp.int32, (c, 1), 0)

    def halo_rows(ref, cb_ref, ci):
        if n_chunks > 1:
            prev = ref[0, pl.ds(pl.multiple_of(jnp.maximum(ci * c - SUBLANE, 0), SUBLANE), SUBLANE), :]
        else:
            prev = jnp.zeros((SUBLANE, wid), F32)
        if has_state:
            pad = jnp.zeros((SUBLANE - (GDN_CONV - 1), wid), F32)
            first = jnp.concatenate([pad, cb_ref[0]], axis=0)
        else:
            first = jnp.zeros((SUBLANE, wid), F32)
        return jnp.where(ci > 0, prev, first)

    def conv_silu(ref, cb_ref, cw_ref, ci):
        if padded:
            cur = jnp.concatenate([ref[0], jnp.zeros((c - t, wid), F32)], axis=0)
        else:
            cur = ref[0, pl.ds(pl.multiple_of(ci * c, c), c), :]
        w = jnp.concatenate([halo_rows(ref, cb_ref, ci), cur], axis=0)
        out = cur * cw_ref[GDN_CONV - 1:GDN_CONV, :]
        for i in range(1, GDN_CONV):
            out = out + w[SUBLANE - i:SUBLANE - i + c] * cw_ref[GDN_CONV - 1 - i:GDN_CONV - i, :]
        return _silu(out)

    cpt = GDN_CHUNKS_PER_TRIP if n_chunks % GDN_CHUNKS_PER_TRIP == 0 else 1
    cols = [slice(i * DH, (i + 1) * DH) for i in heads]

    def body(it, carry):
        cb = (cbq_ref, cbk_ref, cbv_ref) if has_state else (None, None, None)
        q, k, v, gate, beta, gcum, g_last, decay = [], [], [], [], [], [], [], []
        for u in range(cpt):
            ci = it * cpt + u
            qa = conv_silu(q_ref, cb[0], cwq_ref, ci)
            ka = conv_silu(k_ref, cb[1], cwk_ref, ci)
            va = conv_silu(v_ref, cb[2], cwv_ref, ci)
            if padded:
                ba = jnp.concatenate([ba_ref[0, 0], jnp.zeros((c - t, LANE), F32)], axis=0)
                gate.append(jnp.concatenate([gate_ref[0], jnp.zeros((c - t, wid), F32)], axis=0))
                live = row_id < t
            else:
                ba = ba_ref[0, 0, pl.ds(pl.multiple_of(ci * c, c), c), :]
                gate.append(gate_ref[0, pl.ds(pl.multiple_of(ci * c, c), c), :])
            beta_all = jax.nn.sigmoid(ba)
            z = ba + dt_ref[0]
            softplus = jnp.maximum(z, 0.0) + jnp.log1p(jnp.exp(-jnp.abs(z)))
            g_all = -jnp.exp(alog_ref[0]) * softplus
            if padded:
                beta_all = jnp.where(live, beta_all, 0.0)
                g_all = jnp.where(live, g_all, 0.0)
            gcum_all = _cumsum_rows(g_all)
            gcum_t = gcum_all.T
            for i in heads:
                qi = qa[:, cols[i]]
                ki = ka[:, cols[i]]
                vi = va[:, cols[i]]
                qi = qi * lax.rsqrt(jnp.sum(qi * qi, axis=-1, keepdims=True) + EPS) * (DH ** -0.5)
                ki = ki * lax.rsqrt(jnp.sum(ki * ki, axis=-1, keepdims=True) + EPS)
                if padded:
                    qi = jnp.where(live, qi, 0.0)
                    ki = jnp.where(live, ki, 0.0)
                    vi = jnp.where(live, vi, 0.0)
                q.append(qi)
                k.append(ki)
                v.append(vi)
                beta.append(beta_all[:, i:i + 1])
                gc = gcum_all[:, hg + i:hg + i + 1]
                gcum.append(gc)
                g_last.append(gcum_all[c - 1:c, hg + i:hg + i + 1])
                decay.append(jnp.where(
                    tril, jnp.exp(jnp.where(tril, gc - gcum_t[hg + i:hg + i + 1, :], 0.0)), 0.0))
        chains = range(cpt * hg)
        e_cum = [jnp.exp(g) for g in gcum]
        kb = [k[n] * beta[n] for n in chains]
        k16 = [x.astype(BF16) for x in k]
        kk = [_dot_nt(kb[n].astype(BF16), k16[n]) for n in chains]
        qk = [_dot_nt(q[n].astype(BF16), k16[n]) for n in chains]
        m_mat = [jnp.where(strict, kk[n] * decay[n], 0.0) for n in chains]
        a_qk = [(qk[n] * decay[n]).astype(BF16) for n in chains]
        rhs = [jnp.concatenate([v[n] * beta[n], kb[n] * e_cum[n]], axis=1) for n in chains]
        sol = _unit_lower_solve(m_mat, rhs, eye, sub_diag)
        q_dec = [(q[n] * e_cum[n]).astype(BF16) for n in chains]
        k_dec_t = [(k[n] * jnp.exp(g_last[n] - gcum[n])).T.astype(BF16) for n in chains]
        state = [s_ref[0, i] for i in heads]
        for u in range(cpt):
            ns = [u * hg + i for i in heads]
            s16 = [x.astype(BF16) for x in state]
            w_s = [_dot(sol[ns[i]][:, DH:].astype(BF16), s16[i]) for i in heads]
            q_s = [_dot(q_dec[ns[i]], s16[i]) for i in heads]
            vn16 = [(sol[ns[i]][:, :DH] - w_s[i]).astype(BF16) for i in heads]
            o = [q_s[i] + _dot(a_qk[ns[i]], vn16[i]) for i in heads]
            upd = [_dot(k_dec_t[ns[i]], vn16[i]) for i in heads]
            state = [state[i] * jnp.exp(g_last[ns[i]]) + upd[i] for i in heads]
            y = jnp.concatenate(
                [_rms_rows(o[i]) * gain_ref[...] * _silu(gate[u][:, cols[i]]) for i in heads], axis=1)
            if padded:
                y_ref[0] = y[:t].astype(y_ref.dtype)
            else:
                y_ref[0, pl.ds(pl.multiple_of((it * cpt + u) * c, c), c), :] = y.astype(y_ref.dtype)
        for i in heads:
            s_ref[0, i] = state[i]
        return carry

    lax.fori_loop(0, n_chunks // cpt, body, 0)


def gdn_group_tail(ba, batch, seq):
    hg = GDN_HEADS_PER_STEP
    groups = H_C // hg
    br = ba[:, :H_C].reshape(batch, seq, groups, hg)
    ar = ba[:, H_C:2 * H_C].reshape(batch, seq, groups, hg)
    cat = jnp.concatenate([br, ar], axis=-1).transpose(0, 2, 1, 3)
    return jnp.pad(cat, ((0, 0), (0, 0), (0, 0), (0, LANE - 2 * hg)))


def gdn_group_vec(v):
    hg = GDN_HEADS_PER_STEP
    return jnp.pad(v.reshape(H_C // hg, 1, hg), ((0, 0), (0, 0), (hg, LANE - 2 * hg)))


def gdn_mixer(proj3, bag, conv_w, alog_vec, dt_vec, out_gain, conv_buf, state0, out_dtype):
    b, t, _ = proj3.shape
    hg = GDN_HEADS_PER_STEP
    wid = hg * DH
    has_state = state0 is not None
    col = lambda off: (lambda i, j: (i, 0, off // wid + j))
    wcol = lambda off: (lambda i, j: (0, off // wid + j))
    in_specs = [pl.BlockSpec((1, t, wid), col(OFF_C)),
                pl.BlockSpec((1, t, wid), col(OFF_C + D_C)),
                pl.BlockSpec((1, t, wid), col(OFF_C + 2 * D_C)),
                pl.BlockSpec((1, t, wid), col(OFF_G)),
                pl.BlockSpec((1, 1, t, LANE), lambda i, j: (i, j, 0, 0)),
                pl.BlockSpec((GDN_CONV, wid), wcol(0)),
                pl.BlockSpec((GDN_CONV, wid), wcol(D_C)),
                pl.BlockSpec((GDN_CONV, wid), wcol(2 * D_C)),
                pl.BlockSpec((1, 1, LANE), lambda i, j: (j, 0, 0)),
                pl.BlockSpec((1, 1, LANE), lambda i, j: (j, 0, 0)),
                pl.BlockSpec((1, DH), lambda i, j: (0, 0))]
    args = [proj3, proj3, proj3, proj3, bag, conv_w, conv_w, conv_w, alog_vec, dt_vec, out_gain]
    if has_state:
        in_specs += [pl.BlockSpec((1, GDN_CONV - 1, wid), col(0)),
                     pl.BlockSpec((1, GDN_CONV - 1, wid), col(D_C)),
                     pl.BlockSpec((1, GDN_CONV - 1, wid), col(2 * D_C)),
                     pl.BlockSpec((1, hg, DH, DH), lambda i, j: (i, j, 0, 0))]
        args += [conv_buf, conv_buf, conv_buf, state0]
    vmem = 2 * (4 * t * wid * 4 + t * LANE * 4 + t * wid * 4 + 2 * hg * DH * DH * 4) + (4 << 20)
    return pl.pallas_call(
        functools.partial(_gdn_kernel, t=t, has_state=has_state),
        grid=(b, H_C // hg),
        in_specs=in_specs,
        out_specs=[pl.BlockSpec((1, t, wid), lambda i, j: (i, 0, j)),
                   pl.BlockSpec((1, hg, DH, DH), lambda i, j: (i, j, 0, 0))],
        out_shape=[jax.ShapeDtypeStruct((b, t, D_C), out_dtype),
                   jax.ShapeDtypeStruct((b, H_C, DH, DH), F32)],
        compiler_params=_cparams(("parallel", "parallel"), vmem),
        name="gdn_mixer",
    )(*args)


def _proj_out_kernel(ya_ref, yb_ref, yc_ref, w_ref, x_ref, o_ref):
    acc = _dot(ya_ref[...], w_ref[0:D_A, :])
    acc = acc + _dot(yb_ref[...], w_ref[D_A:D_A + D_B, :])
    acc = acc + _dot(yc_ref[...], w_ref[D_A + D_B:, :])
    o_ref[...] = x_ref[...] + acc


def proj_out(ya, yb, yc, w_all, layer, x):
    m, n = x.shape
    tm = min(m, 1024)
    tn = 512
    kk = D_A + D_B + D_C
    vmem = 2 * (tm * kk * 2 + kk * tn * 2 + 2 * tm * tn * 4)
    row = lambda i, j: (i, 0)
    return pl.pallas_call(
        _proj_out_kernel,
        grid=(m // tm, n // tn),
        in_specs=[pl.BlockSpec((tm, D_A), row), pl.BlockSpec((tm, D_B), row),
                  pl.BlockSpec((tm, D_C), row),
                  pl.BlockSpec((None, kk, tn), lambda i, j: (layer, 0, j)),
                  pl.BlockSpec((tm, tn), lambda i, j: (i, j))],
        out_specs=pl.BlockSpec((tm, tn), lambda i, j: (i, j)),
        out_shape=jax.ShapeDtypeStruct((m, n), F32),
        compiler_params=_cparams(("parallel", "parallel"), vmem),
        name="proj_out",
    )(ya, yb, yc, w_all, x)


FFN_TN = 256


def _ffn_conv(cur, ext_ref, cw_ref, b_ref, rows):
    out = cur * cw_ref[FFN_CONV - 1:FFN_CONV, :] + b_ref[...]
    for i in range(1, FFN_CONV):
        out = out + ext_ref[SUBLANE - i:SUBLANE - i + rows, :] * cw_ref[FFN_CONV - 1 - i:FFN_CONV - i, :]
    return out


def _shift_rows(u, halo, i):
    n, w = u.shape
    rot = pltpu.roll(u.reshape(n // SUBLANE, SUBLANE, w), i, axis=1)
    above = jnp.concatenate([pltpu.roll(halo, i, axis=0)[None], rot[:-1]], axis=0)
    sub = lax.broadcasted_iota(jnp.int32, (1, SUBLANE, 1), 1)
    return jnp.where(sub < i, above, rot).reshape(n, w)


def _ffn_conv_rows(u, halo, cw_ref, b_ref):
    out = u * cw_ref[FFN_CONV - 1:FFN_CONV, :] + b_ref[...]
    for i in range(1, FFN_CONV):
        out = out + _shift_rows(u, halo, i) * cw_ref[FFN_CONV - 1 - i:FFN_CONV - i, :]
    return out


FFN_ROW_SPLIT = 2


def _ffn_up_prompt_kernel(x_ref, wg_ref, wv_ref, cwg_ref, cwv_ref, bg_ref, bv_ref,
                          act_ref, tg_ref, tv_ref, halo_ref, *, tiles_per_seq):
    tm = x_ref.shape[0]
    rows = tm // FFN_ROW_SPLIT
    mi = pl.program_id(0)
    ni = pl.program_id(1)

    @pl.when(mi % tiles_per_seq == 0)
    def _():
        halo_ref[ni] = jnp.zeros(halo_ref.shape[1:], F32)

    halo_g = halo_ref[ni, 0]
    halo_v = halo_ref[ni, 1]
    ug = []
    uv = []
    for s in range(FFN_ROW_SPLIT):
        x = x_ref[s * rows:(s + 1) * rows, :]
        ug.append(_dot(x, wg_ref[...]))
        uv.append(_dot(x, wv_ref[...]))
    for s in range(FFN_ROW_SPLIT):
        gt = _ffn_conv_rows(ug[s], halo_g, cwg_ref, bg_ref)
        vl = _ffn_conv_rows(uv[s], halo_v, cwv_ref, bv_ref)
        act_ref[s * rows:(s + 1) * rows, :] = (_silu(gt) * vl).astype(act_ref.dtype)
        halo_g = ug[s][rows - SUBLANE:]
        halo_v = uv[s][rows - SUBLANE:]
    halo_ref[ni, 0] = halo_g
    halo_ref[ni, 1] = halo_v
    tg_ref[0] = halo_g
    tv_ref[0] = halo_v


def ffn_up_prompt(xn, w_up, conv_w, conv_b, layer, seq):
    m, k = xn.shape
    d_ff = w_up.shape[2] // 2
    tn = FFN_TN
    nt = d_ff // tn
    tm = min(seq, 1024)
    mt = m // tm
    lo = lambda i, j: (layer, 0, j)
    hi = lambda i, j: (layer, 0, nt + j)
    vmem = (2 * (tm * k * 2 + 2 * k * tn * 2 + tm * tn * 2) + nt * 2 * SUBLANE * tn * 4
            + 8 * tm * tn * 4)
    return pl.pallas_call(
        functools.partial(_ffn_up_prompt_kernel, tiles_per_seq=seq // tm),
        grid=(mt, nt),
        in_specs=[pl.BlockSpec((tm, k), lambda i, j: (i, 0)),
                  pl.BlockSpec((None, k, tn), lo), pl.BlockSpec((None, k, tn), hi),
                  pl.BlockSpec((None, FFN_CONV, tn), lo), pl.BlockSpec((None, FFN_CONV, tn), hi),
                  pl.BlockSpec((None, 1, tn), lo), pl.BlockSpec((None, 1, tn), hi)],
        out_specs=[pl.BlockSpec((tm, tn), lambda i, j: (i, j)),
                   pl.BlockSpec((1, SUBLANE, tn), lambda i, j: (i, 0, j)),
                   pl.BlockSpec((1, SUBLANE, tn), lambda i, j: (i, 0, j))],
        out_shape=[jax.ShapeDtypeStruct((m, d_ff), BF16),
                   jax.ShapeDtypeStruct((mt, SUBLANE, d_ff), F32),
                   jax.ShapeDtypeStruct((mt, SUBLANE, d_ff), F32)],
        scratch_shapes=[pltpu.VMEM((nt, 2, SUBLANE, tn), F32)],
        compiler_params=_cparams(("arbitrary", "arbitrary"), vmem),
        name="ffn_up_prompt",
    )(xn, w_up, w_up, conv_w, conv_w, conv_b, conv_b)


def _ffn_up_sample_kernel(x_ref, wg_ref, wv_ref, cwg_ref, cwv_ref, bg_ref, bv_ref, sg_ref, sv_ref,
                          act_ref, ug_ref, uv_ref, ext_ref, *, t):
    nb = x_ref.shape[0] // t
    assert t == SUBLANE
    x = x_ref[...]
    ug = _dot(x, wg_ref[...])
    uv = _dot(x, wv_ref[...])
    ug_ref[...] = ug
    uv_ref[...] = uv

    def conv(u, st_ref, cw_ref, b_ref):
        outs = []
        for bi in range(nb):
            cur = u[bi * t:(bi + 1) * t]
            ext_ref[SUBLANE - (FFN_CONV - 1):SUBLANE, :] = st_ref[bi]
            ext_ref[SUBLANE:, :] = cur
            outs.append(_ffn_conv(cur, ext_ref, cw_ref, b_ref, t))
        return jnp.concatenate(outs, axis=0)

    gt = conv(ug, sg_ref, cwg_ref, bg_ref)
    vl = conv(uv, sv_ref, cwv_ref, bv_ref)
    act_ref[...] = (_silu(gt) * vl).astype(act_ref.dtype)


def ffn_up_sample(xn, w_up, conv_w, conv_b, conv_state, layer, t):
    m, k = xn.shape
    d_ff = w_up.shape[2] // 2
    tn = FFN_TN
    nt = d_ff // tn
    nb = m // t
    lo = lambda j: (layer, 0, j)
    hi = lambda j: (layer, 0, nt + j)
    vmem = 2 * (m * k * 2 + 2 * k * tn * 2 + 4 * m * tn * 4) + (2 << 20)
    return pl.pallas_call(
        functools.partial(_ffn_up_sample_kernel, t=t),
        grid=(nt,),
        in_specs=[pl.BlockSpec((m, k), lambda j: (0, 0)),
                  pl.BlockSpec((None, k, tn), lo), pl.BlockSpec((None, k, tn), hi),
                  pl.BlockSpec((None, FFN_CONV, tn), lo), pl.BlockSpec((None, FFN_CONV, tn), hi),
                  pl.BlockSpec((None, 1, tn), lo), pl.BlockSpec((None, 1, tn), hi),
                  pl.BlockSpec((None, nb, FFN_CONV - 1, tn), lambda j: (layer, 0, 0, j)),
                  pl.BlockSpec((None, nb, FFN_CONV - 1, tn), lambda j: (layer, 0, 0, nt + j))],
        out_specs=[pl.BlockSpec((m, tn), lambda j: (0, j))] * 3,
        out_shape=[jax.ShapeDtypeStruct((m, d_ff), BF16),
                   jax.ShapeDtypeStruct((m, d_ff), F32),
                   jax.ShapeDtypeStruct((m, d_ff), F32)],
        scratch_shapes=[pltpu.VMEM((SUBLANE + t, tn), F32)],
        compiler_params=_cparams(("parallel",), vmem),
        name="ffn_up_sample",
    )(xn, w_up, w_up, conv_w, conv_w, conv_b, conv_b, conv_state, conv_state)


def _ffn_down_kernel(a_ref, w_ref, h_ref, o_ref):
    o_ref[...] = h_ref[...] + _dot(a_ref[...], w_ref[...])


def ffn_down_proj(act, w_down, layer, h):
    m, k = act.shape
    n = w_down.shape[2]
    tm = min(m, 512)
    tn = 512
    vmem = 2 * (tm * k * 2 + k * tn * 2 + 2 * tm * tn * 4)
    return pl.pallas_call(
        _ffn_down_kernel,
        grid=(m // tm, n // tn),
        in_specs=[pl.BlockSpec((tm, k), lambda i, j: (i, 0)),
                  pl.BlockSpec((None, k, tn), lambda i, j: (layer, 0, j)),
                  pl.BlockSpec((tm, tn), lambda i, j: (i, j))],
        out_specs=pl.BlockSpec((tm, tn), lambda i, j: (i, j)),
        out_shape=jax.ShapeDtypeStruct((m, n), F32),
        compiler_params=_cparams(("parallel", "parallel"), vmem),
        name="ffn_down",
    )(act, w_down, h)


def _t5_bucket(dist):
    max_exact = REL_BUCKETS // 2
    d = jnp.maximum(dist, 1).astype(F32)
    large = max_exact + (jnp.log(d / max_exact) / math.log(REL_MAX_DIST / max_exact)
                         * (REL_BUCKETS - max_exact)).astype(jnp.int32)
    large = jnp.minimum(large, REL_BUCKETS - 1)
    return jnp.where(dist < max_exact, dist, large)


def _bias_lookup(rel_bias, buckets):
    onehot = jax.nn.one_hot(buckets, REL_BUCKETS, dtype=F32)
    return jnp.einsum('...k,kh->...h', onehot, rel_bias.astype(F32), precision=HIGHEST)


def _prompt_bias_blocks(rel_bias):
    qi = np.arange(BLK)[:, None]
    ki = np.arange(2 * BLK)[None, :]
    rel = qi + BLK - ki
    out = []
    for window, dil in A_BRANCHES:
        nj = window // dil + 1
        valid = (rel >= 0) & (rel < nj)
        buckets = _t5_bucket(jnp.asarray(np.clip(rel, 0, nj - 1) * dil, jnp.int32))
        bias = _bias_lookup(rel_bias, buckets)
        out.append(jnp.where(jnp.asarray(valid)[..., None], bias, NEG_INF).transpose(2, 0, 1))
    return jnp.stack(out)


def _branch_count(dist):
    cnt = np.zeros(dist.shape, np.float32)
    for window, dil in A_BRANCHES:
        cnt += ((dist >= 0) & (dist % dil == 0) & (dist // dil <= window // dil)).astype(np.float32)
    return cnt


def _sample_bias_tables(rel_bias, t, n_past):
    tq = np.arange(t)[:, None]
    d_cache = n_past + tq - np.arange(n_past)[None, :]
    d_new = tq - np.arange(BLK)[None, :]
    d_new = np.where(np.arange(BLK)[None, :] < t, d_new, -1)
    tables = []
    for dist in (d_cache, d_new):
        cnt = _branch_count(dist)
        bias = _bias_lookup(rel_bias, _t5_bucket(jnp.asarray(np.maximum(dist, 0), jnp.int32)))
        bias = jnp.where(jnp.asarray(cnt > 0)[..., None], bias, NEG_INF).transpose(2, 0, 1)
        tables += [bias, jnp.asarray(cnt)]
    return tables


def _layer(x, lw, gw, *, batch, seq, layer, cache_hm=None, pool_buf=None, gconv_buf=None, gstate=None,
           fconv_buf=None, bias_tabs=None):
    sample = cache_hm is not None
    mix_dtype = F32 if sample else BF16
    xn = rmsnorm_cast(x, lw['norm_mix'])
    proj, ba = proj_in(xn, gw['w_in'], gw['w_in_tail'], layer)
    proj3 = proj.reshape(batch, seq, N_MAIN)
    bag = gdn_group_tail(ba, batch, seq)
    qg = lw['a_q_norm'].reshape(1, DH)
    kg = lw['a_k_norm'].reshape(1, DH)
    og = lw['a_out_norm'].reshape(H_A, DH)
    if sample:
        ya, kv = attn_sample(proj3, cache_hm, layer, qg, kg, og, *bias_tabs)
    else:
        ya, kv = attn_prompt(proj3, qg, kg, og, bias_tabs)
    kv_rows = kv.reshape(batch, seq, H_A, 2, DH).transpose(0, 1, 3, 2, 4)
    yb = pool_mixer(proj3, pool_buf, lw['pool_w'], lw['pool_scale'].reshape(1, D_B), mix_dtype)
    pu = proj3[:, :, OFF_P:OFF_P + D_B]
    if sample:
        pool_new = jnp.concatenate([pool_buf, pu], axis=1)[:, -POOL_BUF:]
    else:
        pool_new = pu[:, -POOL_BUF:]
    yc, gstate_new = gdn_mixer(proj3, bag, lw['gdn_conv_w'], lw['alog_vec'], lw['dt_vec'],
                               lw['gdn_out_norm'].reshape(1, DH), gconv_buf, gstate, mix_dtype)
    gconv_new = proj3[:, -(GDN_CONV - 1):, OFF_C:OFF_C + 3 * D_C]
    m = batch * seq
    h = proj_out(ya.reshape(m, D_A).astype(BF16), yb.reshape(m, D_B).astype(BF16),
                 yc.reshape(m, D_C).astype(BF16), gw['w_out'], layer, x)
    hn = rmsnorm_cast(h, lw['norm_ffn'])
    if sample:
        act, ug, uv = ffn_up_sample(hn, gw['ffn_up'], gw['ffn_conv_w'], gw['ffn_conv_b'], fconv_buf,
                                    layer, seq)
        up = jnp.concatenate([ug, uv], axis=-1).reshape(batch, seq, -1)
        fconv_new = up[:, -(FFN_CONV - 1):]
    else:
        act, tg, tv = ffn_up_prompt(hn, gw['ffn_up'], gw['ffn_conv_w'], gw['ffn_conv_b'], layer, seq)
        tails = jnp.concatenate([tg, tv], axis=-1)
        tiles_per_seq = tails.shape[0] // batch
        fconv_new = tails[tiles_per_seq - 1::tiles_per_seq, -(FFN_CONV - 1):]
    y = ffn_down_proj(act, gw['ffn_down'], layer, h)
    return y, (kv_rows, pool_new, gconv_new, gstate_new, fconv_new)


def kernel(x_prompt, x_sample, cache_attn_kv, state_pool, state_gdn_conv, state_gdn, state_ffn_conv,
           rel_bias, norm_mix, w_in, a_q_norm, a_k_norm, a_out_norm, pool_w, pool_scale,
           gdn_conv_w, gdn_a_log, gdn_dt_bias, gdn_out_norm, w_out, norm_ffn,
           ffn_up, ffn_conv_w, ffn_conv_b, ffn_down):
    depth = w_in.shape[0]
    bp, sp, d_model = x_prompt.shape
    bs, ss, _ = x_sample.shape
    n_past = cache_attn_kv.shape[2]
    assert w_in.shape[2] == N_MAIN + N_TAIL and w_out.shape[1] == D_A + D_B + D_C

    prompt_bias = _prompt_bias_blocks(rel_bias)
    sample_tabs = _sample_bias_tables(rel_bias, ss, n_past)
    cache_hm = jnp.transpose(cache_attn_kv, (0, 1, 3, 4, 2, 5))

    xp = x_prompt.reshape(bp * sp, d_model)
    xs = x_sample.reshape(bs * ss, d_model)
    p_out = [[] for _ in range(5)]
    s_out = [[] for _ in range(5)]
    gw = {
        'w_in': w_in.astype(BF16),
        'w_in_tail': jnp.pad(w_in[:, :, N_MAIN:], ((0, 0), (0, 0), (0, LANE - N_TAIL))).astype(BF16),
        'w_out': w_out.astype(BF16), 'ffn_up': ffn_up.astype(BF16), 'ffn_down': ffn_down.astype(BF16),
        'ffn_conv_w': ffn_conv_w, 'ffn_conv_b': ffn_conv_b.reshape(depth, 1, -1),
    }
    for l in range(depth):
        lw = {
            'norm_mix': norm_mix[l], 'norm_ffn': norm_ffn[l],
            'a_q_norm': a_q_norm[l], 'a_k_norm': a_k_norm[l], 'a_out_norm': a_out_norm[l],
            'pool_w': pool_w[l].astype(BF16), 'pool_scale': pool_scale[l],
            'gdn_conv_w': gdn_conv_w[l],
            'alog_vec': gdn_group_vec(gdn_a_log[l]), 'dt_vec': gdn_group_vec(gdn_dt_bias[l]),
            'gdn_out_norm': gdn_out_norm[l],
        }
        xp, outs = _layer(xp, lw, gw, batch=bp, seq=sp, layer=l, bias_tabs=prompt_bias)
        for acc, o in zip(p_out, outs):
            acc.append(o)
        xs, outs = _layer(xs, lw, gw, batch=bs, seq=ss, layer=l, cache_hm=cache_hm,
                          pool_buf=state_pool[l], gconv_buf=state_gdn_conv[l], gstate=state_gdn[l],
                          fconv_buf=state_ffn_conv, bias_tabs=sample_tabs)
        for acc, o in zip(s_out, outs):
            acc.append(o)
    res = [xp.reshape(bp, sp, d_model), xs.reshape(bs, ss, d_model)]
    for po, so in zip(p_out, s_out):
        res += [jnp.stack(po), jnp.stack(so)]
    return tuple(res)
```

```python
import functools
import math

import numpy as np
import jax
import jax.numpy as jnp
from jax import lax
from jax.experimental import pallas as pl
from jax.experimental.pallas import tpu as pltpu

F32 = jnp.float32
BF16 = jnp.bfloat16
HIGHEST = lax.Precision.HIGHEST

DH = 128
H_A = 12
H_C = 12
A_BRANCHES = ((128, 1), (512, 4), (2048, 16))
BLK = 128
REL_BUCKETS = 32
REL_MAX_DIST = 2048
POOL_WINDOWS = (2, 4, 8, 16)
CG = 256
POOL_BUF = 15
GDN_CONV = 4
GDN_CHUNK = 64
GDN_SUB = 16
FFN_CONV = 3
EPS = 1e-6
NEG_INF = -1e30

D_A = H_A * DH
D_B = len(POOL_WINDOWS) * CG
D_C = H_C * DH
OFF_Q, OFF_K, OFF_V = 0, D_A, 2 * D_A
OFF_P = 3 * D_A
OFF_C = OFF_P + D_B
OFF_G = OFF_C + 3 * D_C
N_MAIN = OFF_G + D_C
N_TAIL = 2 * H_C

LANE = 128
SUBLANE = 8
VMEM_CAP = 56 * 1024 * 1024


def _cparams(sem, vmem_bytes):
    limit = int(min(max(vmem_bytes * 5 // 4 + (2 << 20), 16 << 20), VMEM_CAP))
    return pltpu.CompilerParams(dimension_semantics=sem, vmem_limit_bytes=limit)


def _rms_rows(x):
    return x * lax.rsqrt(jnp.mean(x * x, axis=-1, keepdims=True) + EPS)


def _silu(x):
    return x * jax.nn.sigmoid(x)


def _dot(a, b):
    return jnp.dot(a, b, preferred_element_type=F32)


def _dot_nt(a, b):
    return lax.dot_general(a, b, (((1,), (1,)), ((), ())), preferred_element_type=F32)


def _hdot(a, b):
    return jnp.dot(a, b, preferred_element_type=F32, precision=HIGHEST)


def _hdot_nt(a, b):
    return lax.dot_general(a, b, (((1,), (1,)), ((), ())), preferred_element_type=F32,
                           precision=HIGHEST)


def _hdot_tn(a, b):
    return lax.dot_general(a, b, (((0,), (0,)), ((), ())), preferred_element_type=F32,
                           precision=HIGHEST)


def _rmsnorm_kernel(x_ref, g_ref, o_ref):
    o_ref[...] = (_rms_rows(x_ref[...]) * g_ref[...]).astype(o_ref.dtype)


def rmsnorm_cast(x, gain):
    m, d = x.shape
    tm = min(m, 256)
    return pl.pallas_call(
        _rmsnorm_kernel,
        grid=(m // tm,),
        in_specs=[pl.BlockSpec((tm, d), lambda i: (i, 0)),
                  pl.BlockSpec((1, d), lambda i: (0, 0))],
        out_specs=pl.BlockSpec((tm, d), lambda i: (i, 0)),
        out_shape=jax.ShapeDtypeStruct((m, d), BF16),
        compiler_params=_cparams(("parallel",), 2 * tm * d * 6),
        name="rmsnorm_cast",
    )(x, gain.reshape(1, d))


MM_TM = 1024
MM_TN = 512


def _proj_in_kernel(xp_ref, xs_ref, w_ref, op_ref, os_ref, wbf_ref):
    @pl.when(pl.program_id(1) == 0)
    def _():
        wbf_ref[...] = w_ref[...].astype(BF16)
        os_ref[...] = _dot(xs_ref[...], wbf_ref[...])

    op_ref[...] = _dot(xp_ref[...], wbf_ref[...])


def proj_in(xn_p, xn_s, w_all, layer):
    mp, k = xn_p.shape
    ms = xn_s.shape[0]
    tm, tn = MM_TM, MM_TN
    vmem = 2 * (tm * k * 2 + ms * k * 2 + k * tn * 4 + tm * tn * 4 + ms * tn * 4) + 3 * k * tn * 2
    return pl.pallas_call(
        _proj_in_kernel,
        grid=(N_MAIN // tn, mp // tm),
        in_specs=[pl.BlockSpec((tm, k), lambda j, i: (i, 0)),
                  pl.BlockSpec((ms, k), lambda j, i: (0, 0)),
                  pl.BlockSpec((None, k, tn), lambda j, i: (layer, 0, j))],
        out_specs=[pl.BlockSpec((tm, tn), lambda j, i: (i, j)),
                   pl.BlockSpec((ms, tn), lambda j, i: (0, j))],
        out_shape=[jax.ShapeDtypeStruct((mp, N_MAIN), F32),
                   jax.ShapeDtypeStruct((ms, N_MAIN), F32)],
        scratch_shapes=[pltpu.VMEM((k, tn), BF16)],
        compiler_params=_cparams(("arbitrary", "arbitrary"), vmem),
        name="proj_in",
    )(xn_p, xn_s, w_all)


def _proj_tail_kernel(x_ref, w_ref, o_ref):
    o_ref[...] = _dot(x_ref[...], w_ref[...])


def proj_tail(xn, w_tail, layer):
    m, k = xn.shape
    tm = min(m, MM_TM)
    vmem = 2 * (tm * k * 2 + k * LANE * 2 + tm * LANE * 4)
    return pl.pallas_call(
        _proj_tail_kernel,
        grid=(m // tm,),
        in_specs=[pl.BlockSpec((tm, k), lambda i: (i, 0)),
                  pl.BlockSpec((None, k, LANE), lambda i: (layer, 0, 0))],
        out_specs=pl.BlockSpec((tm, LANE), lambda i: (i, 0)),
        out_shape=jax.ShapeDtypeStruct((m, LANE), F32),
        compiler_params=_cparams(("parallel",), vmem),
        name="proj_tail",
    )(xn, w_tail)


ATTN_BLOCKS_PER_TRIP = 4


def _strided_rows(start, dil):
    return pl.ds(start, BLK) if dil == 1 else pl.ds(start, BLK, stride=dil)


def _attn_prompt_kernel(q_ref, k_ref, v_ref, qg_ref, kg_ref, og_ref, bias_ref,
                        y_ref, kv_ref, qs_ref, ks_ref, acc_ref, m_ref, l_ref):
    seq = q_ref.shape[1]
    qs_ref[...] = _rms_rows(q_ref[0]) * qg_ref[...] * (DH ** -0.5)
    ks_ref[...] = _rms_rows(k_ref[0]) * kg_ref[...]
    kv_ref[0, :, 0:DH] = ks_ref[...]
    kv_ref[0, :, DH:] = v_ref[0]
    acc_ref[...] = jnp.zeros(acc_ref.shape, F32)
    l_ref[...] = jnp.zeros(l_ref.shape, F32)
    m_ref[...] = jnp.full(m_ref.shape, NEG_INF, F32)

    for bi, (window, dil) in enumerate(A_BRANCHES):
        assert window // dil == BLK and seq % (dil * BLK) == 0
        nb = seq // (dil * BLK)

        assert (dil * nb) % ATTN_BLOCKS_PER_TRIP == 0

        def body(it, carry, bi=bi, dil=dil, nb=nb):
            us = range(ATTN_BLOCKS_PER_TRIP)
            item = [it * ATTN_BLOCKS_PER_TRIP + u for u in us]
            r = [x // nb for x in item]
            n = [item[u] - r[u] * nb for u in us]
            rows = [_strided_rows(n[u] * (BLK * dil) + r[u], dil) for u in us]
            qb = [qs_ref[rw, :].astype(BF16) for rw in rows]
            kc = [ks_ref[rw, :] for rw in rows]
            vc = [v_ref[0, rw, :] for rw in rows]
            bias_cur = bias_ref[bi, 0, :, BLK:]
            if nb > 1:
                prows = [_strided_rows(jnp.maximum(n[u] - 1, 0) * (BLK * dil) + r[u], dil) for u in us]
                kk = [jnp.concatenate([ks_ref[prows[u], :], kc[u]], axis=0).astype(BF16) for u in us]
                vv = [jnp.concatenate([v_ref[0, prows[u], :], vc[u]], axis=0).astype(BF16) for u in us]
                bias_prev = bias_ref[bi, 0, :, :BLK]
                bias = [jnp.concatenate([jnp.where(n[u] > 0, bias_prev, NEG_INF), bias_cur], axis=1)
                        for u in us]
            else:
                kk = [x.astype(BF16) for x in kc]
                vv = [x.astype(BF16) for x in vc]
                bias = [bias_cur for _ in us]
            s = [_dot_nt(qb[u], kk[u]) + bias[u] for u in us]
            m_old = [m_ref[rw, :] for rw in rows]
            m_new = [jnp.maximum(m_old[u], jnp.max(s[u], axis=1, keepdims=True)) for u in us]
            p = [jnp.exp(s[u] - m_new[u][:, :1]) for u in us]
            alpha = [jnp.exp(m_old[u] - m_new[u]) for u in us]
            pv = [_dot(p[u].astype(BF16), vv[u]) for u in us]
            for u in us:
                l_ref[rows[u], :] = alpha[u] * l_ref[rows[u], :] + jnp.sum(p[u], axis=1, keepdims=True)
                acc_ref[rows[u], :] = alpha[u] * acc_ref[rows[u], :] + pv[u]
                m_ref[rows[u], :] = m_new[u]
            return carry

        lax.fori_loop(0, dil * nb // ATTN_BLOCKS_PER_TRIP, body, 0)

    o = acc_ref[...] / l_ref[...]
    y_ref[0] = (_rms_rows(o) * og_ref[0]).astype(y_ref.dtype)


def attn_prompt(proj3, q_gain, k_gain, o_gain, bias_blocks):
    b, s, _ = proj3.shape
    col = lambda off: (lambda i, h: (i, 0, off // DH + h))
    vmem = 2 * (3 * s * DH * 4 + s * DH * 2 + s * DH * 4 + 3 * BLK * 2 * BLK * 4) + 5 * s * DH * 4
    return pl.pallas_call(
        _attn_prompt_kernel,
        grid=(b, H_A),
        in_specs=[pl.BlockSpec((1, s, DH), col(OFF_Q)),
                  pl.BlockSpec((1, s, DH), col(OFF_K)),
                  pl.BlockSpec((1, s, DH), col(OFF_V)),
                  pl.BlockSpec((1, DH), lambda i, h: (0, 0)),
                  pl.BlockSpec((1, DH), lambda i, h: (0, 0)),
                  pl.BlockSpec((1, 1, DH), lambda i, h: (h, 0, 0)),
                  pl.BlockSpec((len(A_BRANCHES), 1, BLK, 2 * BLK), lambda i, h: (0, h, 0, 0))],
        out_specs=[pl.BlockSpec((1, s, DH), lambda i, h: (i, 0, h)),
                   pl.BlockSpec((1, s, 2 * DH), lambda i, h: (i, 0, h))],
        out_shape=[jax.ShapeDtypeStruct((b, s, D_A), BF16),
                   jax.ShapeDtypeStruct((b, s, 2 * D_A), F32)],
        scratch_shapes=[pltpu.VMEM((s, DH), F32)] * 5,
        compiler_params=_cparams(("parallel", "parallel"), vmem),
        name="attn_prompt",
    )(proj3, proj3, proj3, q_gain, k_gain, o_gain.reshape(H_A, 1, DH), bias_blocks)


def _attn_sample_kernel(q_ref, k_ref, v_ref, kc_ref, vc_ref, qg_ref, kg_ref, og_ref,
                        bc_ref, cc_ref, bn_ref, cn_ref, y_ref, kv_ref, kpad_ref, vpad_ref):
    t = q_ref.shape[1]
    qn = (_rms_rows(q_ref[0]) * qg_ref[...] * (DH ** -0.5)).astype(BF16)
    kn = _rms_rows(k_ref[0]) * kg_ref[...]
    kv_ref[0, :, 0:DH] = kn
    kv_ref[0, :, DH:] = v_ref[0]
    kpad_ref[...] = jnp.zeros(kpad_ref.shape, F32)
    vpad_ref[...] = jnp.zeros(vpad_ref.shape, F32)
    kpad_ref[0:t, :] = kn
    vpad_ref[0:t, :] = v_ref[0]
    s_c = _dot_nt(qn, kc_ref[0, 0, 0, 0].astype(BF16)) + bc_ref[0]
    s_n = _dot_nt(qn, kpad_ref[...].astype(BF16)) + bn_ref[0]
    m = jnp.maximum(jnp.max(s_c, axis=1, keepdims=True), jnp.max(s_n, axis=1, keepdims=True))
    p_c = jnp.exp(s_c - m) * cc_ref[...]
    p_n = jnp.exp(s_n - m) * cn_ref[...]
    den = jnp.sum(p_c, axis=1, keepdims=True) + jnp.sum(p_n, axis=1, keepdims=True)
    o = (_dot(p_c.astype(BF16), vc_ref[0, 0, 0, 0].astype(BF16))
         + _dot(p_n.astype(BF16), vpad_ref[...].astype(BF16))) / den
    y_ref[0] = _rms_rows(o) * og_ref[0]


def attn_sample(proj3, cache_hm, layer, q_gain, k_gain, o_gain, bias_c, cnt_c, bias_n, cnt_n):
    b, t, _ = proj3.shape
    n_past = cache_hm.shape[4]
    col = lambda off: (lambda i, h: (i, 0, off // DH + h))
    vmem = 2 * (2 * n_past * DH * 4 + 2 * t * n_past * 4) + 2 * BLK * DH * 4 + (1 << 20)
    return pl.pallas_call(
        _attn_sample_kernel,
        grid=(b, H_A),
        in_specs=[pl.BlockSpec((1, t, DH), col(OFF_Q)),
                  pl.BlockSpec((1, t, DH), col(OFF_K)),
                  pl.BlockSpec((1, t, DH), col(OFF_V)),
                  pl.BlockSpec((1, 1, 1, 1, n_past, DH), lambda i, h: (layer, i, 0, h, 0, 0)),
                  pl.BlockSpec((1, 1, 1, 1, n_past, DH), lambda i, h: (layer, i, 1, h, 0, 0)),
                  pl.BlockSpec((1, DH), lambda i, h: (0, 0)),
                  pl.BlockSpec((1, DH), lambda i, h: (0, 0)),
                  pl.BlockSpec((1, 1, DH), lambda i, h: (h, 0, 0)),
                  pl.BlockSpec((1, t, n_past), lambda i, h: (h, 0, 0)),
                  pl.BlockSpec((t, n_past), lambda i, h: (0, 0)),
                  pl.BlockSpec((1, t, BLK), lambda i, h: (h, 0, 0)),
                  pl.BlockSpec((t, BLK), lambda i, h: (0, 0))],
        out_specs=[pl.BlockSpec((1, t, DH), lambda i, h: (i, 0, h)),
                   pl.BlockSpec((1, t, 2 * DH), lambda i, h: (i, 0, h))],
        out_shape=[jax.ShapeDtypeStruct((b, t, D_A), F32),
                   jax.ShapeDtypeStruct((b, t, 2 * D_A), F32)],
        scratch_shapes=[pltpu.VMEM((BLK, DH), F32)] * 2,
        compiler_params=_cparams(("parallel", "parallel"), vmem),
        name="attn_sample",
    )(proj3, proj3, proj3, cache_hm, cache_hm, q_gain, k_gain, o_gain.reshape(H_A, 1, DH),
      bias_c, cnt_c, bias_n, cnt_n)


POOL_HALO = 16


def _pool_kernel(*refs, t, rows, n_valid, has_buf):
    ng = len(POOL_WINDOWS)
    u_refs = refs[:ng]
    refs = refs[ng:]
    if has_buf:
        buf_refs = refs[:ng]
        refs = refs[ng:]
    w_ref, scale_ref, y_ref, ext_ref = refs
    for g, win in enumerate(POOL_WINDOWS):
        ext_ref[0:POOL_HALO, :] = jnp.zeros((POOL_HALO, CG), F32)
        if has_buf:
            ext_ref[POOL_HALO - POOL_BUF:POOL_HALO, :] = buf_refs[g][0]
        ext_ref[POOL_HALO:, :] = u_refs[g][0]

        def chunk(ci, carry, g=g, win=win):
            base = pl.multiple_of(ci * rows, SUBLANE)
            w = ext_ref[pl.ds(base, rows + POOL_HALO), :]
            u = w[POOL_HALO:]
            tot = u
            for i in range(1, win):
                tot = tot + w[POOL_HALO - i:POOL_HALO - i + rows]
            pos = base + lax.broadcasted_iota(jnp.int32, (rows, 1), 0)
            cnt = jnp.minimum(win, n_valid + pos + 1).astype(F32)
            d = tot / cnt - u
            y = _rms_rows(_dot(d.astype(BF16), w_ref[g])) * scale_ref[:, g * CG:(g + 1) * CG]
            y_ref[0, pl.ds(base, rows), g * CG:(g + 1) * CG] = y.astype(y_ref.dtype)
            return carry

        lax.fori_loop(0, t // rows, chunk, 0)


def pool_mixer(proj3, bufs, w_pool, scale, out_dtype):
    b, t, _ = proj3.shape
    ng = len(POOL_WINDOWS)
    rows = min(t, 256)
    has_buf = bufs is not None
    in_specs = [pl.BlockSpec((1, t, CG), (lambda i, g=g: (i, 0, OFF_P // CG + g))) for g in range(ng)]
    args = [proj3] * ng
    if has_buf:
        in_specs += [pl.BlockSpec((1, POOL_BUF, CG), (lambda i, g=g: (i, 0, g))) for g in range(ng)]
        args += [bufs] * ng
    in_specs += [pl.BlockSpec((ng, CG, CG), lambda i: (0, 0, 0)),
                 pl.BlockSpec((1, D_B), lambda i: (0, 0))]
    args += [w_pool, scale]
    vmem = 2 * (ng * t * CG * 4 + t * D_B * 4 + ng * CG * CG * 2) + (t + POOL_HALO) * CG * 4
    return pl.pallas_call(
        functools.partial(_pool_kernel, t=t, rows=rows, n_valid=POOL_BUF if has_buf else 0,
                          has_buf=has_buf),
        grid=(b,),
        in_specs=in_specs,
        out_specs=pl.BlockSpec((1, t, D_B), lambda i: (i, 0, 0)),
        out_shape=jax.ShapeDtypeStruct((b, t, D_B), out_dtype),
        scratch_shapes=[pltpu.VMEM((t + POOL_HALO, CG), F32)],
        compiler_params=_cparams(("parallel",), vmem),
        name="pool_mixer",
    )(*args)


GDN_HEADS_PER_STEP = 4
GDN_CHUNKS_PER_TRIP = 4


def _split_bf16(a):
    hi = a.astype(BF16)
    return hi, (a - hi.astype(F32)).astype(BF16)


def _dot3(a, b):
    return _dot(a[0], b[0]) + (_dot(a[0], b[1]) + _dot(a[1], b[0]))


def _cumsum_rows(x):
    n = x.shape[0]
    row = lax.broadcasted_iota(jnp.int32, (n, 1), 0)
    k = 1
    while k < n:
        x = x + jnp.where(row >= k, pltpu.roll(x, k, axis=0), 0.0)
        k *= 2
    return x


def _unit_lower_solve(mats, rhss, eye, sub_diag):
    assert GDN_CHUNK // GDN_SUB == 4 and GDN_SUB == 16
    idx = range(len(mats))
    d = [jnp.where(sub_diag, m, 0.0) for m in mats]
    low = [_split_bf16(m - di) for m, di in zip(mats, d)]
    rs = [_split_bf16(r) for r in rhss]
    x = [eye - di for di in d]
    ps = [_split_bf16(di) for di in d]
    ps = [_split_bf16(_dot3(ps[i], ps[i])) for i in idx]
    for _ in range(2):
        xs = [_split_bf16(xi) for xi in x]
        x = [x[i] + _dot3(xs[i], ps[i]) for i in idx]
        ps = [_split_bf16(_dot3(ps[i], ps[i])) for i in idx]
    xs = [_split_bf16(xi) for xi in x]
    x = [x[i] + _dot3(xs[i], ps[i]) for i in idx]
    xs = [_split_bf16(xi) for xi in x]
    n = [_dot3(xs[i], low[i]) for i in idx]
    xr = [_dot3(xs[i], rs[i]) for i in idx]
    ns = [_split_bf16(ni) for ni in n]
    n2 = [_split_bf16(_dot3(ns[i], ns[i])) for i in idx]
    imn = [eye - ni for ni in n]
    y = [imn[i] + _dot3(_split_bf16(imn[i]), n2[i]) for i in idx]
    return [_dot3(_split_bf16(y[i]), _split_bf16(xr[i])) for i in idx]


def _gdn_kernel(*refs, t, has_state):
    c = GDN_CHUNK
    hg = GDN_HEADS_PER_STEP
    (q_ref, k_ref, v_ref, gate_ref, ba_ref, cwq_ref, cwk_ref, cwv_ref,
     alog_ref, dt_ref, gain_ref) = refs[:11]
    refs = refs[11:]
    if has_state:
        cbq_ref, cbk_ref, cbv_ref, s0_ref = refs[:4]
        refs = refs[4:]
    y_ref, s_ref = refs
    n_chunks = -(-t // c)
    padded = t % c != 0
    assert (not padded) or n_chunks == 1
    wid = hg * DH
    heads = range(hg)

    if has_state:
        s_ref[0] = s0_ref[0]
    else:
        s_ref[0] = jnp.zeros(s_ref.shape[1:], F32)

    ii = lax.broadcasted_iota(jnp.int32, (c, c), 0)
    jj = lax.broadcasted_iota(jnp.int32, (c, c), 1)
    tril = ii >= jj
    strict = ii > jj
    eye = (ii == jj).astype(F32)
    sub_diag = (ii // GDN_SUB) == (jj // GDN_SUB)
    row_id = lax.broadcasted_iota(jnp.int32, (c, 1), 0)

    def halo_rows(ref, cb_ref, ci):
        if n_chunks > 1:
            prev = ref[0, pl.ds(pl.multiple_of(jnp.maximum(ci * c - SUBLANE, 0), SUBLANE), SUBLANE), :]
        else:
            prev = jnp.zeros((SUBLANE, wid), F32)
        if has_state:
            pad = jnp.zeros((SUBLANE - (GDN_CONV - 1), wid), F32)
            first = jnp.concatenate([pad, cb_ref[0]], axis=0)
        else:
            first = jnp.zeros((SUBLANE, wid), F32)
        return jnp.where(ci > 0, prev, first)

    def conv_silu(ref, cb_ref, cw_ref, ci):
        if padded:
            cur = jnp.concatenate([ref[0], jnp.zeros((c - t, wid), F32)], axis=0)
        else:
            cur = ref[0, pl.ds(pl.multiple_of(ci * c, c), c), :]
        w = jnp.concatenate([halo_rows(ref, cb_ref, ci), cur], axis=0)
        out = cur * cw_ref[GDN_CONV - 1:GDN_CONV, :]
        for i in range(1, GDN_CONV):
            out = out + w[SUBLANE - i:SUBLANE - i + c] * cw_ref[GDN_CONV - 1 - i:GDN_CONV - i, :]
        return _silu(out)

    cpt = GDN_CHUNKS_PER_TRIP if n_chunks % GDN_CHUNKS_PER_TRIP == 0 else 1
    cols = [slice(i * DH, (i + 1) * DH) for i in heads]

    def body(it, carry):
        cb = (cbq_ref, cbk_ref, cbv_ref) if has_state else (None, None, None)
        q, k, v, gate, beta, gcum, g_last, decay = [], [], [], [], [], [], [], []
        for u in range(cpt):
            ci = it * cpt + u
            qa = conv_silu(q_ref, cb[0], cwq_ref, ci)
            ka = conv_silu(k_ref, cb[1], cwk_ref, ci)
            va = conv_silu(v_ref, cb[2], cwv_ref, ci)
            if padded:
                ba = jnp.concatenate([ba_ref[0, 0], jnp.zeros((c - t, LANE), F32)], axis=0)
                gate.append(jnp.concatenate([gate_ref[0], jnp.zeros((c - t, wid), F32)], axis=0))
                live = row_id < t
            else:
                ba = ba_ref[0, 0, pl.ds(pl.multiple_of(ci * c, c), c), :]
                gate.append(gate_ref[0, pl.ds(pl.multiple_of(ci * c, c), c), :])
            beta_all = jax.nn.sigmoid(ba)
            z = ba + dt_ref[0]
            softplus = jnp.maximum(z, 0.0) + jnp.log1p(jnp.exp(-jnp.abs(z)))
            g_all = -jnp.exp(alog_ref[0]) * softplus
            if padded:
                beta_all = jnp.where(live, beta_all, 0.0)
                g_all = jnp.where(live, g_all, 0.0)
            gcum_all = _cumsum_rows(g_all)
            gcum_t = gcum_all.T
            for i in heads:
                qi = qa[:, cols[i]]
                ki = ka[:, cols[i]]
                vi = va[:, cols[i]]
                qi = qi * lax.rsqrt(jnp.sum(qi * qi, axis=-1, keepdims=True) + EPS) * (DH ** -0.5)
                ki = ki * lax.rsqrt(jnp.sum(ki * ki, axis=-1, keepdims=True) + EPS)
                if padded:
                    qi = jnp.where(live, qi, 0.0)
                    ki = jnp.where(live, ki, 0.0)
                    vi = jnp.where(live, vi, 0.0)
                q.append(qi)
                k.append(ki)
                v.append(vi)
                beta.append(beta_all[:, i:i + 1])
                gc = gcum_all[:, hg + i:hg + i + 1]
                gcum.append(gc)
                g_last.append(gcum_all[c - 1:c, hg + i:hg + i + 1])
                decay.append(jnp.where(
                    tril, jnp.exp(jnp.where(tril, gc - gcum_t[hg + i:hg + i + 1, :], 0.0)), 0.0))
        chains = range(cpt * hg)
        e_cum = [jnp.exp(g) for g in gcum]
        kb = [k[n] * beta[n] for n in chains]
        k16 = [x.astype(BF16) for x in k]
        kk = [_dot_nt(kb[n].astype(BF16), k16[n]) for n in chains]
        qk = [_dot_nt(q[n].astype(BF16), k16[n]) for n in chains]
        m_mat = [jnp.where(strict, kk[n] * decay[n], 0.0) for n in chains]
        a_qk = [(qk[n] * decay[n]).astype(BF16) for n in chains]
        rhs = [jnp.concatenate([v[n] * beta[n], kb[n] * e_cum[n]], axis=1) for n in chains]
        sol = _unit_lower_solve(m_mat, rhs, eye, sub_diag)
        q_dec = [(q[n] * e_cum[n]).astype(BF16) for n in chains]
        k_dec_t = [(k[n] * jnp.exp(g_last[n] - gcum[n])).T.astype(BF16) for n in chains]
        state = [s_ref[0, i] for i in heads]
        for u in range(cpt):
            ns = [u * hg + i for i in heads]
            s16 = [x.astype(BF16) for x in state]
            w_s = [_dot(sol[ns[i]][:, DH:].astype(BF16), s16[i]) for i in heads]
            q_s = [_dot(q_dec[ns[i]], s16[i]) for i in heads]
            vn16 = [(sol[ns[i]][:, :DH] - w_s[i]).astype(BF16) for i in heads]
            o = [q_s[i] + _dot(a_qk[ns[i]], vn16[i]) for i in heads]
            upd = [_dot(k_dec_t[ns[i]], vn16[i]) for i in heads]
            state = [state[i] * jnp.exp(g_last[ns[i]]) + upd[i] for i in heads]
            y = jnp.concatenate(
                [_rms_rows(o[i]) * gain_ref[...] * _silu(gate[u][:, cols[i]]) for i in heads], axis=1)
            if padded:
                y_ref[0] = y[:t].astype(y_ref.dtype)
            else:
                y_ref[0, pl.ds(pl.multiple_of((it * cpt + u) * c, c), c), :] = y.astype(y_ref.dtype)
        for i in heads:
            s_ref[0, i] = state[i]
        return carry

    lax.fori_loop(0, n_chunks // cpt, body, 0)


def gdn_group_tail(ba, batch, seq):
    hg = GDN_HEADS_PER_STEP
    groups = H_C // hg
    br = ba[:, :H_C].reshape(batch, seq, groups, hg)
    ar = ba[:, H_C:2 * H_C].reshape(batch, seq, groups, hg)
    cat = jnp.concatenate([br, ar], axis=-1).transpose(0, 2, 1, 3)
    return jnp.pad(cat, ((0, 0), (0, 0), (0, 0), (0, LANE - 2 * hg)))


def gdn_group_vec(v):
    hg = GDN_HEADS_PER_STEP
    return jnp.pad(v.reshape(H_C // hg, 1, hg), ((0, 0), (0, 0), (hg, LANE - 2 * hg)))


def gdn_mixer(proj3, bag, conv_w, alog_vec, dt_vec, out_gain, conv_buf, state0, out_dtype):
    b, t, _ = proj3.shape
    hg = GDN_HEADS_PER_STEP
    wid = hg * DH
    has_state = state0 is not None
    col = lambda off: (lambda i, j: (i, 0, off // wid + j))
    wcol = lambda off: (lambda i, j: (0, off // wid + j))
    in_specs = [pl.BlockSpec((1, t, wid), col(OFF_C)),
                pl.BlockSpec((1, t, wid), col(OFF_C + D_C)),
                pl.BlockSpec((1, t, wid), col(OFF_C + 2 * D_C)),
                pl.BlockSpec((1, t, wid), col(OFF_G)),
                pl.BlockSpec((1, 1, t, LANE), lambda i, j: (i, j, 0, 0)),
                pl.BlockSpec((GDN_CONV, wid), wcol(0)),
                pl.BlockSpec((GDN_CONV, wid), wcol(D_C)),
                pl.BlockSpec((GDN_CONV, wid), wcol(2 * D_C)),
                pl.BlockSpec((1, 1, LANE), lambda i, j: (j, 0, 0)),
                pl.BlockSpec((1, 1, LANE), lambda i, j: (j, 0, 0)),
                pl.BlockSpec((1, DH), lambda i, j: (0, 0))]
    args = [proj3, proj3, proj3, proj3, bag, conv_w, conv_w, conv_w, alog_vec, dt_vec, out_gain]
    if has_state:
        in_specs += [pl.BlockSpec((1, GDN_CONV - 1, wid), col(0)),
                     pl.BlockSpec((1, GDN_CONV - 1, wid), col(D_C)),
                     pl.BlockSpec((1, GDN_CONV - 1, wid), col(2 * D_C)),
                     pl.BlockSpec((1, hg, DH, DH), lambda i, j: (i, j, 0, 0))]
        args += [conv_buf, conv_buf, conv_buf, state0]
    vmem = 2 * (4 * t * wid * 4 + t * LANE * 4 + t * wid * 4 + 2 * hg * DH * DH * 4) + (4 << 20)
    return pl.pallas_call(
        functools.partial(_gdn_kernel, t=t, has_state=has_state),
        grid=(b, H_C // hg),
        in_specs=in_specs,
        out_specs=[pl.BlockSpec((1, t, wid), lambda i, j: (i, 0, j)),
                   pl.BlockSpec((1, hg, DH, DH), lambda i, j: (i, j, 0, 0))],
        out_shape=[jax.ShapeDtypeStruct((b, t, D_C), out_dtype),
                   jax.ShapeDtypeStruct((b, H_C, DH, DH), F32)],
        compiler_params=_cparams(("parallel", "parallel"), vmem),
        name="gdn_mixer",
    )(*args)


def _mix_dot(ya_ref, yb_ref, yc_ref, wbf_ref):
    acc = _dot(ya_ref[...], wbf_ref[0:D_A, :])
    acc = acc + _dot(yb_ref[...], wbf_ref[D_A:D_A + D_B, :])
    return acc + _dot(yc_ref[...], wbf_ref[D_A + D_B:, :])


def _proj_out_kernel(yap_ref, ybp_ref, ycp_ref, yas_ref, ybs_ref, ycs_ref, w_ref, xp_ref, xs_ref,
                     op_ref, os_ref, wbf_ref):
    @pl.when(pl.program_id(1) == 0)
    def _():
        wbf_ref[...] = w_ref[...].astype(BF16)
        os_ref[...] = xs_ref[...] + _mix_dot(yas_ref, ybs_ref, ycs_ref, wbf_ref)

    op_ref[...] = xp_ref[...] + _mix_dot(yap_ref, ybp_ref, ycp_ref, wbf_ref)


def proj_out(mix_p, mix_s, w_all, layer, x_p, x_s):
    mp, n = x_p.shape
    ms = x_s.shape[0]
    tm, tn = MM_TM, MM_TN
    kk = D_A + D_B + D_C
    vmem = 2 * ((tm + ms) * kk * 2 + kk * tn * 4 + 2 * (tm + ms) * tn * 4) + 3 * kk * tn * 2
    rowp = lambda j, i: (i, 0)
    rows = lambda j, i: (0, 0)
    widths = (D_A, D_B, D_C)
    return pl.pallas_call(
        _proj_out_kernel,
        grid=(n // tn, mp // tm),
        in_specs=([pl.BlockSpec((tm, w), rowp) for w in widths]
                  + [pl.BlockSpec((ms, w), rows) for w in widths]
                  + [pl.BlockSpec((None, kk, tn), lambda j, i: (layer, 0, j)),
                     pl.BlockSpec((tm, tn), lambda j, i: (i, j)),
                     pl.BlockSpec((ms, tn), lambda j, i: (0, j))]),
        out_specs=[pl.BlockSpec((tm, tn), lambda j, i: (i, j)),
                   pl.BlockSpec((ms, tn), lambda j, i: (0, j))],
        out_shape=[jax.ShapeDtypeStruct((mp, n), F32), jax.ShapeDtypeStruct((ms, n), F32)],
        scratch_shapes=[pltpu.VMEM((kk, tn), BF16)],
        compiler_params=_cparams(("arbitrary", "arbitrary"), vmem),
        name="proj_out",
    )(*mix_p, *mix_s, w_all, x_p, x_s)


FFN_TN = 256


def _ffn_conv(cur, ext_ref, cw_ref, b_ref, rows):
    out = cur * cw_ref[FFN_CONV - 1:FFN_CONV, :] + b_ref[...]
    for i in range(1, FFN_CONV):
        out = out + ext_ref[SUBLANE - i:SUBLANE - i + rows, :] * cw_ref[FFN_CONV - 1 - i:FFN_CONV - i, :]
    return out


def _shift_rows(u, halo, i):
    n, w = u.shape
    rot = pltpu.roll(u.reshape(n // SUBLANE, SUBLANE, w), i, axis=1)
    above = jnp.concatenate([pltpu.roll(halo, i, axis=0)[None], rot[:-1]], axis=0)
    sub = lax.broadcasted_iota(jnp.int32, (1, SUBLANE, 1), 1)
    return jnp.where(sub < i, above, rot).reshape(n, w)


def _ffn_conv_rows(u, halo, cw_ref, b_ref):
    out = u * cw_ref[FFN_CONV - 1:FFN_CONV, :] + b_ref[...]
    for i in range(1, FFN_CONV):
        out = out + _shift_rows(u, halo, i) * cw_ref[FFN_CONV - 1 - i:FFN_CONV - i, :]
    return out


FFN_ROW_SPLIT = 2


def _ffn_up_kernel(xp_ref, xs_ref, wg_ref, wv_ref, cwg_ref, cwv_ref, bg_ref, bv_ref, sg_ref, sv_ref,
                   actp_ref, tg_ref, tv_ref, acts_ref, ugs_ref, uvs_ref,
                   wgbf_ref, wvbf_ref, halo_ref, ext_ref, *, tiles_per_seq, t):
    tm = xp_ref.shape[0]
    rows = tm // FFN_ROW_SPLIT
    mi = pl.program_id(1)

    @pl.when(mi == 0)
    def _():
        wgbf_ref[...] = wg_ref[...].astype(BF16)
        wvbf_ref[...] = wv_ref[...].astype(BF16)
        assert t == SUBLANE
        xs = xs_ref[...]
        nb = xs.shape[0] // t
        us = _dot(xs, wgbf_ref[...])
        vs = _dot(xs, wvbf_ref[...])
        ugs_ref[...] = us
        uvs_ref[...] = vs

        def conv(u, st_ref, cw_ref, b_ref):
            outs = []
            for bi in range(nb):
                cur = u[bi * t:(bi + 1) * t]
                ext_ref[SUBLANE - (FFN_CONV - 1):SUBLANE, :] = st_ref[bi]
                ext_ref[SUBLANE:, :] = cur
                outs.append(_ffn_conv(cur, ext_ref, cw_ref, b_ref, t))
            return jnp.concatenate(outs, axis=0)

        gts = conv(us, sg_ref, cwg_ref, bg_ref)
        vls = conv(vs, sv_ref, cwv_ref, bv_ref)
        acts_ref[...] = (_silu(gts) * vls).astype(acts_ref.dtype)

    @pl.when(mi % tiles_per_seq == 0)
    def _():
        halo_ref[...] = jnp.zeros(halo_ref.shape, F32)

    halo_g = halo_ref[0]
    halo_v = halo_ref[1]
    ug = []
    uv = []
    for s in range(FFN_ROW_SPLIT):
        x = xp_ref[s * rows:(s + 1) * rows, :]
        ug.append(_dot(x, wgbf_ref[...]))
        uv.append(_dot(x, wvbf_ref[...]))
    for s in range(FFN_ROW_SPLIT):
        gt = _ffn_conv_rows(ug[s], halo_g, cwg_ref, bg_ref)
        vl = _ffn_conv_rows(uv[s], halo_v, cwv_ref, bv_ref)
        actp_ref[s * rows:(s + 1) * rows, :] = (_silu(gt) * vl).astype(actp_ref.dtype)
        halo_g = ug[s][rows - SUBLANE:]
        halo_v = uv[s][rows - SUBLANE:]
    halo_ref[0] = halo_g
    halo_ref[1] = halo_v
    tg_ref[0] = halo_g
    tv_ref[0] = halo_v


def ffn_up(xn_p, xn_s, w_up, conv_w, conv_b, conv_state, layer, seq, t):
    mp, k = xn_p.shape
    ms = xn_s.shape[0]
    d_ff = w_up.shape[2] // 2
    tn = FFN_TN
    nt = d_ff // tn
    tm = min(seq, MM_TM)
    mt = mp // tm
    nb = ms // t
    lo = lambda j, i: (layer, 0, j)
    hi = lambda j, i: (layer, 0, nt + j)
    vmem = (2 * (tm * k * 2 + ms * k * 2 + 2 * k * tn * 4 + tm * tn * 2) + 2 * k * tn * 2
            + 8 * tm * tn * 4)
    return pl.pallas_call(
        functools.partial(_ffn_up_kernel, tiles_per_seq=seq // tm, t=t),
        grid=(nt, mt),
        in_specs=[pl.BlockSpec((tm, k), lambda j, i: (i, 0)),
                  pl.BlockSpec((ms, k), lambda j, i: (0, 0)),
                  pl.BlockSpec((None, k, tn), lo), pl.BlockSpec((None, k, tn), hi),
                  pl.BlockSpec((None, FFN_CONV, tn), lo), pl.BlockSpec((None, FFN_CONV, tn), hi),
                  pl.BlockSpec((None, 1, tn), lo), pl.BlockSpec((None, 1, tn), hi),
                  pl.BlockSpec((None, nb, FFN_CONV - 1, tn), lambda j, i: (layer, 0, 0, j)),
                  pl.BlockSpec((None, nb, FFN_CONV - 1, tn), lambda j, i: (layer, 0, 0, nt + j))],
        out_specs=[pl.BlockSpec((tm, tn), lambda j, i: (i, j)),
                   pl.BlockSpec((1, SUBLANE, tn), lambda j, i: (i, 0, j)),
                   pl.BlockSpec((1, SUBLANE, tn), lambda j, i: (i, 0, j)),
                   pl.BlockSpec((ms, tn), lambda j, i: (0, j)),
                   pl.BlockSpec((ms, tn), lambda j, i: (0, j)),
                   pl.BlockSpec((ms, tn), lambda j, i: (0, j))],
        out_shape=[jax.ShapeDtypeStruct((mp, d_ff), BF16),
                   jax.ShapeDtypeStruct((mt, SUBLANE, d_ff), F32),
                   jax.ShapeDtypeStruct((mt, SUBLANE, d_ff), F32),
                   jax.ShapeDtypeStruct((ms, d_ff), BF16),
                   jax.ShapeDtypeStruct((ms, d_ff), F32),
                   jax.ShapeDtypeStruct((ms, d_ff), F32)],
        scratch_shapes=[pltpu.VMEM((k, tn), BF16), pltpu.VMEM((k, tn), BF16),
                        pltpu.VMEM((2, SUBLANE, tn), F32), pltpu.VMEM((SUBLANE + t, tn), F32)],
        compiler_params=_cparams(("arbitrary", "arbitrary"), vmem),
        name="ffn_up",
    )(xn_p, xn_s, w_up, w_up, conv_w, conv_w, conv_b, conv_b, conv_state, conv_state)


def _ffn_down_kernel(a_ref, w_ref, h_ref, o_ref):
    o_ref[...] = h_ref[...] + _dot(a_ref[...], w_ref[...])


def ffn_down_proj(act, w_down, layer, h):
    m, k = act.shape
    n = w_down.shape[2]
    tm = min(m, 512)
    tn = 512
    vmem = 2 * (tm * k * 2 + k * tn * 2 + 2 * tm * tn * 4)
    return pl.pallas_call(
        _ffn_down_kernel,
        grid=(m // tm, n // tn),
        in_specs=[pl.BlockSpec((tm, k), lambda i, j: (i, 0)),
                  pl.BlockSpec((None, k, tn), lambda i, j: (layer, 0, j)),
                  pl.BlockSpec((tm, tn), lambda i, j: (i, j))],
        out_specs=pl.BlockSpec((tm, tn), lambda i, j: (i, j)),
        out_shape=jax.ShapeDtypeStruct((m, n), F32),
        compiler_params=_cparams(("parallel", "parallel"), vmem),
        name="ffn_down",
    )(act, w_down, h)


def _t5_bucket(dist):
    max_exact = REL_BUCKETS // 2
    d = jnp.maximum(dist, 1).astype(F32)
    large = max_exact + (jnp.log(d / max_exact) / math.log(REL_MAX_DIST / max_exact)
                         * (REL_BUCKETS - max_exact)).astype(jnp.int32)
    large = jnp.minimum(large, REL_BUCKETS - 1)
    return jnp.where(dist < max_exact, dist, large)


def _bias_lookup(rel_bias, buckets):
    onehot = jax.nn.one_hot(buckets, REL_BUCKETS, dtype=F32)
    return jnp.einsum('...k,kh->...h', onehot, rel_bias.astype(F32), precision=HIGHEST)


def _prompt_bias_blocks(rel_bias):
    qi = np.arange(BLK)[:, None]
    ki = np.arange(2 * BLK)[None, :]
    rel = qi + BLK - ki
    out = []
    for window, dil in A_BRANCHES:
        nj = window // dil + 1
        valid = (rel >= 0) & (rel < nj)
        buckets = _t5_bucket(jnp.asarray(np.clip(rel, 0, nj - 1) * dil, jnp.int32))
        bias = _bias_lookup(rel_bias, buckets)
        out.append(jnp.where(jnp.asarray(valid)[..., None], bias, NEG_INF).transpose(2, 0, 1))
    return jnp.stack(out)


def _branch_count(dist):
    cnt = np.zeros(dist.shape, np.float32)
    for window, dil in A_BRANCHES:
        cnt += ((dist >= 0) & (dist % dil == 0) & (dist // dil <= window // dil)).astype(np.float32)
    return cnt


def _sample_bias_tables(rel_bias, t, n_past):
    tq = np.arange(t)[:, None]
    d_cache = n_past + tq - np.arange(n_past)[None, :]
    d_new = tq - np.arange(BLK)[None, :]
    d_new = np.where(np.arange(BLK)[None, :] < t, d_new, -1)
    tables = []
    for dist in (d_cache, d_new):
        cnt = _branch_count(dist)
        bias = _bias_lookup(rel_bias, _t5_bucket(jnp.asarray(np.maximum(dist, 0), jnp.int32)))
        bias = jnp.where(jnp.asarray(cnt > 0)[..., None], bias, NEG_INF).transpose(2, 0, 1)
        tables += [bias, jnp.asarray(cnt)]
    return tables


def _mixers(proj, ba, lw, *, batch, seq, layer, cache_hm=None, pool_buf=None, gconv_buf=None,
            gstate=None, bias_tabs=None):
    sample = cache_hm is not None
    mix_dtype = F32 if sample else BF16
    proj3 = proj.reshape(batch, seq, N_MAIN)
    bag = gdn_group_tail(ba, batch, seq)
    qg = lw['a_q_norm'].reshape(1, DH)
    kg = lw['a_k_norm'].reshape(1, DH)
    og = lw['a_out_norm'].reshape(H_A, DH)
    if sample:
        ya, kv = attn_sample(proj3, cache_hm, layer, qg, kg, og, *bias_tabs)
    else:
        ya, kv = attn_prompt(proj3, qg, kg, og, bias_tabs)
    kv_rows = kv.reshape(batch, seq, H_A, 2, DH).transpose(0, 1, 3, 2, 4)
    yb = pool_mixer(proj3, pool_buf, lw['pool_w'], lw['pool_scale'].reshape(1, D_B), mix_dtype)
    pu = proj3[:, :, OFF_P:OFF_P + D_B]
    if sample:
        pool_new = jnp.concatenate([pool_buf, pu], axis=1)[:, -POOL_BUF:]
    else:
        pool_new = pu[:, -POOL_BUF:]
    yc, gstate_new = gdn_mixer(proj3, bag, lw['gdn_conv_w'], lw['alog_vec'], lw['dt_vec'],
                               lw['gdn_out_norm'].reshape(1, DH), gconv_buf, gstate, mix_dtype)
    gconv_new = proj3[:, -(GDN_CONV - 1):, OFF_C:OFF_C + 3 * D_C]
    m = batch * seq
    mix = (ya.reshape(m, D_A).astype(BF16), yb.reshape(m, D_B).astype(BF16),
           yc.reshape(m, D_C).astype(BF16))
    return mix, (kv_rows, pool_new, gconv_new, gstate_new)


def _layer(xp, xs, lw, gw, *, layer, prompt_shape, sample_shape, cache_hm, pool_buf, gconv_buf, gstate,
           fconv_state, prompt_bias, sample_tabs):
    bp, sp = prompt_shape
    bs, ss = sample_shape
    xnp = rmsnorm_cast(xp, lw['norm_mix'])
    xns = rmsnorm_cast(xs, lw['norm_mix'])
    proj_p, proj_s = proj_in(xnp, xns, gw['w_in'], layer)
    ba_p = proj_tail(xnp, gw['w_in_tail'], layer)
    ba_s = proj_tail(xns, gw['w_in_tail'], layer)
    mix_p, outs_p = _mixers(proj_p, ba_p, lw, batch=bp, seq=sp, layer=layer, bias_tabs=prompt_bias)
    mix_s, outs_s = _mixers(proj_s, ba_s, lw, batch=bs, seq=ss, layer=layer, cache_hm=cache_hm,
                            pool_buf=pool_buf, gconv_buf=gconv_buf, gstate=gstate, bias_tabs=sample_tabs)
    hp, hs = proj_out(mix_p, mix_s, gw['w_out'], layer, xp, xs)
    hnp = rmsnorm_cast(hp, lw['norm_ffn'])
    hns = rmsnorm_cast(hs, lw['norm_ffn'])
    act_p, tg, tv, act_s, ug, uv = ffn_up(hnp, hns, gw['ffn_up'], gw['ffn_conv_w'], gw['ffn_conv_b'],
                                          fconv_state, layer, sp, ss)
    tails = jnp.concatenate([tg, tv], axis=-1)
    tiles_per_seq = tails.shape[0] // bp
    fconv_p = tails[tiles_per_seq - 1::tiles_per_seq, -(FFN_CONV - 1):]
    up_s = jnp.concatenate([ug, uv], axis=-1).reshape(bs, ss, -1)
    fconv_s = up_s[:, -(FFN_CONV - 1):]
    yp = ffn_down_proj(act_p, gw['ffn_down'], layer, hp)
    ys = ffn_down_proj(act_s, gw['ffn_down'], layer, hs)
    return yp, ys, outs_p + (fconv_p,), outs_s + (fconv_s,)


def kernel(x_prompt, x_sample, cache_attn_kv, state_pool, state_gdn_conv, state_gdn, state_ffn_conv,
           rel_bias, norm_mix, w_in, a_q_norm, a_k_norm, a_out_norm, pool_w, pool_scale,
           gdn_conv_w, gdn_a_log, gdn_dt_bias, gdn_out_norm, w_out, norm_ffn,
           ffn_up, ffn_conv_w, ffn_conv_b, ffn_down):
    depth = w_in.shape[0]
    bp, sp, d_model = x_prompt.shape
    bs, ss, _ = x_sample.shape
    n_past = cache_attn_kv.shape[2]
    assert w_in.shape[2] == N_MAIN + N_TAIL and w_out.shape[1] == D_A + D_B + D_C

    prompt_bias = _prompt_bias_blocks(rel_bias)
    sample_tabs = _sample_bias_tables(rel_bias, ss, n_past)
    cache_hm = jnp.transpose(cache_attn_kv, (0, 1, 3, 4, 2, 5))

    xp = x_prompt.reshape(bp * sp, d_model)
    xs = x_sample.reshape(bs * ss, d_model)
    p_out = [[] for _ in range(5)]
    s_out = [[] for _ in range(5)]
    gw = {
        'w_in': w_in,
        'w_in_tail': jnp.pad(w_in[:, :, N_MAIN:], ((0, 0), (0, 0), (0, LANE - N_TAIL))).astype(BF16),
        'w_out': w_out, 'ffn_up': ffn_up, 'ffn_down': ffn_down.astype(BF16),
        'ffn_conv_w': ffn_conv_w, 'ffn_conv_b': ffn_conv_b.reshape(depth, 1, -1),
    }
    for l in range(depth):
        lw = {
            'norm_mix': norm_mix[l], 'norm_ffn': norm_ffn[l],
            'a_q_norm': a_q_norm[l], 'a_k_norm': a_k_norm[l], 'a_out_norm': a_out_norm[l],
            'pool_w': pool_w[l].astype(BF16), 'pool_scale': pool_scale[l],
            'gdn_conv_w': gdn_conv_w[l],
            'alog_vec': gdn_group_vec(gdn_a_log[l]), 'dt_vec': gdn_group_vec(gdn_dt_bias[l]),
            'gdn_out_norm': gdn_out_norm[l],
        }
        xp, xs, outs_p, outs_s = _layer(
            xp, xs, lw, gw, layer=l, prompt_shape=(bp, sp), sample_shape=(bs, ss), cache_hm=cache_hm,
            pool_buf=state_pool[l], gconv_buf=state_gdn_conv[l], gstate=state_gdn[l],
            fconv_state=state_ffn_conv, prompt_bias=prompt_bias, sample_tabs=sample_tabs)
        for acc, o in zip(p_out, outs_p):
            acc.append(o)
        for acc, o in zip(s_out, outs_s):
            acc.append(o)
    res = [xp.reshape(bp, sp, d_model), xs.reshape(bs, ss, d_model)]
    for po, so in zip(p_out, s_out):
        res += [jnp.stack(po), jnp.stack(so)]
    return tuple(res)
```

```python
import functools
import math

import numpy as np
import jax
import jax.numpy as jnp
from jax import lax
from jax.experimental import pallas as pl
from jax.experimental.pallas import tpu as pltpu

F32 = jnp.float32
BF16 = jnp.bfloat16
HIGHEST = lax.Precision.HIGHEST

DH = 128
H_A = 12
H_C = 12
A_BRANCHES = ((128, 1), (512, 4), (2048, 16))
BLK = 128
REL_BUCKETS = 32
REL_MAX_DIST = 2048
POOL_WINDOWS = (2, 4, 8, 16)
CG = 256
POOL_BUF = 15
GDN_CONV = 4
GDN_CHUNK = 64
GDN_SUB = 16
FFN_CONV = 3
EPS = 1e-6
NEG_INF = -1e30

D_A = H_A * DH
D_B = len(POOL_WINDOWS) * CG
D_C = H_C * DH
OFF_Q, OFF_K, OFF_V = 0, D_A, 2 * D_A
OFF_P = 3 * D_A
OFF_C = OFF_P + D_B
OFF_G = OFF_C + 3 * D_C
N_MAIN = OFF_G + D_C
N_TAIL = 2 * H_C

LANE = 128
SUBLANE = 8
VMEM_CAP = 56 * 1024 * 1024


def _cparams(sem, vmem_bytes):
    limit = int(min(max(vmem_bytes * 5 // 4 + (2 << 20), 16 << 20), VMEM_CAP))
    return pltpu.CompilerParams(dimension_semantics=sem, vmem_limit_bytes=limit)


def _rms_rows(x):
    return x * lax.rsqrt(jnp.mean(x * x, axis=-1, keepdims=True) + EPS)


def _silu(x):
    return x * jax.nn.sigmoid(x)


def _dot(a, b):
    return jnp.dot(a, b, preferred_element_type=F32)


def _dot_nt(a, b):
    return lax.dot_general(a, b, (((1,), (1,)), ((), ())), preferred_element_type=F32)


def _hdot(a, b):
    return jnp.dot(a, b, preferred_element_type=F32, precision=HIGHEST)


def _hdot_nt(a, b):
    return lax.dot_general(a, b, (((1,), (1,)), ((), ())), preferred_element_type=F32,
                           precision=HIGHEST)


def _hdot_tn(a, b):
    return lax.dot_general(a, b, (((0,), (0,)), ((), ())), preferred_element_type=F32,
                           precision=HIGHEST)


def _rmsnorm_kernel(x_ref, g_ref, o_ref):
    o_ref[...] = (_rms_rows(x_ref[...]) * g_ref[...]).astype(o_ref.dtype)


def rmsnorm_cast(x, gain):
    m, d = x.shape
    tm = min(m, 256)
    return pl.pallas_call(
        _rmsnorm_kernel,
        grid=(m // tm,),
        in_specs=[pl.BlockSpec((tm, d), lambda i: (i, 0)),
                  pl.BlockSpec((1, d), lambda i: (0, 0))],
        out_specs=pl.BlockSpec((tm, d), lambda i: (i, 0)),
        out_shape=jax.ShapeDtypeStruct((m, d), BF16),
        compiler_params=_cparams(("parallel",), 2 * tm * d * 6),
        name="rmsnorm_cast",
    )(x, gain.reshape(1, d))


MM_TM = 1024
MM_TN = 512


def _proj_in_kernel(xp_ref, xs_ref, w_ref, op_ref, os_ref):
    @pl.when(pl.program_id(1) == 0)
    def _():
        os_ref[...] = _dot(xs_ref[...], w_ref[...])

    op_ref[...] = _dot(xp_ref[...], w_ref[...])


def proj_in(xn_p, xn_s, w_all, layer):
    mp, k = xn_p.shape
    ms = xn_s.shape[0]
    tm, tn = MM_TM, MM_TN
    vmem = 2 * (tm * k * 2 + ms * k * 2 + k * tn * 2 + tm * tn * 4 + ms * tn * 4) + k * tn * 2
    return pl.pallas_call(
        _proj_in_kernel,
        grid=(N_MAIN // tn, mp // tm),
        in_specs=[pl.BlockSpec((tm, k), lambda j, i: (i, 0)),
                  pl.BlockSpec((ms, k), lambda j, i: (0, 0)),
                  pl.BlockSpec((None, k, tn), lambda j, i: (layer, 0, j))],
        out_specs=[pl.BlockSpec((tm, tn), lambda j, i: (i, j)),
                   pl.BlockSpec((ms, tn), lambda j, i: (0, j))],
        out_shape=[jax.ShapeDtypeStruct((mp, N_MAIN), F32),
                   jax.ShapeDtypeStruct((ms, N_MAIN), F32)],
        compiler_params=_cparams(("arbitrary", "arbitrary"), vmem),
        name="proj_in",
    )(xn_p, xn_s, w_all)


def _proj_tail_kernel(x_ref, w_ref, o_ref):
    o_ref[...] = _dot(x_ref[...], w_ref[...])


def proj_tail(xn, w_tail, layer):
    m, k = xn.shape
    tm = min(m, MM_TM)
    vmem = 2 * (tm * k * 2 + k * LANE * 2 + tm * LANE * 4)
    return pl.pallas_call(
        _proj_tail_kernel,
        grid=(m // tm,),
        in_specs=[pl.BlockSpec((tm, k), lambda i: (i, 0)),
                  pl.BlockSpec((None, k, LANE), lambda i: (layer, 0, 0))],
        out_specs=pl.BlockSpec((tm, LANE), lambda i: (i, 0)),
        out_shape=jax.ShapeDtypeStruct((m, LANE), F32),
        compiler_params=_cparams(("parallel",), vmem),
        name="proj_tail",
    )(xn, w_tail)


ATTN_BLOCKS_PER_TRIP = 4


def _strided_rows(start, dil):
    return pl.ds(start, BLK) if dil == 1 else pl.ds(start, BLK, stride=dil)


def _attn_prompt_kernel(q_ref, k_ref, v_ref, qg_ref, kg_ref, og_ref, bias_ref,
                        y_ref, kv_ref, qs_ref, ks_ref, acc_ref, m_ref, l_ref):
    seq = q_ref.shape[1]
    qs_ref[...] = _rms_rows(q_ref[0]) * qg_ref[...] * (DH ** -0.5)
    ks_ref[...] = _rms_rows(k_ref[0]) * kg_ref[...]
    kv_ref[0, :, 0, 0, :] = ks_ref[...]
    kv_ref[0, :, 0, 1, :] = v_ref[0]
    acc_ref[...] = jnp.zeros(acc_ref.shape, F32)
    l_ref[...] = jnp.zeros(l_ref.shape, F32)
    m_ref[...] = jnp.full(m_ref.shape, NEG_INF, F32)

    for bi, (window, dil) in enumerate(A_BRANCHES):
        assert window // dil == BLK and seq % (dil * BLK) == 0
        nb = seq // (dil * BLK)

        assert (dil * nb) % ATTN_BLOCKS_PER_TRIP == 0

        def body(it, carry, bi=bi, dil=dil, nb=nb):
            us = range(ATTN_BLOCKS_PER_TRIP)
            item = [it * ATTN_BLOCKS_PER_TRIP + u for u in us]
            r = [x // nb for x in item]
            n = [item[u] - r[u] * nb for u in us]
            rows = [_strided_rows(n[u] * (BLK * dil) + r[u], dil) for u in us]
            qb = [qs_ref[rw, :].astype(BF16) for rw in rows]
            kc = [ks_ref[rw, :] for rw in rows]
            vc = [v_ref[0, rw, :] for rw in rows]
            bias_cur = bias_ref[bi, 0, :, BLK:]
            if nb > 1:
                prows = [_strided_rows(jnp.maximum(n[u] - 1, 0) * (BLK * dil) + r[u], dil) for u in us]
                kk = [jnp.concatenate([ks_ref[prows[u], :], kc[u]], axis=0).astype(BF16) for u in us]
                vv = [jnp.concatenate([v_ref[0, prows[u], :], vc[u]], axis=0).astype(BF16) for u in us]
                bias_prev = bias_ref[bi, 0, :, :BLK]
                bias = [jnp.concatenate([jnp.where(n[u] > 0, bias_prev, NEG_INF), bias_cur], axis=1)
                        for u in us]
            else:
                kk = [x.astype(BF16) for x in kc]
                vv = [x.astype(BF16) for x in vc]
                bias = [bias_cur for _ in us]
            s = [_dot_nt(qb[u], kk[u]) + bias[u] for u in us]
            m_old = [m_ref[rw, :] for rw in rows]
            m_new = [jnp.maximum(m_old[u], jnp.max(s[u], axis=1, keepdims=True)) for u in us]
            p = [jnp.exp(s[u] - m_new[u][:, :1]) for u in us]
            alpha = [jnp.exp(m_old[u] - m_new[u]) for u in us]
            pv = [_dot(p[u].astype(BF16), vv[u]) for u in us]
            for u in us:
                l_ref[rows[u], :] = alpha[u] * l_ref[rows[u], :] + jnp.sum(p[u], axis=1, keepdims=True)
                acc_ref[rows[u], :] = alpha[u] * acc_ref[rows[u], :] + pv[u]
                m_ref[rows[u], :] = m_new[u]
            return carry

        lax.fori_loop(0, dil * nb // ATTN_BLOCKS_PER_TRIP, body, 0)

    o = acc_ref[...] / l_ref[...]
    y_ref[0] = (_rms_rows(o) * og_ref[0]).astype(y_ref.dtype)


def attn_prompt(proj3, q_gain, k_gain, o_gain, bias_blocks):
    b, s, _ = proj3.shape
    col = lambda off: (lambda i, h: (i, 0, off // DH + h))
    vmem = 2 * (3 * s * DH * 4 + s * DH * 2 + s * DH * 4 + 3 * BLK * 2 * BLK * 4) + 5 * s * DH * 4
    return pl.pallas_call(
        _attn_prompt_kernel,
        grid=(b, H_A),
        in_specs=[pl.BlockSpec((1, s, DH), col(OFF_Q)),
                  pl.BlockSpec((1, s, DH), col(OFF_K)),
                  pl.BlockSpec((1, s, DH), col(OFF_V)),
                  pl.BlockSpec((1, DH), lambda i, h: (0, 0)),
                  pl.BlockSpec((1, DH), lambda i, h: (0, 0)),
                  pl.BlockSpec((1, 1, DH), lambda i, h: (h, 0, 0)),
                  pl.BlockSpec((len(A_BRANCHES), 1, BLK, 2 * BLK), lambda i, h: (0, h, 0, 0))],
        out_specs=[pl.BlockSpec((1, s, DH), lambda i, h: (i, 0, h)),
                   pl.BlockSpec((1, s, 1, 2, DH), lambda i, h: (i, 0, h, 0, 0))],
        out_shape=[jax.ShapeDtypeStruct((b, s, D_A), BF16),
                   jax.ShapeDtypeStruct((b, s, H_A, 2, DH), F32)],
        scratch_shapes=[pltpu.VMEM((s, DH), F32)] * 5,
        compiler_params=_cparams(("parallel", "parallel"), vmem),
        name="attn_prompt",
    )(proj3, proj3, proj3, q_gain, k_gain, o_gain.reshape(H_A, 1, DH), bias_blocks)


def _attn_sample_kernel(q_ref, k_ref, v_ref, c_ref, qg_ref, kg_ref, og_ref,
                        bc_ref, cc_ref, bn_ref, cn_ref, y_ref, kv_ref, kpad_ref, vpad_ref):
    t = q_ref.shape[1]
    qn = (_rms_rows(q_ref[0]) * qg_ref[...] * (DH ** -0.5)).astype(BF16)
    kn = _rms_rows(k_ref[0]) * kg_ref[...]
    kv_ref[0, :, 0, 0, :] = kn
    kv_ref[0, :, 0, 1, :] = v_ref[0]
    kpad_ref[...] = jnp.zeros(kpad_ref.shape, F32)
    vpad_ref[...] = jnp.zeros(vpad_ref.shape, F32)
    kpad_ref[0:t, :] = kn
    vpad_ref[0:t, :] = v_ref[0]
    s_c = _dot_nt(qn, c_ref[0, 0, :, 0, 0, :].astype(BF16)) + bc_ref[0]
    s_n = _dot_nt(qn, kpad_ref[...].astype(BF16)) + bn_ref[0]
    m = jnp.maximum(jnp.max(s_c, axis=1, keepdims=True), jnp.max(s_n, axis=1, keepdims=True))
    p_c = jnp.exp(s_c - m) * cc_ref[...]
    p_n = jnp.exp(s_n - m) * cn_ref[...]
    den = jnp.sum(p_c, axis=1, keepdims=True) + jnp.sum(p_n, axis=1, keepdims=True)
    o = (_dot(p_c.astype(BF16), c_ref[0, 0, :, 0, 1, :].astype(BF16))
         + _dot(p_n.astype(BF16), vpad_ref[...].astype(BF16))) / den
    y_ref[0] = _rms_rows(o) * og_ref[0]


def attn_sample(proj3, cache_hk, layer, q_gain, k_gain, o_gain, bias_c, cnt_c, bias_n, cnt_n):
    b, t, _ = proj3.shape
    n_past = cache_hk.shape[2]
    col = lambda off: (lambda i, h: (i, 0, off // DH + h))
    vmem = 2 * (2 * n_past * DH * 4 + 2 * t * n_past * 4) + 2 * BLK * DH * 4 + (1 << 20)
    return pl.pallas_call(
        _attn_sample_kernel,
        grid=(b, H_A),
        in_specs=[pl.BlockSpec((1, t, DH), col(OFF_Q)),
                  pl.BlockSpec((1, t, DH), col(OFF_K)),
                  pl.BlockSpec((1, t, DH), col(OFF_V)),
                  pl.BlockSpec((1, 1, n_past, 1, 2, DH), lambda i, h: (layer, i, 0, h, 0, 0)),
                  pl.BlockSpec((1, DH), lambda i, h: (0, 0)),
                  pl.BlockSpec((1, DH), lambda i, h: (0, 0)),
                  pl.BlockSpec((1, 1, DH), lambda i, h: (h, 0, 0)),
                  pl.BlockSpec((1, t, n_past), lambda i, h: (h, 0, 0)),
                  pl.BlockSpec((t, n_past), lambda i, h: (0, 0)),
                  pl.BlockSpec((1, t, BLK), lambda i, h: (h, 0, 0)),
                  pl.BlockSpec((t, BLK), lambda i, h: (0, 0))],
        out_specs=[pl.BlockSpec((1, t, DH), lambda i, h: (i, 0, h)),
                   pl.BlockSpec((1, t, 1, 2, DH), lambda i, h: (i, 0, h, 0, 0))],
        out_shape=[jax.ShapeDtypeStruct((b, t, D_A), F32),
                   jax.ShapeDtypeStruct((b, t, H_A, 2, DH), F32)],
        scratch_shapes=[pltpu.VMEM((BLK, DH), F32)] * 2,
        compiler_params=_cparams(("parallel", "parallel"), vmem),
        name="attn_sample",
    )(proj3, proj3, proj3, cache_hk, q_gain, k_gain, o_gain.reshape(H_A, 1, DH),
      bias_c, cnt_c, bias_n, cnt_n)


POOL_HALO = 16


def _pool_kernel(*refs, t, rows, n_valid, has_buf):
    ng = len(POOL_WINDOWS)
    u_refs = refs[:ng]
    refs = refs[ng:]
    if has_buf:
        buf_refs = refs[:ng]
        refs = refs[ng:]
    w_ref, scale_ref, y_ref, ext_ref = refs
    for g, win in enumerate(POOL_WINDOWS):
        ext_ref[0:POOL_HALO, :] = jnp.zeros((POOL_HALO, CG), F32)
        if has_buf:
            ext_ref[POOL_HALO - POOL_BUF:POOL_HALO, :] = buf_refs[g][0]
        ext_ref[POOL_HALO:, :] = u_refs[g][0]

        def chunk(ci, carry, g=g, win=win):
            base = pl.multiple_of(ci * rows, SUBLANE)
            w = ext_ref[pl.ds(base, rows + POOL_HALO), :]
            u = w[POOL_HALO:]
            tot = u
            for i in range(1, win):
                tot = tot + w[POOL_HALO - i:POOL_HALO - i + rows]
            pos = base + lax.broadcasted_iota(jnp.int32, (rows, 1), 0)
            cnt = jnp.minimum(win, n_valid + pos + 1).astype(F32)
            d = tot / cnt - u
            y = _rms_rows(_dot(d.astype(BF16), w_ref[g])) * scale_ref[:, g * CG:(g + 1) * CG]
            y_ref[0, pl.ds(base, rows), g * CG:(g + 1) * CG] = y.astype(y_ref.dtype)
            return carry

        lax.fori_loop(0, t // rows, chunk, 0)


def pool_mixer(proj3, bufs, w_pool, scale, out_dtype):
    b, t, _ = proj3.shape
    ng = len(POOL_WINDOWS)
    rows = min(t, 256)
    has_buf = bufs is not None
    in_specs = [pl.BlockSpec((1, t, CG), (lambda i, g=g: (i, 0, OFF_P // CG + g))) for g in range(ng)]
    args = [proj3] * ng
    if has_buf:
        in_specs += [pl.BlockSpec((1, POOL_BUF, CG), (lambda i, g=g: (i, 0, g))) for g in range(ng)]
        args += [bufs] * ng
    in_specs += [pl.BlockSpec((ng, CG, CG), lambda i: (0, 0, 0)),
                 pl.BlockSpec((1, D_B), lambda i: (0, 0))]
    args += [w_pool, scale]
    vmem = 2 * (ng * t * CG * 4 + t * D_B * 4 + ng * CG * CG * 2) + (t + POOL_HALO) * CG * 4
    return pl.pallas_call(
        functools.partial(_pool_kernel, t=t, rows=rows, n_valid=POOL_BUF if has_buf else 0,
                          has_buf=has_buf),
        grid=(b,),
        in_specs=in_specs,
        out_specs=pl.BlockSpec((1, t, D_B), lambda i: (i, 0, 0)),
        out_shape=jax.ShapeDtypeStruct((b, t, D_B), out_dtype),
        scratch_shapes=[pltpu.VMEM((t + POOL_HALO, CG), F32)],
        compiler_params=_cparams(("parallel",), vmem),
        name="pool_mixer",
    )(*args)


GDN_HEADS_PER_STEP = 4
GDN_CHUNKS_PER_TRIP = 4


def _split_bf16(a):
    hi = a.astype(BF16)
    return hi, (a - hi.astype(F32)).astype(BF16)


def _dot3(a, b):
    return _dot(a[0], b[0]) + (_dot(a[0], b[1]) + _dot(a[1], b[0]))


def _cumsum_rows(x):
    n = x.shape[0]
    row = lax.broadcasted_iota(jnp.int32, (n, 1), 0)
    k = 1
    while k < n:
        x = x + jnp.where(row >= k, pltpu.roll(x, k, axis=0), 0.0)
        k *= 2
    return x


def _unit_lower_solve(mats, rhss, eye, sub_diag):
    assert GDN_CHUNK // GDN_SUB == 4 and GDN_SUB == 16
    idx = range(len(mats))
    d = [jnp.where(sub_diag, m, 0.0) for m in mats]
    low = [_split_bf16(m - di) for m, di in zip(mats, d)]
    rs = [_split_bf16(r) for r in rhss]
    x = [eye - di for di in d]
    ps = [_split_bf16(di) for di in d]
    ps = [_split_bf16(_dot3(ps[i], ps[i])) for i in idx]
    for _ in range(2):
        xs = [_split_bf16(xi) for xi in x]
        x = [x[i] + _dot3(xs[i], ps[i]) for i in idx]
        ps = [_split_bf16(_dot3(ps[i], ps[i])) for i in idx]
    xs = [_split_bf16(xi) for xi in x]
    x = [x[i] + _dot3(xs[i], ps[i]) for i in idx]
    xs = [_split_bf16(xi) for xi in x]
    n = [_dot3(xs[i], low[i]) for i in idx]
    xr = [_dot3(xs[i], rs[i]) for i in idx]
    ns = [_split_bf16(ni) for ni in n]
    n2 = [_split_bf16(_dot3(ns[i], ns[i])) for i in idx]
    imn = [eye - ni for ni in n]
    y = [imn[i] + _dot3(_split_bf16(imn[i]), n2[i]) for i in idx]
    return [_dot3(_split_bf16(y[i]), _split_bf16(xr[i])) for i in idx]


def _gdn_kernel(*refs, t, has_state):
    c = GDN_CHUNK
    hg = GDN_HEADS_PER_STEP
    (q_ref, k_ref, v_ref, gate_ref, ba_ref, cwq_ref, cwk_ref, cwv_ref,
     alog_ref, dt_ref, gain_ref) = refs[:11]
    refs = refs[11:]
    if has_state:
        cbq_ref, cbk_ref, cbv_ref, s0_ref = refs[:4]
        refs = refs[4:]
    y_ref, s_ref = refs
    n_chunks = -(-t // c)
    padded = t % c != 0
    assert (not padded) or n_chunks == 1
    wid = hg * DH
    heads = range(hg)

    if has_state:
        s_ref[0] = s0_ref[0]
    else:
        s_ref[0] = jnp.zeros(s_ref.shape[1:], F32)

    ii = lax.broadcasted_iota(jnp.int32, (c, c), 0)
    jj = lax.broadcasted_iota(jnp.int32, (c, c), 1)
    tril = ii >= jj
    strict = ii > jj
    eye = (ii == jj).astype(F32)
    sub_diag = (ii // GDN_SUB) == (jj // GDN_SUB)
    row_id = lax.broadcasted_iota(jnp.int32, (c, 1), 0)

    def halo_rows(ref, cb_ref, ci):
        if n_chunks > 1:
            prev = ref[0, pl.ds(pl.multiple_of(jnp.maximum(ci * c - SUBLANE, 0), SUBLANE), SUBLANE), :]
        else:
            prev = jnp.zeros((SUBLANE, wid), F32)
        if has_state:
            pad = jnp.zeros((SUBLANE - (GDN_CONV - 1), wid), F32)
            first = jnp.concatenate([pad, cb_ref[0]], axis=0)
        else:
            first = jnp.zeros((SUBLANE, wid), F32)
        return jnp.where(ci > 0, prev, first)

    def conv_silu(ref, cb_ref, cw_ref, ci):
        if padded:
            cur = jnp.concatenate([ref[0], jnp.zeros((c - t, wid), F32)], axis=0)
        else:
            cur = ref[0, pl.ds(pl.multiple_of(ci * c, c), c), :]
        w = jnp.concatenate([halo_rows(ref, cb_ref, ci), cur], axis=0)
        out = cur * cw_ref[GDN_CONV - 1:GDN_CONV, :]
        for i in range(1, GDN_CONV):
            out = out + w[SUBLANE - i:SUBLANE - i + c] * cw_ref[GDN_CONV - 1 - i:GDN_CONV - i, :]
        return _silu(out)

    cpt = GDN_CHUNKS_PER_TRIP if n_chunks % GDN_CHUNKS_PER_TRIP == 0 else 1
    cols = [slice(i * DH, (i + 1) * DH) for i in heads]

    def body(it, carry):
        cb = (cbq_ref, cbk_ref, cbv_ref) if has_state else (None, None, None)
        q, k, v, gate, beta, gcum, g_last, decay = [], [], [], [], [], [], [], []
        for u in range(cpt):
            ci = it * cpt + u
            qa = conv_silu(q_ref, cb[0], cwq_ref, ci)
            ka = conv_silu(k_ref, cb[1], cwk_ref, ci)
            va = conv_silu(v_ref, cb[2], cwv_ref, ci)
            if padded:
                ba = jnp.concatenate([ba_ref[0, 0], jnp.zeros((c - t, LANE), F32)], axis=0)
                gate.append(jnp.concatenate([gate_ref[0], jnp.zeros((c - t, wid), F32)], axis=0))
                live = row_id < t
            else:
                ba = ba_ref[0, 0, pl.ds(pl.multiple_of(ci * c, c), c), :]
                gate.append(gate_ref[0, pl.ds(pl.multiple_of(ci * c, c), c), :])
            beta_all = jax.nn.sigmoid(ba)
            z = ba + dt_ref[0]
            softplus = jnp.maximum(z, 0.0) + jnp.log1p(jnp.exp(-jnp.abs(z)))
            g_all = -jnp.exp(alog_ref[0]) * softplus
            if padded:
                beta_all = jnp.where(live, beta_all, 0.0)
                g_all = jnp.where(live, g_all, 0.0)
            gcum_all = _cumsum_rows(g_all)
            gcum_t = gcum_all.T
            for i in heads:
                qi = qa[:, cols[i]]
                ki = ka[:, cols[i]]
                vi = va[:, cols[i]]
                qi = qi * lax.rsqrt(jnp.sum(qi * qi, axis=-1, keepdims=True) + EPS) * (DH ** -0.5)
                ki = ki * lax.rsqrt(jnp.sum(ki * ki, axis=-1, keepdims=True) + EPS)
                if padded:
                    qi = jnp.where(live, qi, 0.0)
                    ki = jnp.where(live, ki, 0.0)
                    vi = jnp.where(live, vi, 0.0)
                q.append(qi)
                k.append(ki)
                v.append(vi)
                beta.append(beta_all[:, i:i + 1])
                gc = gcum_all[:, hg + i:hg + i + 1]
                gcum.append(gc)
                g_last.append(gcum_all[c - 1:c, hg + i:hg + i + 1])
                decay.append(jnp.where(
                    tril, jnp.exp(jnp.where(tril, gc - gcum_t[hg + i:hg + i + 1, :], 0.0)), 0.0))
        chains = range(cpt * hg)
        e_cum = [jnp.exp(g) for g in gcum]
        kb = [k[n] * beta[n] for n in chains]
        k16 = [x.astype(BF16) for x in k]
        kk = [_dot_nt(kb[n].astype(BF16), k16[n]) for n in chains]
        qk = [_dot_nt(q[n].astype(BF16), k16[n]) for n in chains]
        m_mat = [jnp.where(strict, kk[n] * decay[n], 0.0) for n in chains]
        a_qk = [(qk[n] * decay[n]).astype(BF16) for n in chains]
        rhs = [jnp.concatenate([v[n] * beta[n], kb[n] * e_cum[n]], axis=1) for n in chains]
        sol = _unit_lower_solve(m_mat, rhs, eye, sub_diag)
        q_dec = [(q[n] * e_cum[n]).astype(BF16) for n in chains]
        k_dec_t = [(k[n] * jnp.exp(g_last[n] - gcum[n])).T.astype(BF16) for n in chains]
        state = [s_ref[0, i] for i in heads]
        for u in range(cpt):
            ns = [u * hg + i for i in heads]
            s16 = [x.astype(BF16) for x in state]
            w_s = [_dot(sol[ns[i]][:, DH:].astype(BF16), s16[i]) for i in heads]
            q_s = [_dot(q_dec[ns[i]], s16[i]) for i in heads]
            vn16 = [(sol[ns[i]][:, :DH] - w_s[i]).astype(BF16) for i in heads]
            o = [q_s[i] + _dot(a_qk[ns[i]], vn16[i]) for i in heads]
            upd = [_dot(k_dec_t[ns[i]], vn16[i]) for i in heads]
            state = [state[i] * jnp.exp(g_last[ns[i]]) + upd[i] for i in heads]
            y = jnp.concatenate(
                [_rms_rows(o[i]) * gain_ref[...] * _silu(gate[u][:, cols[i]]) for i in heads], axis=1)
            if padded:
                y_ref[0] = y[:t].astype(y_ref.dtype)
            else:
                y_ref[0, pl.ds(pl.multiple_of((it * cpt + u) * c, c), c), :] = y.astype(y_ref.dtype)
        for i in heads:
            s_ref[0, i] = state[i]
        return carry

    lax.fori_loop(0, n_chunks // cpt, body, 0)


def gdn_group_tail(ba, batch, seq):
    hg = GDN_HEADS_PER_STEP
    groups = H_C // hg
    br = ba[:, :H_C].reshape(batch, seq, groups, hg)
    ar = ba[:, H_C:2 * H_C].reshape(batch, seq, groups, hg)
    cat = jnp.concatenate([br, ar], axis=-1).transpose(0, 2, 1, 3)
    return jnp.pad(cat, ((0, 0), (0, 0), (0, 0), (0, LANE - 2 * hg)))


def gdn_group_vec(v):
    hg = GDN_HEADS_PER_STEP
    return jnp.pad(v.reshape(H_C // hg, 1, hg), ((0, 0), (0, 0), (hg, LANE - 2 * hg)))


def gdn_mixer(proj3, bag, conv_w, alog_vec, dt_vec, out_gain, conv_buf, state0, out_dtype):
    b, t, _ = proj3.shape
    hg = GDN_HEADS_PER_STEP
    wid = hg * DH
    has_state = state0 is not None
    col = lambda off: (lambda i, j: (i, 0, off // wid + j))
    wcol = lambda off: (lambda i, j: (0, off // wid + j))
    in_specs = [pl.BlockSpec((1, t, wid), col(OFF_C)),
                pl.BlockSpec((1, t, wid), col(OFF_C + D_C)),
                pl.BlockSpec((1, t, wid), col(OFF_C + 2 * D_C)),
                pl.BlockSpec((1, t, wid), col(OFF_G)),
                pl.BlockSpec((1, 1, t, LANE), lambda i, j: (i, j, 0, 0)),
                pl.BlockSpec((GDN_CONV, wid), wcol(0)),
                pl.BlockSpec((GDN_CONV, wid), wcol(D_C)),
                pl.BlockSpec((GDN_CONV, wid), wcol(2 * D_C)),
                pl.BlockSpec((1, 1, LANE), lambda i, j: (j, 0, 0)),
                pl.BlockSpec((1, 1, LANE), lambda i, j: (j, 0, 0)),
                pl.BlockSpec((1, DH), lambda i, j: (0, 0))]
    args = [proj3, proj3, proj3, proj3, bag, conv_w, conv_w, conv_w, alog_vec, dt_vec, out_gain]
    if has_state:
        in_specs += [pl.BlockSpec((1, GDN_CONV - 1, wid), col(0)),
                     pl.BlockSpec((1, GDN_CONV - 1, wid), col(D_C)),
                     pl.BlockSpec((1, GDN_CONV - 1, wid), col(2 * D_C)),
                     pl.BlockSpec((1, hg, DH, DH), lambda i, j: (i, j, 0, 0))]
        args += [conv_buf, conv_buf, conv_buf, state0]
    vmem = 2 * (4 * t * wid * 4 + t * LANE * 4 + t * wid * 4 + 2 * hg * DH * DH * 4) + (4 << 20)
    return pl.pallas_call(
        functools.partial(_gdn_kernel, t=t, has_state=has_state),
        grid=(b, H_C // hg),
        in_specs=in_specs,
        out_specs=[pl.BlockSpec((1, t, wid), lambda i, j: (i, 0, j)),
                   pl.BlockSpec((1, hg, DH, DH), lambda i, j: (i, j, 0, 0))],
        out_shape=[jax.ShapeDtypeStruct((b, t, D_C), out_dtype),
                   jax.ShapeDtypeStruct((b, H_C, DH, DH), F32)],
        compiler_params=_cparams(("parallel", "parallel"), vmem),
        name="gdn_mixer",
    )(*args)


def _mix_dot(ya_ref, yb_ref, yc_ref, wbf_ref):
    acc = _dot(ya_ref[...], wbf_ref[0:D_A, :])
    acc = acc + _dot(yb_ref[...], wbf_ref[D_A:D_A + D_B, :])
    return acc + _dot(yc_ref[...], wbf_ref[D_A + D_B:, :])


def _proj_out_kernel(yap_ref, ybp_ref, ycp_ref, yas_ref, ybs_ref, ycs_ref, w_ref, xp_ref, xs_ref,
                     op_ref, os_ref, wbf_ref):
    @pl.when(pl.program_id(1) == 0)
    def _():
        wbf_ref[...] = w_ref[...].astype(BF16)
        os_ref[...] = xs_ref[...] + _mix_dot(yas_ref, ybs_ref, ycs_ref, wbf_ref)

    op_ref[...] = xp_ref[...] + _mix_dot(yap_ref, ybp_ref, ycp_ref, wbf_ref)


def proj_out(mix_p, mix_s, w_all, layer, x_p, x_s):
    mp, n = x_p.shape
    ms = x_s.shape[0]
    tm, tn = MM_TM, MM_TN
    kk = D_A + D_B + D_C
    vmem = 2 * ((tm + ms) * kk * 2 + kk * tn * 4 + 2 * (tm + ms) * tn * 4) + 3 * kk * tn * 2
    rowp = lambda j, i: (i, 0)
    rows = lambda j, i: (0, 0)
    widths = (D_A, D_B, D_C)
    return pl.pallas_call(
        _proj_out_kernel,
        grid=(n // tn, mp // tm),
        in_specs=([pl.BlockSpec((tm, w), rowp) for w in widths]
                  + [pl.BlockSpec((ms, w), rows) for w in widths]
                  + [pl.BlockSpec((None, kk, tn), lambda j, i: (layer, 0, j)),
                     pl.BlockSpec((tm, tn), lambda j, i: (i, j)),
                     pl.BlockSpec((ms, tn), lambda j, i: (0, j))]),
        out_specs=[pl.BlockSpec((tm, tn), lambda j, i: (i, j)),
                   pl.BlockSpec((ms, tn), lambda j, i: (0, j))],
        out_shape=[jax.ShapeDtypeStruct((mp, n), F32), jax.ShapeDtypeStruct((ms, n), F32)],
        scratch_shapes=[pltpu.VMEM((kk, tn), BF16)],
        compiler_params=_cparams(("arbitrary", "arbitrary"), vmem),
        name="proj_out",
    )(*mix_p, *mix_s, w_all, x_p, x_s)


FFN_TN = 256


def _ffn_conv(cur, ext_ref, cw_ref, b_ref, rows):
    out = cur * cw_ref[FFN_CONV - 1:FFN_CONV, :] + b_ref[...]
    for i in range(1, FFN_CONV):
        out = out + ext_ref[SUBLANE - i:SUBLANE - i + rows, :] * cw_ref[FFN_CONV - 1 - i:FFN_CONV - i, :]
    return out


def _shift_rows(u, halo, i):
    n, w = u.shape
    rot = pltpu.roll(u.reshape(n // SUBLANE, SUBLANE, w), i, axis=1)
    above = jnp.concatenate([pltpu.roll(halo, i, axis=0)[None], rot[:-1]], axis=0)
    sub = lax.broadcasted_iota(jnp.int32, (1, SUBLANE, 1), 1)
    return jnp.where(sub < i, above, rot).reshape(n, w)


def _ffn_conv_rows(u, halo, cw_ref, b_ref):
    out = u * cw_ref[FFN_CONV - 1:FFN_CONV, :] + b_ref[...]
    for i in range(1, FFN_CONV):
        out = out + _shift_rows(u, halo, i) * cw_ref[FFN_CONV - 1 - i:FFN_CONV - i, :]
    return out


FFN_ROW_SPLIT = 2


def _ffn_up_kernel(xp_ref, xs_ref, wg_ref, wv_ref, cwg_ref, cwv_ref, bg_ref, bv_ref, sg_ref, sv_ref,
                   actp_ref, tg_ref, tv_ref, acts_ref, ugs_ref, uvs_ref,
                   wgbf_ref, wvbf_ref, halo_ref, ext_ref, *, tiles_per_seq, t):
    tm = xp_ref.shape[0]
    rows = tm // FFN_ROW_SPLIT
    mi = pl.program_id(1)

    @pl.when(mi == 0)
    def _():
        wgbf_ref[...] = wg_ref[...].astype(BF16)
        wvbf_ref[...] = wv_ref[...].astype(BF16)
        assert t == SUBLANE
        xs = xs_ref[...]
        nb = xs.shape[0] // t
        us = _dot(xs, wgbf_ref[...])
        vs = _dot(xs, wvbf_ref[...])
        ugs_ref[...] = us
        uvs_ref[...] = vs

        def conv(u, st_ref, cw_ref, b_ref):
            outs = []
            for bi in range(nb):
                cur = u[bi * t:(bi + 1) * t]
                ext_ref[SUBLANE - (FFN_CONV - 1):SUBLANE, :] = st_ref[bi]
                ext_ref[SUBLANE:, :] = cur
                outs.append(_ffn_conv(cur, ext_ref, cw_ref, b_ref, t))
            return jnp.concatenate(outs, axis=0)

        gts = conv(us, sg_ref, cwg_ref, bg_ref)
        vls = conv(vs, sv_ref, cwv_ref, bv_ref)
        acts_ref[...] = (_silu(gts) * vls).astype(acts_ref.dtype)

    @pl.when(mi % tiles_per_seq == 0)
    def _():
        halo_ref[...] = jnp.zeros(halo_ref.shape, F32)

    halo_g = halo_ref[0]
    halo_v = halo_ref[1]
    ug = []
    uv = []
    for s in range(FFN_ROW_SPLIT):
        x = xp_ref[s * rows:(s + 1) * rows, :]
        ug.append(_dot(x, wgbf_ref[...]))
        uv.append(_dot(x, wvbf_ref[...]))
    for s in range(FFN_ROW_SPLIT):
        gt = _ffn_conv_rows(ug[s], halo_g, cwg_ref, bg_ref)
        vl = _ffn_conv_rows(uv[s], halo_v, cwv_ref, bv_ref)
        actp_ref[s * rows:(s + 1) * rows, :] = (_silu(gt) * vl).astype(actp_ref.dtype)
        halo_g = ug[s][rows - SUBLANE:]
        halo_v = uv[s][rows - SUBLANE:]
    halo_ref[0] = halo_g
    halo_ref[1] = halo_v
    tg_ref[0] = halo_g
    tv_ref[0] = halo_v


def ffn_up(xn_p, xn_s, w_up, conv_w, conv_b, conv_state, layer, seq, t):
    mp, k = xn_p.shape
    ms = xn_s.shape[0]
    d_ff = w_up.shape[2] // 2
    tn = FFN_TN
    nt = d_ff // tn
    tm = min(seq, MM_TM)
    mt = mp // tm
    nb = ms // t
    lo = lambda j, i: (layer, 0, j)
    hi = lambda j, i: (layer, 0, nt + j)
    vmem = (2 * (tm * k * 2 + ms * k * 2 + 2 * k * tn * 4 + tm * tn * 2) + 2 * k * tn * 2
            + 8 * tm * tn * 4)
    return pl.pallas_call(
        functools.partial(_ffn_up_kernel, tiles_per_seq=seq // tm, t=t),
        grid=(nt, mt),
        in_specs=[pl.BlockSpec((tm, k), lambda j, i: (i, 0)),
                  pl.BlockSpec((ms, k), lambda j, i: (0, 0)),
                  pl.BlockSpec((None, k, tn), lo), pl.BlockSpec((None, k, tn), hi),
                  pl.BlockSpec((None, FFN_CONV, tn), lo), pl.BlockSpec((None, FFN_CONV, tn), hi),
                  pl.BlockSpec((None, 1, tn), lo), pl.BlockSpec((None, 1, tn), hi),
                  pl.BlockSpec((None, nb, FFN_CONV - 1, tn), lambda j, i: (layer, 0, 0, j)),
                  pl.BlockSpec((None, nb, FFN_CONV - 1, tn), lambda j, i: (layer, 0, 0, nt + j))],
        out_specs=[pl.BlockSpec((tm, tn), lambda j, i: (i, j)),
                   pl.BlockSpec((1, SUBLANE, tn), lambda j, i: (i, 0, j)),
                   pl.BlockSpec((1, SUBLANE, tn), lambda j, i: (i, 0, j)),
                   pl.BlockSpec((ms, tn), lambda j, i: (0, j)),
                   pl.BlockSpec((ms, tn), lambda j, i: (0, j)),
                   pl.BlockSpec((ms, tn), lambda j, i: (0, j))],
        out_shape=[jax.ShapeDtypeStruct((mp, d_ff), BF16),
                   jax.ShapeDtypeStruct((mt, SUBLANE, d_ff), F32),
                   jax.ShapeDtypeStruct((mt, SUBLANE, d_ff), F32),
                   jax.ShapeDtypeStruct((ms, d_ff), BF16),
                   jax.ShapeDtypeStruct((ms, d_ff), F32),
                   jax.ShapeDtypeStruct((ms, d_ff), F32)],
        scratch_shapes=[pltpu.VMEM((k, tn), BF16), pltpu.VMEM((k, tn), BF16),
                        pltpu.VMEM((2, SUBLANE, tn), F32), pltpu.VMEM((SUBLANE + t, tn), F32)],
        compiler_params=_cparams(("arbitrary", "arbitrary"), vmem),
        name="ffn_up",
    )(xn_p, xn_s, w_up, w_up, conv_w, conv_w, conv_b, conv_b, conv_state, conv_state)


def _ffn_down_kernel(a_ref, w_ref, h_ref, o_ref):
    o_ref[...] = h_ref[...] + _dot(a_ref[...], w_ref[...])


def ffn_down_proj(act, w_down, layer, h):
    m, k = act.shape
    n = w_down.shape[2]
    tm = min(m, 512)
    tn = 512
    vmem = 2 * (tm * k * 2 + k * tn * 2 + 2 * tm * tn * 4)
    return pl.pallas_call(
        _ffn_down_kernel,
        grid=(m // tm, n // tn),
        in_specs=[pl.BlockSpec((tm, k), lambda i, j: (i, 0)),
                  pl.BlockSpec((None, k, tn), lambda i, j: (layer, 0, j)),
                  pl.BlockSpec((tm, tn), lambda i, j: (i, j))],
        out_specs=pl.BlockSpec((tm, tn), lambda i, j: (i, j)),
        out_shape=jax.ShapeDtypeStruct((m, n), F32),
        compiler_params=_cparams(("parallel", "parallel"), vmem),
        name="ffn_down",
    )(act, w_down, h)


def _t5_bucket(dist):
    max_exact = REL_BUCKETS // 2
    d = jnp.maximum(dist, 1).astype(F32)
    large = max_exact + (jnp.log(d / max_exact) / math.log(REL_MAX_DIST / max_exact)
                         * (REL_BUCKETS - max_exact)).astype(jnp.int32)
    large = jnp.minimum(large, REL_BUCKETS - 1)
    return jnp.where(dist < max_exact, dist, large)


def _bias_lookup(rel_bias, buckets):
    onehot = jax.nn.one_hot(buckets, REL_BUCKETS, dtype=F32)
    return jnp.einsum('...k,kh->...h', onehot, rel_bias.astype(F32), precision=HIGHEST)


def _prompt_bias_blocks(rel_bias):
    qi = np.arange(BLK)[:, None]
    ki = np.arange(2 * BLK)[None, :]
    rel = qi + BLK - ki
    out = []
    for window, dil in A_BRANCHES:
        nj = window // dil + 1
        valid = (rel >= 0) & (rel < nj)
        buckets = _t5_bucket(jnp.asarray(np.clip(rel, 0, nj - 1) * dil, jnp.int32))
        bias = _bias_lookup(rel_bias, buckets)
        out.append(jnp.where(jnp.asarray(valid)[..., None], bias, NEG_INF).transpose(2, 0, 1))
    return jnp.stack(out)


def _branch_count(dist):
    cnt = np.zeros(dist.shape, np.float32)
    for window, dil in A_BRANCHES:
        cnt += ((dist >= 0) & (dist % dil == 0) & (dist // dil <= window // dil)).astype(np.float32)
    return cnt


def _sample_bias_tables(rel_bias, t, n_past):
    tq = np.arange(t)[:, None]
    d_cache = n_past + tq - np.arange(n_past)[None, :]
    d_new = tq - np.arange(BLK)[None, :]
    d_new = np.where(np.arange(BLK)[None, :] < t, d_new, -1)
    tables = []
    for dist in (d_cache, d_new):
        cnt = _branch_count(dist)
        bias = _bias_lookup(rel_bias, _t5_bucket(jnp.asarray(np.maximum(dist, 0), jnp.int32)))
        bias = jnp.where(jnp.asarray(cnt > 0)[..., None], bias, NEG_INF).transpose(2, 0, 1)
        tables += [bias, jnp.asarray(cnt)]
    return tables


def _mixers(proj, ba, lw, *, batch, seq, layer, cache_hk=None, pool_buf=None, gconv_buf=None,
            gstate=None, bias_tabs=None):
    sample = cache_hk is not None
    mix_dtype = F32 if sample else BF16
    proj3 = proj.reshape(batch, seq, N_MAIN)
    bag = gdn_group_tail(ba, batch, seq)
    qg = lw['a_q_norm'].reshape(1, DH)
    kg = lw['a_k_norm'].reshape(1, DH)
    og = lw['a_out_norm'].reshape(H_A, DH)
    if sample:
        ya, kv = attn_sample(proj3, cache_hk, layer, qg, kg, og, *bias_tabs)
    else:
        ya, kv = attn_prompt(proj3, qg, kg, og, bias_tabs)
    kv_rows = kv.transpose(0, 1, 3, 2, 4)
    yb = pool_mixer(proj3, pool_buf, lw['pool_w'], lw['pool_scale'].reshape(1, D_B), mix_dtype)
    pu = proj3[:, :, OFF_P:OFF_P + D_B]
    if sample:
        pool_new = jnp.concatenate([pool_buf, pu], axis=1)[:, -POOL_BUF:]
    else:
        pool_new = pu[:, -POOL_BUF:]
    yc, gstate_new = gdn_mixer(proj3, bag, lw['gdn_conv_w'], lw['alog_vec'], lw['dt_vec'],
                               lw['gdn_out_norm'].reshape(1, DH), gconv_buf, gstate, mix_dtype)
    gconv_new = proj3[:, -(GDN_CONV - 1):, OFF_C:OFF_C + 3 * D_C]
    m = batch * seq
    mix = (ya.reshape(m, D_A).astype(BF16), yb.reshape(m, D_B).astype(BF16),
           yc.reshape(m, D_C).astype(BF16))
    return mix, (kv_rows, pool_new, gconv_new, gstate_new)


def _layer(xp, xs, lw, gw, *, layer, prompt_shape, sample_shape, cache_hk, pool_buf, gconv_buf, gstate,
           fconv_state, prompt_bias, sample_tabs):
    bp, sp = prompt_shape
    bs, ss = sample_shape
    xnp = rmsnorm_cast(xp, lw['norm_mix'])
    xns = rmsnorm_cast(xs, lw['norm_mix'])
    proj_p, proj_s = proj_in(xnp, xns, gw['w_in'], layer)
    ba_p = proj_tail(xnp, gw['w_in_tail'], layer)
    ba_s = proj_tail(xns, gw['w_in_tail'], layer)
    mix_p, outs_p = _mixers(proj_p, ba_p, lw, batch=bp, seq=sp, layer=layer, bias_tabs=prompt_bias)
    mix_s, outs_s = _mixers(proj_s, ba_s, lw, batch=bs, seq=ss, layer=layer, cache_hk=cache_hk,
                            pool_buf=pool_buf, gconv_buf=gconv_buf, gstate=gstate, bias_tabs=sample_tabs)
    hp, hs = proj_out(mix_p, mix_s, gw['w_out'], layer, xp, xs)
    hnp = rmsnorm_cast(hp, lw['norm_ffn'])
    hns = rmsnorm_cast(hs, lw['norm_ffn'])
    act_p, tg, tv, act_s, ug, uv = ffn_up(hnp, hns, gw['ffn_up'], gw['ffn_conv_w'], gw['ffn_conv_b'],
                                          fconv_state, layer, sp, ss)
    tails = jnp.concatenate([tg, tv], axis=-1)
    tiles_per_seq = tails.shape[0] // bp
    fconv_p = tails[tiles_per_seq - 1::tiles_per_seq, -(FFN_CONV - 1):]
    up_s = jnp.concatenate([ug, uv], axis=-1).reshape(bs, ss, -1)
    fconv_s = up_s[:, -(FFN_CONV - 1):]
    yp = ffn_down_proj(act_p, gw['ffn_down'], layer, hp)
    ys = ffn_down_proj(act_s, gw['ffn_down'], layer, hs)
    return yp, ys, outs_p + (fconv_p,), outs_s + (fconv_s,)


def kernel(x_prompt, x_sample, cache_attn_kv, state_pool, state_gdn_conv, state_gdn, state_ffn_conv,
           rel_bias, norm_mix, w_in, a_q_norm, a_k_norm, a_out_norm, pool_w, pool_scale,
           gdn_conv_w, gdn_a_log, gdn_dt_bias, gdn_out_norm, w_out, norm_ffn,
           ffn_up, ffn_conv_w, ffn_conv_b, ffn_down):
    depth = w_in.shape[0]
    bp, sp, d_model = x_prompt.shape
    bs, ss, _ = x_sample.shape
    n_past = cache_attn_kv.shape[2]
    assert w_in.shape[2] == N_MAIN + N_TAIL and w_out.shape[1] == D_A + D_B + D_C

    prompt_bias = _prompt_bias_blocks(rel_bias)
    sample_tabs = _sample_bias_tables(rel_bias, ss, n_past)
    cache_hk = jnp.swapaxes(cache_attn_kv, 3, 4)

    xp = x_prompt.reshape(bp * sp, d_model)
    xs = x_sample.reshape(bs * ss, d_model)
    p_out = [[] for _ in range(5)]
    s_out = [[] for _ in range(5)]
    gw = {
        'w_in': w_in.astype(BF16),
        'w_in_tail': jnp.pad(w_in[:, :, N_MAIN:], ((0, 0), (0, 0), (0, LANE - N_TAIL))).astype(BF16),
        'w_out': w_out, 'ffn_up': ffn_up, 'ffn_down': ffn_down.astype(BF16),
        'ffn_conv_w': ffn_conv_w, 'ffn_conv_b': ffn_conv_b.reshape(depth, 1, -1),
    }
    for l in range(depth):
        lw = {
            'norm_mix': norm_mix[l], 'norm_ffn': norm_ffn[l],
            'a_q_norm': a_q_norm[l], 'a_k_norm': a_k_norm[l], 'a_out_norm': a_out_norm[l],
            'pool_w': pool_w[l].astype(BF16), 'pool_scale': pool_scale[l],
            'gdn_conv_w': gdn_conv_w[l],
            'alog_vec': gdn_group_vec(gdn_a_log[l]), 'dt_vec': gdn_group_vec(gdn_dt_bias[l]),
            'gdn_out_norm': gdn_out_norm[l],
        }
        xp, xs, outs_p, outs_s = _layer(
            xp, xs, lw, gw, layer=l, prompt_shape=(bp, sp), sample_shape=(bs, ss), cache_hk=cache_hk,
            pool_buf=state_pool[l], gconv_buf=state_gdn_conv[l], gstate=state_gdn[l],
            fconv_state=state_ffn_conv, prompt_bias=prompt_bias, sample_tabs=sample_tabs)
        for acc, o in zip(p_out, outs_p):
            acc.append(o)
        for acc, o in zip(s_out, outs_s):
            acc.append(o)
    res = [xp.reshape(bp, sp, d_model), xs.reshape(bs, ss, d_model)]
    for po, so in zip(p_out, s_out):
        res += [jnp.stack(po), jnp.stack(so)]
    return tuple(res)
```

```python
import functools
import math

import numpy as np
import jax
import jax.numpy as jnp
from jax import lax
from jax.experimental import pallas as pl
from jax.experimental.pallas import tpu as pltpu

F32 = jnp.float32
BF16 = jnp.bfloat16
HIGHEST = lax.Precision.HIGHEST

DH = 128
H_A = 12
H_C = 12
A_BRANCHES = ((128, 1), (512, 4), (2048, 16))
BLK = 128
REL_BUCKETS = 32
REL_MAX_DIST = 2048
POOL_WINDOWS = (2, 4, 8, 16)
CG = 256
POOL_BUF = 15
GDN_CONV = 4
GDN_CHUNK = 64
GDN_SUB = 16
FFN_CONV = 3
EPS = 1e-6
NEG_INF = -1e30

D_A = H_A * DH
D_B = len(POOL_WINDOWS) * CG
D_C = H_C * DH
OFF_Q, OFF_K, OFF_V = 0, D_A, 2 * D_A
OFF_P = 3 * D_A
OFF_C = OFF_P + D_B
OFF_G = OFF_C + 3 * D_C
N_MAIN = OFF_G + D_C
N_TAIL = 2 * H_C

LANE = 128
SUBLANE = 8
VMEM_CAP = 56 * 1024 * 1024


def _cparams(sem, vmem_bytes):
    limit = int(min(max(vmem_bytes * 5 // 4 + (2 << 20), 16 << 20), VMEM_CAP))
    return pltpu.CompilerParams(dimension_semantics=sem, vmem_limit_bytes=limit)


def _rms_rows(x):
    return x * lax.rsqrt(jnp.mean(x * x, axis=-1, keepdims=True) + EPS)


def _silu(x):
    return x * jax.nn.sigmoid(x)


def _dot(a, b):
    return jnp.dot(a, b, preferred_element_type=F32)


def _dot_nt(a, b):
    return lax.dot_general(a, b, (((1,), (1,)), ((), ())), preferred_element_type=F32)


def _hdot(a, b):
    return jnp.dot(a, b, preferred_element_type=F32, precision=HIGHEST)


def _hdot_nt(a, b):
    return lax.dot_general(a, b, (((1,), (1,)), ((), ())), preferred_element_type=F32,
                           precision=HIGHEST)


def _hdot_tn(a, b):
    return lax.dot_general(a, b, (((0,), (0,)), ((), ())), preferred_element_type=F32,
                           precision=HIGHEST)


def _rmsnorm_kernel(x_ref, g_ref, o_ref):
    o_ref[...] = (_rms_rows(x_ref[...]) * g_ref[...]).astype(o_ref.dtype)


def rmsnorm_cast(x, gain):
    m, d = x.shape
    tm = min(m, 256)
    return pl.pallas_call(
        _rmsnorm_kernel,
        grid=(m // tm,),
        in_specs=[pl.BlockSpec((tm, d), lambda i: (i, 0)),
                  pl.BlockSpec((1, d), lambda i: (0, 0))],
        out_specs=pl.BlockSpec((tm, d), lambda i: (i, 0)),
        out_shape=jax.ShapeDtypeStruct((m, d), BF16),
        compiler_params=_cparams(("parallel",), 2 * tm * d * 6),
        name="rmsnorm_cast",
    )(x, gain.reshape(1, d))


MM_TM = 1024
MM_TN = 512


def _proj_in_kernel(xp_ref, xs_ref, w_ref, op_ref, os_ref):
    @pl.when(pl.program_id(1) == 0)
    def _():
        os_ref[...] = _dot(xs_ref[...], w_ref[...])

    op_ref[...] = _dot(xp_ref[...], w_ref[...])


def proj_in(xn_p, xn_s, w_all, layer):
    mp, k = xn_p.shape
    ms = xn_s.shape[0]
    tm, tn = MM_TM, MM_TN
    vmem = 2 * (tm * k * 2 + ms * k * 2 + k * tn * 2 + tm * tn * 4 + ms * tn * 4) + k * tn * 2
    return pl.pallas_call(
        _proj_in_kernel,
        grid=(N_MAIN // tn, mp // tm),
        in_specs=[pl.BlockSpec((tm, k), lambda j, i: (i, 0)),
                  pl.BlockSpec((ms, k), lambda j, i: (0, 0)),
                  pl.BlockSpec((None, k, tn), lambda j, i: (layer, 0, j))],
        out_specs=[pl.BlockSpec((tm, tn), lambda j, i: (i, j)),
                   pl.BlockSpec((ms, tn), lambda j, i: (0, j))],
        out_shape=[jax.ShapeDtypeStruct((mp, N_MAIN), F32),
                   jax.ShapeDtypeStruct((ms, N_MAIN), F32)],
        compiler_params=_cparams(("arbitrary", "arbitrary"), vmem),
        name="proj_in",
    )(xn_p, xn_s, w_all)


def _proj_tail_kernel(x_ref, w_ref, o_ref):
    o_ref[...] = _dot(x_ref[...], w_ref[...])


def proj_tail(xn, w_tail, layer):
    m, k = xn.shape
    tm = min(m, MM_TM)
    vmem = 2 * (tm * k * 2 + k * LANE * 2 + tm * LANE * 4)
    return pl.pallas_call(
        _proj_tail_kernel,
        grid=(m // tm,),
        in_specs=[pl.BlockSpec((tm, k), lambda i: (i, 0)),
                  pl.BlockSpec((None, k, LANE), lambda i: (layer, 0, 0))],
        out_specs=pl.BlockSpec((tm, LANE), lambda i: (i, 0)),
        out_shape=jax.ShapeDtypeStruct((m, LANE), F32),
        compiler_params=_cparams(("parallel",), vmem),
        name="proj_tail",
    )(xn, w_tail)


ATTN_BLOCKS_PER_TRIP = 4


def _strided_rows(start, dil):
    return pl.ds(start, BLK) if dil == 1 else pl.ds(start, BLK, stride=dil)


def _attn_prompt_kernel(q_ref, k_ref, v_ref, qg_ref, kg_ref, og_ref, bias_ref,
                        y_ref, kv_ref, qs_ref, ks_ref, acc_ref, m_ref, l_ref):
    seq = q_ref.shape[1]
    qs_ref[...] = _rms_rows(q_ref[0]) * qg_ref[...] * (DH ** -0.5)
    ks_ref[...] = _rms_rows(k_ref[0]) * kg_ref[...]
    kv_ref[0, :, 0, 0, :] = ks_ref[...]
    kv_ref[0, :, 0, 1, :] = v_ref[0]
    acc_ref[...] = jnp.zeros(acc_ref.shape, F32)
    l_ref[...] = jnp.zeros(l_ref.shape, F32)
    m_ref[...] = jnp.full(m_ref.shape, NEG_INF, F32)

    for bi, (window, dil) in enumerate(A_BRANCHES):
        assert window // dil == BLK and seq % (dil * BLK) == 0
        nb = seq // (dil * BLK)

        assert (dil * nb) % ATTN_BLOCKS_PER_TRIP == 0

        def body(it, carry, bi=bi, dil=dil, nb=nb):
            us = range(ATTN_BLOCKS_PER_TRIP)
            item = [it * ATTN_BLOCKS_PER_TRIP + u for u in us]
            r = [x // nb for x in item]
            n = [item[u] - r[u] * nb for u in us]
            rows = [_strided_rows(n[u] * (BLK * dil) + r[u], dil) for u in us]
            qb = [qs_ref[rw, :].astype(BF16) for rw in rows]
            kc = [ks_ref[rw, :] for rw in rows]
            vc = [v_ref[0, rw, :] for rw in rows]
            bias_cur = bias_ref[bi, 0, :, BLK:]
            if nb > 1:
                prows = [_strided_rows(jnp.maximum(n[u] - 1, 0) * (BLK * dil) + r[u], dil) for u in us]
                kk = [jnp.concatenate([ks_ref[prows[u], :], kc[u]], axis=0).astype(BF16) for u in us]
                vv = [jnp.concatenate([v_ref[0, prows[u], :], vc[u]], axis=0).astype(BF16) for u in us]
                bias_prev = bias_ref[bi, 0, :, :BLK]
                bias = [jnp.concatenate([jnp.where(n[u] > 0, bias_prev, NEG_INF), bias_cur], axis=1)
                        for u in us]
            else:
                kk = [x.astype(BF16) for x in kc]
                vv = [x.astype(BF16) for x in vc]
                bias = [bias_cur for _ in us]
            s = [_dot_nt(qb[u], kk[u]) + bias[u] for u in us]
            m_old = [m_ref[rw, :] for rw in rows]
            m_new = [jnp.maximum(m_old[u], jnp.max(s[u], axis=1, keepdims=True)) for u in us]
            p = [jnp.exp(s[u] - m_new[u][:, :1]) for u in us]
            alpha = [jnp.exp(m_old[u] - m_new[u]) for u in us]
            pv = [_dot(p[u].astype(BF16), vv[u]) for u in us]
            for u in us:
                l_ref[rows[u], :] = alpha[u] * l_ref[rows[u], :] + jnp.sum(p[u], axis=1, keepdims=True)
                acc_ref[rows[u], :] = alpha[u] * acc_ref[rows[u], :] + pv[u]
                m_ref[rows[u], :] = m_new[u]
            return carry

        lax.fori_loop(0, dil * nb // ATTN_BLOCKS_PER_TRIP, body, 0)

    o = acc_ref[...] / l_ref[...]
    y_ref[0] = (_rms_rows(o) * og_ref[0]).astype(y_ref.dtype)


def attn_prompt(proj3, q_gain, k_gain, o_gain, bias_blocks):
    b, s, _ = proj3.shape
    col = lambda off: (lambda i, h: (i, 0, off // DH + h))
    vmem = 2 * (3 * s * DH * 4 + s * DH * 2 + s * DH * 4 + 3 * BLK * 2 * BLK * 4) + 5 * s * DH * 4
    return pl.pallas_call(
        _attn_prompt_kernel,
        grid=(b, H_A),
        in_specs=[pl.BlockSpec((1, s, DH), col(OFF_Q)),
                  pl.BlockSpec((1, s, DH), col(OFF_K)),
                  pl.BlockSpec((1, s, DH), col(OFF_V)),
                  pl.BlockSpec((1, DH), lambda i, h: (0, 0)),
                  pl.BlockSpec((1, DH), lambda i, h: (0, 0)),
                  pl.BlockSpec((1, 1, DH), lambda i, h: (h, 0, 0)),
                  pl.BlockSpec((len(A_BRANCHES), 1, BLK, 2 * BLK), lambda i, h: (0, h, 0, 0))],
        out_specs=[pl.BlockSpec((1, s, DH), lambda i, h: (i, 0, h)),
                   pl.BlockSpec((1, s, 1, 2, DH), lambda i, h: (i, 0, h, 0, 0))],
        out_shape=[jax.ShapeDtypeStruct((b, s, D_A), BF16),
                   jax.ShapeDtypeStruct((b, s, H_A, 2, DH), F32)],
        scratch_shapes=[pltpu.VMEM((s, DH), F32)] * 5,
        compiler_params=_cparams(("parallel", "parallel"), vmem),
        name="attn_prompt",
    )(proj3, proj3, proj3, q_gain, k_gain, o_gain.reshape(H_A, 1, DH), bias_blocks)


ATTN_SAMPLE_TK = 256


def _attn_sample_kernel(q_ref, k_ref, v_ref, c_ref, qg_ref, kg_ref, og_ref,
                        bc_ref, cc_ref, bn_ref, cn_ref, y_ref, kv_ref, qn_ref, m_ref, l_ref, acc_ref):
    t = q_ref.shape[1]
    tk = c_ref.shape[2] // (2 * H_A)
    kt = pl.program_id(1)
    heads = range(H_A)
    cols = [slice(h * DH, (h + 1) * DH) for h in heads]

    @pl.when(kt == 0)
    def _():
        pad = jnp.zeros((BLK - t, DH), F32)
        for h in heads:
            qn = _rms_rows(q_ref[0, :, cols[h]]) * qg_ref[...] * (DH ** -0.5)
            kn = _rms_rows(k_ref[0, :, cols[h]]) * kg_ref[...]
            vn = v_ref[0, :, cols[h]]
            qn_ref[:, cols[h]] = qn
            kv_ref[0, :, h, 0, :] = kn
            kv_ref[0, :, h, 1, :] = vn
            kpad = jnp.concatenate([kn, pad], axis=0).astype(BF16)
            vpad = jnp.concatenate([vn, pad], axis=0).astype(BF16)
            s = _dot_nt(qn.astype(BF16), kpad) + bn_ref[h]
            m = jnp.max(s, axis=1, keepdims=True)
            p = jnp.exp(s - m) * cn_ref[...]
            m_ref[h] = jnp.broadcast_to(m, (t, DH))
            l_ref[h] = jnp.broadcast_to(jnp.sum(p, axis=1, keepdims=True), (t, DH))
            acc_ref[h] = _dot(p.astype(BF16), vpad)

    q16 = [qn_ref[:, cols[h]].astype(BF16) for h in heads]
    kc = [c_ref[0, 0, pl.ds(2 * h, tk, stride=2 * H_A), :].astype(BF16) for h in heads]
    vc = [c_ref[0, 0, pl.ds(2 * h + 1, tk, stride=2 * H_A), :].astype(BF16) for h in heads]
    s = [_dot_nt(q16[h], kc[h]) + bc_ref[h] for h in heads]
    m_old = [m_ref[h] for h in heads]
    m_new = [jnp.maximum(m_old[h], jnp.max(s[h], axis=1, keepdims=True)) for h in heads]
    p = [jnp.exp(s[h] - m_new[h][:, :1]) * cc_ref[...] for h in heads]
    alpha = [jnp.exp(m_old[h] - m_new[h]) for h in heads]
    pv = [_dot(p[h].astype(BF16), vc[h]) for h in heads]
    for h in heads:
        l_ref[h] = alpha[h] * l_ref[h] + jnp.sum(p[h], axis=1, keepdims=True)
        acc_ref[h] = alpha[h] * acc_ref[h] + pv[h]
        m_ref[h] = m_new[h]

    @pl.when(kt == pl.num_programs(1) - 1)
    def _():
        for h in heads:
            o = acc_ref[h] / l_ref[h]
            y_ref[0, :, cols[h]] = _rms_rows(o) * og_ref[:, cols[h]]


def attn_sample(proj3, cache_rows, layer, q_gain, k_gain, o_gain, bias_c, cnt_c, bias_n, cnt_n):
    b, t, _ = proj3.shape
    n_past = cache_rows.shape[2] // (2 * H_A)
    tk = ATTN_SAMPLE_TK
    col = lambda off: (lambda i, j: (i, 0, off // D_A))
    vmem = (2 * (tk * 2 * H_A * DH * 4 + 3 * t * D_A * 4 + H_A * t * tk * 4 + t * tk * 4
                 + 2 * t * 2 * D_A * 4) + 4 * H_A * t * DH * 4 + (2 << 20))
    return pl.pallas_call(
        _attn_sample_kernel,
        grid=(b, n_past // tk),
        in_specs=[pl.BlockSpec((1, t, D_A), col(OFF_Q)),
                  pl.BlockSpec((1, t, D_A), col(OFF_K)),
                  pl.BlockSpec((1, t, D_A), col(OFF_V)),
                  pl.BlockSpec((1, 1, tk * 2 * H_A, DH), lambda i, j: (layer, i, j, 0)),
                  pl.BlockSpec((1, DH), lambda i, j: (0, 0)),
                  pl.BlockSpec((1, DH), lambda i, j: (0, 0)),
                  pl.BlockSpec((1, D_A), lambda i, j: (0, 0)),
                  pl.BlockSpec((H_A, t, tk), lambda i, j: (0, 0, j)),
                  pl.BlockSpec((t, tk), lambda i, j: (0, j)),
                  pl.BlockSpec((H_A, t, BLK), lambda i, j: (0, 0, 0)),
                  pl.BlockSpec((t, BLK), lambda i, j: (0, 0))],
        out_specs=[pl.BlockSpec((1, t, D_A), lambda i, j: (i, 0, 0)),
                   pl.BlockSpec((1, t, H_A, 2, DH), lambda i, j: (i, 0, 0, 0, 0))],
        out_shape=[jax.ShapeDtypeStruct((b, t, D_A), F32),
                   jax.ShapeDtypeStruct((b, t, H_A, 2, DH), F32)],
        scratch_shapes=[pltpu.VMEM((t, D_A), F32)] + [pltpu.VMEM((H_A, t, DH), F32)] * 3,
        compiler_params=_cparams(("parallel", "arbitrary"), vmem),
        name="attn_sample",
    )(proj3, proj3, proj3, cache_rows, q_gain, k_gain, o_gain.reshape(1, D_A),
      bias_c, cnt_c, bias_n, cnt_n)


POOL_HALO = 16


def _pool_kernel(*refs, t, rows, n_valid, has_buf):
    ng = len(POOL_WINDOWS)
    u_refs = refs[:ng]
    refs = refs[ng:]
    if has_buf:
        buf_refs = refs[:ng]
        refs = refs[ng:]
    w_ref, scale_ref, y_ref, ext_ref = refs
    for g, win in enumerate(POOL_WINDOWS):
        ext_ref[0:POOL_HALO, :] = jnp.zeros((POOL_HALO, CG), F32)
        if has_buf:
            ext_ref[POOL_HALO - POOL_BUF:POOL_HALO, :] = buf_refs[g][0]
        ext_ref[POOL_HALO:, :] = u_refs[g][0]

        def chunk(ci, carry, g=g, win=win):
            base = pl.multiple_of(ci * rows, SUBLANE)
            w = ext_ref[pl.ds(base, rows + POOL_HALO), :]
            u = w[POOL_HALO:]
            tot = u
            for i in range(1, win):
                tot = tot + w[POOL_HALO - i:POOL_HALO - i + rows]
            pos = base + lax.broadcasted_iota(jnp.int32, (rows, 1), 0)
            cnt = jnp.minimum(win, n_valid + pos + 1).astype(F32)
            d = tot / cnt - u
            y = _rms_rows(_dot(d.astype(BF16), w_ref[g])) * scale_ref[:, g * CG:(g + 1) * CG]
            y_ref[0, pl.ds(base, rows), g * CG:(g + 1) * CG] = y.astype(y_ref.dtype)
            return carry

        lax.fori_loop(0, t // rows, chunk, 0)


def pool_mixer(proj3, bufs, w_pool, scale, out_dtype):
    b, t, _ = proj3.shape
    ng = len(POOL_WINDOWS)
    rows = min(t, 256)
    has_buf = bufs is not None
    in_specs = [pl.BlockSpec((1, t, CG), (lambda i, g=g: (i, 0, OFF_P // CG + g))) for g in range(ng)]
    args = [proj3] * ng
    if has_buf:
        in_specs += [pl.BlockSpec((1, POOL_BUF, CG), (lambda i, g=g: (i, 0, g))) for g in range(ng)]
        args += [bufs] * ng
    in_specs += [pl.BlockSpec((ng, CG, CG), lambda i: (0, 0, 0)),
                 pl.BlockSpec((1, D_B), lambda i: (0, 0))]
    args += [w_pool, scale]
    vmem = 2 * (ng * t * CG * 4 + t * D_B * 4 + ng * CG * CG * 2) + (t + POOL_HALO) * CG * 4
    return pl.pallas_call(
        functools.partial(_pool_kernel, t=t, rows=rows, n_valid=POOL_BUF if has_buf else 0,
                          has_buf=has_buf),
        grid=(b,),
        in_specs=in_specs,
        out_specs=pl.BlockSpec((1, t, D_B), lambda i: (i, 0, 0)),
        out_shape=jax.ShapeDtypeStruct((b, t, D_B), out_dtype),
        scratch_shapes=[pltpu.VMEM((t + POOL_HALO, CG), F32)],
        compiler_params=_cparams(("parallel",), vmem),
        name="pool_mixer",
    )(*args)


GDN_HEADS_PER_STEP = 4
GDN_CHUNKS_PER_TRIP = 4


def _split_bf16(a):
    hi = a.astype(BF16)
    return hi, (a - hi.astype(F32)).astype(BF16)


def _dot3(a, b):
    return _dot(a[0], b[0]) + (_dot(a[0], b[1]) + _dot(a[1], b[0]))


def _cumsum_rows(x):
    n = x.shape[0]
    row = lax.broadcasted_iota(jnp.int32, (n, 1), 0)
    k = 1
    while k < n:
        x = x + jnp.where(row >= k, pltpu.roll(x, k, axis=0), 0.0)
        k *= 2
    return x


def _unit_lower_solve(mats, rhss, eye, sub_diag):
    assert GDN_CHUNK // GDN_SUB == 4 and GDN_SUB == 16
    idx = range(len(mats))
    d = [jnp.where(sub_diag, m, 0.0) for m in mats]
    low = [_split_bf16(m - di) for m, di in zip(mats, d)]
    rs = [_split_bf16(r) for r in rhss]
    x = [eye - di for di in d]
    ps = [_split_bf16(di) for di in d]
    ps = [_split_bf16(_dot3(ps[i], ps[i])) for i in idx]
    for _ in range(2):
        xs = [_split_bf16(xi) for xi in x]
        x = [x[i] + _dot3(xs[i], ps[i]) for i in idx]
        ps = [_split_bf16(_dot3(ps[i], ps[i])) for i in idx]
    xs = [_split_bf16(xi) for xi in x]
    x = [x[i] + _dot3(xs[i], ps[i]) for i in idx]
    xs = [_split_bf16(xi) for xi in x]
    n = [_dot3(xs[i], low[i]) for i in idx]
    xr = [_dot3(xs[i], rs[i]) for i in idx]
    ns = [_split_bf16(ni) for ni in n]
    n2 = [_split_bf16(_dot3(ns[i], ns[i])) for i in idx]
    imn = [eye - ni for ni in n]
    y = [imn[i] + _dot3(_split_bf16(imn[i]), n2[i]) for i in idx]
    return [_dot3(_split_bf16(y[i]), _split_bf16(xr[i])) for i in idx]


def _gdn_kernel(*refs, t, has_state):
    c = GDN_CHUNK
    hg = GDN_HEADS_PER_STEP
    (q_ref, k_ref, v_ref, gate_ref, ba_ref, cwq_ref, cwk_ref, cwv_ref,
     alog_ref, dt_ref, gain_ref) = refs[:11]
    refs = refs[11:]
    if has_state:
        cbq_ref, cbk_ref, cbv_ref, s0_ref = refs[:4]
        refs = refs[4:]
    y_ref, s_ref, cwq_rows, cwk_rows, cwv_rows = refs
    for rows_ref, cw_in in ((cwq_rows, cwq_ref), (cwk_rows, cwk_ref), (cwv_rows, cwv_ref)):
        for j in range(GDN_CONV):
            rows_ref[j] = jnp.broadcast_to(cw_in[j:j + 1, :], rows_ref.shape[1:])
    n_chunks = -(-t // c)
    padded = t % c != 0
    assert (not padded) or n_chunks == 1
    wid = hg * DH
    heads = range(hg)

    if has_state:
        s_ref[0] = s0_ref[0]
    else:
        s_ref[0] = jnp.zeros(s_ref.shape[1:], F32)

    ii = lax.broadcasted_iota(jnp.int32, (c, c), 0)
    jj = lax.broadcasted_iota(jnp.int32, (c, c), 1)
    tril = ii >= jj
    strict = ii > jj
    eye = (ii == jj).astype(F32)
    sub_diag = (ii // GDN_SUB) == (jj // GDN_SUB)
    row_id = lax.broadcasted_iota(jnp.int32, (c, 1), 0)

    def halo_rows(ref, cb_ref, ci):
        if n_chunks > 1:
            prev = ref[0, pl.ds(pl.multiple_of(jnp.maximum(ci * c - SUBLANE, 0), SUBLANE), SUBLANE), :]
        else:
            prev = jnp.zeros((SUBLANE, wid), F32)
        if has_state:
            pad = jnp.zeros((SUBLANE - (GDN_CONV - 1), wid), F32)
            first = jnp.concatenate([pad, cb_ref[0]], axis=0)
        else:
            first = jnp.zeros((SUBLANE, wid), F32)
        return jnp.where(ci > 0, prev, first)

    def conv_silu(ref, cb_ref, cw_ref, ci):
        if padded:
            cur = jnp.concatenate([ref[0], jnp.zeros((c - t, wid), F32)], axis=0)
        else:
            cur = ref[0, pl.ds(pl.multiple_of(ci * c, c), c), :]
        w = jnp.concatenate([halo_rows(ref, cb_ref, ci), cur], axis=0)
        out = cur * cw_ref[GDN_CONV - 1]
        for i in range(1, GDN_CONV):
            out = out + w[SUBLANE - i:SUBLANE - i + c] * cw_ref[GDN_CONV - 1 - i]
        return _silu(out)

    cpt = GDN_CHUNKS_PER_TRIP if n_chunks % GDN_CHUNKS_PER_TRIP == 0 else 1
    cols = [slice(i * DH, (i + 1) * DH) for i in heads]

    def body(it, carry):
        cb = (cbq_ref, cbk_ref, cbv_ref) if has_state else (None, None, None)
        q, k, v, gate, beta, gcum, g_last, decay = [], [], [], [], [], [], [], []
        for u in range(cpt):
            ci = it * cpt + u
            qa = conv_silu(q_ref, cb[0], cwq_rows, ci)
            ka = conv_silu(k_ref, cb[1], cwk_rows, ci)
            va = conv_silu(v_ref, cb[2], cwv_rows, ci)
            if padded:
                ba = jnp.concatenate([ba_ref[0, 0], jnp.zeros((c - t, LANE), F32)], axis=0)
                gate.append(jnp.concatenate([gate_ref[0], jnp.zeros((c - t, wid), F32)], axis=0))
                live = row_id < t
            else:
                ba = ba_ref[0, 0, pl.ds(pl.multiple_of(ci * c, c), c), :]
                gate.append(gate_ref[0, pl.ds(pl.multiple_of(ci * c, c), c), :])
            beta_all = jax.nn.sigmoid(ba)
            z = ba + dt_ref[0]
            softplus = jnp.maximum(z, 0.0) + jnp.log1p(jnp.exp(-jnp.abs(z)))
            g_all = -jnp.exp(alog_ref[0]) * softplus
            if padded:
                beta_all = jnp.where(live, beta_all, 0.0)
                g_all = jnp.where(live, g_all, 0.0)
            gcum_all = _cumsum_rows(g_all)
            gcum_t = gcum_all.T
            for i in heads:
                qi = qa[:, cols[i]]
                ki = ka[:, cols[i]]
                vi = va[:, cols[i]]
                qi = qi * lax.rsqrt(jnp.sum(qi * qi, axis=-1, keepdims=True) + EPS) * (DH ** -0.5)
                ki = ki * lax.rsqrt(jnp.sum(ki * ki, axis=-1, keepdims=True) + EPS)
                if padded:
                    qi = jnp.where(live, qi, 0.0)
                    ki = jnp.where(live, ki, 0.0)
                    vi = jnp.where(live, vi, 0.0)
                q.append(qi)
                k.append(ki)
                v.append(vi)
                beta.append(beta_all[:, i:i + 1])
                gc = gcum_all[:, hg + i:hg + i + 1]
                gcum.append(gc)
                g_last.append(gcum_all[c - 1:c, hg + i:hg + i + 1])
                decay.append(jnp.where(
                    tril, jnp.exp(jnp.where(tril, gc - gcum_t[hg + i:hg + i + 1, :], 0.0)), 0.0))
        chains = range(cpt * hg)
        e_cum = [jnp.exp(g) for g in gcum]
        kb = [k[n] * beta[n] for n in chains]
        k16 = [x.astype(BF16) for x in k]
        kk = [_dot_nt(kb[n].astype(BF16), k16[n]) for n in chains]
        qk = [_dot_nt(q[n].astype(BF16), k16[n]) for n in chains]
        m_mat = [jnp.where(strict, kk[n] * decay[n], 0.0) for n in chains]
        a_qk = [(qk[n] * decay[n]).astype(BF16) for n in chains]
        rhs = [jnp.concatenate([v[n] * beta[n], kb[n] * e_cum[n]], axis=1) for n in chains]
        sol = _unit_lower_solve(m_mat, rhs, eye, sub_diag)
        q_dec = [(q[n] * e_cum[n]).astype(BF16) for n in chains]
        k_dec_t = [(k[n] * jnp.exp(g_last[n] - gcum[n])).T.astype(BF16) for n in chains]
        state = [s_ref[0, i] for i in heads]
        for u in range(cpt):
            ns = [u * hg + i for i in heads]
            s16 = [x.astype(BF16) for x in state]
            w_s = [_dot(sol[ns[i]][:, DH:].astype(BF16), s16[i]) for i in heads]
            q_s = [_dot(q_dec[ns[i]], s16[i]) for i in heads]
            vn16 = [(sol[ns[i]][:, :DH] - w_s[i]).astype(BF16) for i in heads]
            o = [q_s[i] + _dot(a_qk[ns[i]], vn16[i]) for i in heads]
            upd = [_dot(k_dec_t[ns[i]], vn16[i]) for i in heads]
            state = [state[i] * jnp.exp(g_last[ns[i]]) + upd[i] for i in heads]
            y = jnp.concatenate(
                [_rms_rows(o[i]) * gain_ref[...] * _silu(gate[u][:, cols[i]]) for i in heads], axis=1)
            if padded:
                y_ref[0] = y[:t].astype(y_ref.dtype)
            else:
                y_ref[0, pl.ds(pl.multiple_of((it * cpt + u) * c, c), c), :] = y.astype(y_ref.dtype)
        for i in heads:
            s_ref[0, i] = state[i]
        return carry

    lax.fori_loop(0, n_chunks // cpt, body, 0)


def gdn_group_tail(ba, batch, seq):
    hg = GDN_HEADS_PER_STEP
    groups = H_C // hg
    br = ba[:, :H_C].reshape(batch, seq, groups, hg)
    ar = ba[:, H_C:2 * H_C].reshape(batch, seq, groups, hg)
    cat = jnp.concatenate([br, ar], axis=-1).transpose(0, 2, 1, 3)
    return jnp.pad(cat, ((0, 0), (0, 0), (0, 0), (0, LANE - 2 * hg)))


def gdn_group_vec(v):
    hg = GDN_HEADS_PER_STEP
    return jnp.pad(v.reshape(H_C // hg, 1, hg), ((0, 0), (0, 0), (hg, LANE - 2 * hg)))


def gdn_mixer(proj3, bag, conv_w, alog_vec, dt_vec, out_gain, conv_buf, state0, out_dtype):
    b, t, _ = proj3.shape
    hg = GDN_HEADS_PER_STEP
    wid = hg * DH
    has_state = state0 is not None
    col = lambda off: (lambda i, j: (i, 0, off // wid + j))
    wcol = lambda off: (lambda i, j: (0, off // wid + j))
    in_specs = [pl.BlockSpec((1, t, wid), col(OFF_C)),
                pl.BlockSpec((1, t, wid), col(OFF_C + D_C)),
                pl.BlockSpec((1, t, wid), col(OFF_C + 2 * D_C)),
                pl.BlockSpec((1, t, wid), col(OFF_G)),
                pl.BlockSpec((1, 1, t, LANE), lambda i, j: (i, j, 0, 0)),
                pl.BlockSpec((GDN_CONV, wid), wcol(0)),
                pl.BlockSpec((GDN_CONV, wid), wcol(D_C)),
                pl.BlockSpec((GDN_CONV, wid), wcol(2 * D_C)),
                pl.BlockSpec((1, 1, LANE), lambda i, j: (j, 0, 0)),
                pl.BlockSpec((1, 1, LANE), lambda i, j: (j, 0, 0)),
                pl.BlockSpec((1, DH), lambda i, j: (0, 0))]
    args = [proj3, proj3, proj3, proj3, bag, conv_w, conv_w, conv_w, alog_vec, dt_vec, out_gain]
    if has_state:
        in_specs += [pl.BlockSpec((1, GDN_CONV - 1, wid), col(0)),
                     pl.BlockSpec((1, GDN_CONV - 1, wid), col(D_C)),
                     pl.BlockSpec((1, GDN_CONV - 1, wid), col(2 * D_C)),
                     pl.BlockSpec((1, hg, DH, DH), lambda i, j: (i, j, 0, 0))]
        args += [conv_buf, conv_buf, conv_buf, state0]
    vmem = 2 * (4 * t * wid * 4 + t * LANE * 4 + t * wid * 4 + 2 * hg * DH * DH * 4) + (4 << 20)
    return pl.pallas_call(
        functools.partial(_gdn_kernel, t=t, has_state=has_state),
        grid=(b, H_C // hg),
        in_specs=in_specs,
        out_specs=[pl.BlockSpec((1, t, wid), lambda i, j: (i, 0, j)),
                   pl.BlockSpec((1, hg, DH, DH), lambda i, j: (i, j, 0, 0))],
        out_shape=[jax.ShapeDtypeStruct((b, t, D_C), out_dtype),
                   jax.ShapeDtypeStruct((b, H_C, DH, DH), F32)],
        scratch_shapes=[pltpu.VMEM((GDN_CONV, GDN_CHUNK, wid), F32)] * 3,
        compiler_params=_cparams(("parallel", "parallel"), vmem),
        name="gdn_mixer",
    )(*args)


def _mix_dot(ya_ref, yb_ref, yc_ref, wbf_ref):
    acc = _dot(ya_ref[...], wbf_ref[0:D_A, :])
    acc = acc + _dot(yb_ref[...], wbf_ref[D_A:D_A + D_B, :])
    return acc + _dot(yc_ref[...], wbf_ref[D_A + D_B:, :])


def _proj_out_kernel(yap_ref, ybp_ref, ycp_ref, yas_ref, ybs_ref, ycs_ref, w_ref, xp_ref, xs_ref,
                     op_ref, os_ref, wbf_ref):
    @pl.when(pl.program_id(1) == 0)
    def _():
        wbf_ref[...] = w_ref[...].astype(BF16)
        os_ref[...] = xs_ref[...] + _mix_dot(yas_ref, ybs_ref, ycs_ref, wbf_ref)

    op_ref[...] = xp_ref[...] + _mix_dot(yap_ref, ybp_ref, ycp_ref, wbf_ref)


def proj_out(mix_p, mix_s, w_all, layer, x_p, x_s):
    mp, n = x_p.shape
    ms = x_s.shape[0]
    tm, tn = MM_TM, MM_TN
    kk = D_A + D_B + D_C
    vmem = 2 * ((tm + ms) * kk * 2 + kk * tn * 4 + 2 * (tm + ms) * tn * 4) + 3 * kk * tn * 2
    rowp = lambda j, i: (i, 0)
    rows = lambda j, i: (0, 0)
    widths = (D_A, D_B, D_C)
    return pl.pallas_call(
        _proj_out_kernel,
        grid=(n // tn, mp // tm),
        in_specs=([pl.BlockSpec((tm, w), rowp) for w in widths]
                  + [pl.BlockSpec((ms, w), rows) for w in widths]
                  + [pl.BlockSpec((None, kk, tn), lambda j, i: (layer, 0, j)),
                     pl.BlockSpec((tm, tn), lambda j, i: (i, j)),
                     pl.BlockSpec((ms, tn), lambda j, i: (0, j))]),
        out_specs=[pl.BlockSpec((tm, tn), lambda j, i: (i, j)),
                   pl.BlockSpec((ms, tn), lambda j, i: (0, j))],
        out_shape=[jax.ShapeDtypeStruct((mp, n), F32), jax.ShapeDtypeStruct((ms, n), F32)],
        scratch_shapes=[pltpu.VMEM((kk, tn), BF16)],
        compiler_params=_cparams(("arbitrary", "arbitrary"), vmem),
        name="proj_out",
    )(*mix_p, *mix_s, w_all, x_p, x_s)


FFN_TN = 256


def _ffn_conv(cur, ext_ref, cw_ref, b_ref, rows):
    out = cur * cw_ref[FFN_CONV - 1:FFN_CONV, :] + b_ref[...]
    for i in range(1, FFN_CONV):
        out = out + ext_ref[SUBLANE - i:SUBLANE - i + rows, :] * cw_ref[FFN_CONV - 1 - i:FFN_CONV - i, :]
    return out


def _shift_rows(u, halo, i):
    n, w = u.shape
    rot = pltpu.roll(u.reshape(n // SUBLANE, SUBLANE, w), i, axis=1)
    above = jnp.concatenate([pltpu.roll(halo, i, axis=0)[None], rot[:-1]], axis=0)
    sub = lax.broadcasted_iota(jnp.int32, (1, SUBLANE, 1), 1)
    return jnp.where(sub < i, above, rot).reshape(n, w)


def _ffn_conv_rows(u, halo, cw_ref, b_ref):
    out = u * cw_ref[FFN_CONV - 1:FFN_CONV, :] + b_ref[...]
    for i in range(1, FFN_CONV):
        out = out + _shift_rows(u, halo, i) * cw_ref[FFN_CONV - 1 - i:FFN_CONV - i, :]
    return out


FFN_ROW_SPLIT = 2


def _ffn_up_kernel(xp_ref, xs_ref, wg_ref, wv_ref, cwg_ref, cwv_ref, bg_ref, bv_ref, sg_ref, sv_ref,
                   actp_ref, tg_ref, tv_ref, acts_ref, ugs_ref, uvs_ref,
                   wgbf_ref, wvbf_ref, halo_ref, ext_ref, *, tiles_per_seq, t):
    tm = xp_ref.shape[0]
    rows = tm // FFN_ROW_SPLIT
    mi = pl.program_id(1)

    @pl.when(mi == 0)
    def _():
        wgbf_ref[...] = wg_ref[...].astype(BF16)
        wvbf_ref[...] = wv_ref[...].astype(BF16)
        assert t == SUBLANE
        xs = xs_ref[...]
        nb = xs.shape[0] // t
        us = _dot(xs, wgbf_ref[...])
        vs = _dot(xs, wvbf_ref[...])
        ugs_ref[...] = us
        uvs_ref[...] = vs

        def conv(u, st_ref, cw_ref, b_ref):
            outs = []
            for bi in range(nb):
                cur = u[bi * t:(bi + 1) * t]
                ext_ref[SUBLANE - (FFN_CONV - 1):SUBLANE, :] = st_ref[bi]
                ext_ref[SUBLANE:, :] = cur
                outs.append(_ffn_conv(cur, ext_ref, cw_ref, b_ref, t))
            return jnp.concatenate(outs, axis=0)

        gts = conv(us, sg_ref, cwg_ref, bg_ref)
        vls = conv(vs, sv_ref, cwv_ref, bv_ref)
        acts_ref[...] = (_silu(gts) * vls).astype(acts_ref.dtype)

    @pl.when(mi % tiles_per_seq == 0)
    def _():
        halo_ref[...] = jnp.zeros(halo_ref.shape, F32)

    halo_g = halo_ref[0]
    halo_v = halo_ref[1]
    ug = []
    uv = []
    for s in range(FFN_ROW_SPLIT):
        x = xp_ref[s * rows:(s + 1) * rows, :]
        ug.append(_dot(x, wgbf_ref[...]))
        uv.append(_dot(x, wvbf_ref[...]))
    for s in range(FFN_ROW_SPLIT):
        gt = _ffn_conv_rows(ug[s], halo_g, cwg_ref, bg_ref)
        vl = _ffn_conv_rows(uv[s], halo_v, cwv_ref, bv_ref)
        actp_ref[s * rows:(s + 1) * rows, :] = (_silu(gt) * vl).astype(actp_ref.dtype)
        halo_g = ug[s][rows - SUBLANE:]
        halo_v = uv[s][rows - SUBLANE:]
    halo_ref[0] = halo_g
    halo_ref[1] = halo_v
    tg_ref[0] = halo_g
    tv_ref[0] = halo_v


def ffn_up(xn_p, xn_s, w_up, conv_w, conv_b, conv_state, layer, seq, t):
    mp, k = xn_p.shape
    ms = xn_s.shape[0]
    d_ff = w_up.shape[2] // 2
    tn = FFN_TN
    nt = d_ff // tn
    tm = min(seq, MM_TM)
    mt = mp // tm
    nb = ms // t
    lo = lambda j, i: (layer, 0, j)
    hi = lambda j, i: (layer, 0, nt + j)
    vmem = (2 * (tm * k * 2 + ms * k * 2 + 2 * k * tn * 4 + tm * tn * 2) + 2 * k * tn * 2
            + 8 * tm * tn * 4)
    return pl.pallas_call(
        functools.partial(_ffn_up_kernel, tiles_per_seq=seq // tm, t=t),
        grid=(nt, mt),
        in_specs=[pl.BlockSpec((tm, k), lambda j, i: (i, 0)),
                  pl.BlockSpec((ms, k), lambda j, i: (0, 0)),
                  pl.BlockSpec((None, k, tn), lo), pl.BlockSpec((None, k, tn), hi),
                  pl.BlockSpec((None, FFN_CONV, tn), lo), pl.BlockSpec((None, FFN_CONV, tn), hi),
                  pl.BlockSpec((None, 1, tn), lo), pl.BlockSpec((None, 1, tn), hi),
                  pl.BlockSpec((None, nb, FFN_CONV - 1, tn), lambda j, i: (layer, 0, 0, j)),
                  pl.BlockSpec((None, nb, FFN_CONV - 1, tn), lambda j, i: (layer, 0, 0, nt + j))],
        out_specs=[pl.BlockSpec((tm, tn), lambda j, i: (i, j)),
                   pl.BlockSpec((1, SUBLANE, tn), lambda j, i: (i, 0, j)),
                   pl.BlockSpec((1, SUBLANE, tn), lambda j, i: (i, 0, j)),
                   pl.BlockSpec((ms, tn), lambda j, i: (0, j)),
                   pl.BlockSpec((ms, tn), lambda j, i: (0, j)),
                   pl.BlockSpec((ms, tn), lambda j, i: (0, j))],
        out_shape=[jax.ShapeDtypeStruct((mp, d_ff), BF16),
                   jax.ShapeDtypeStruct((mt, SUBLANE, d_ff), F32),
                   jax.ShapeDtypeStruct((mt, SUBLANE, d_ff), F32),
                   jax.ShapeDtypeStruct((ms, d_ff), BF16),
                   jax.ShapeDtypeStruct((ms, d_ff), F32),
                   jax.ShapeDtypeStruct((ms, d_ff), F32)],
        scratch_shapes=[pltpu.VMEM((k, tn), BF16), pltpu.VMEM((k, tn), BF16),
                        pltpu.VMEM((2, SUBLANE, tn), F32), pltpu.VMEM((SUBLANE + t, tn), F32)],
        compiler_params=_cparams(("arbitrary", "arbitrary"), vmem),
        name="ffn_up",
    )(xn_p, xn_s, w_up, w_up, conv_w, conv_w, conv_b, conv_b, conv_state, conv_state)


def _ffn_down_kernel(a_ref, w_ref, h_ref, o_ref):
    o_ref[...] = h_ref[...] + _dot(a_ref[...], w_ref[...])


def ffn_down_proj(act, w_down, layer, h):
    m, k = act.shape
    n = w_down.shape[2]
    tm = min(m, 512)
    tn = 512
    vmem = 2 * (tm * k * 2 + k * tn * 2 + 2 * tm * tn * 4)
    return pl.pallas_call(
        _ffn_down_kernel,
        grid=(m // tm, n // tn),
        in_specs=[pl.BlockSpec((tm, k), lambda i, j: (i, 0)),
                  pl.BlockSpec((None, k, tn), lambda i, j: (layer, 0, j)),
                  pl.BlockSpec((tm, tn), lambda i, j: (i, j))],
        out_specs=pl.BlockSpec((tm, tn), lambda i, j: (i, j)),
        out_shape=jax.ShapeDtypeStruct((m, n), F32),
        compiler_params=_cparams(("parallel", "parallel"), vmem),
        name="ffn_down",
    )(act, w_down, h)


def _t5_bucket(dist):
    max_exact = REL_BUCKETS // 2
    d = jnp.maximum(dist, 1).astype(F32)
    large = max_exact + (jnp.log(d / max_exact) / math.log(REL_MAX_DIST / max_exact)
                         * (REL_BUCKETS - max_exact)).astype(jnp.int32)
    large = jnp.minimum(large, REL_BUCKETS - 1)
    return jnp.where(dist < max_exact, dist, large)


def _bias_lookup(rel_bias, buckets):
    onehot = jax.nn.one_hot(buckets, REL_BUCKETS, dtype=F32)
    return jnp.einsum('...k,kh->...h', onehot, rel_bias.astype(F32), precision=HIGHEST)


def _prompt_bias_blocks(rel_bias):
    qi = np.arange(BLK)[:, None]
    ki = np.arange(2 * BLK)[None, :]
    rel = qi + BLK - ki
    out = []
    for window, dil in A_BRANCHES:
        nj = window // dil + 1
        valid = (rel >= 0) & (rel < nj)
        buckets = _t5_bucket(jnp.asarray(np.clip(rel, 0, nj - 1) * dil, jnp.int32))
        bias = _bias_lookup(rel_bias, buckets)
        out.append(jnp.where(jnp.asarray(valid)[..., None], bias, NEG_INF).transpose(2, 0, 1))
    return jnp.stack(out)


def _branch_count(dist):
    cnt = np.zeros(dist.shape, np.float32)
    for window, dil in A_BRANCHES:
        cnt += ((dist >= 0) & (dist % dil == 0) & (dist // dil <= window // dil)).astype(np.float32)
    return cnt


def _sample_bias_tables(rel_bias, t, n_past):
    tq = np.arange(t)[:, None]
    d_cache = n_past + tq - np.arange(n_past)[None, :]
    d_new = tq - np.arange(BLK)[None, :]
    d_new = np.where(np.arange(BLK)[None, :] < t, d_new, -1)
    tables = []
    for dist in (d_cache, d_new):
        cnt = _branch_count(dist)
        bias = _bias_lookup(rel_bias, _t5_bucket(jnp.asarray(np.maximum(dist, 0), jnp.int32)))
        bias = jnp.where(jnp.asarray(cnt > 0)[..., None], bias, NEG_INF).transpose(2, 0, 1)
        tables += [bias, jnp.asarray(cnt)]
    return tables


def _mixers(proj, ba, lw, *, batch, seq, layer, cache_hk=None, pool_buf=None, gconv_buf=None,
            gstate=None, bias_tabs=None):
    sample = cache_hk is not None
    mix_dtype = F32 if sample else BF16
    proj3 = proj.reshape(batch, seq, N_MAIN)
    bag = gdn_group_tail(ba, batch, seq)
    qg = lw['a_q_norm'].reshape(1, DH)
    kg = lw['a_k_norm'].reshape(1, DH)
    og = lw['a_out_norm'].reshape(H_A, DH)
    if sample:
        ya, kv = attn_sample(proj3, cache_hk, layer, qg, kg, og, *bias_tabs)
    else:
        ya, kv = attn_prompt(proj3, qg, kg, og, bias_tabs)
    kv_rows = kv.transpose(0, 1, 3, 2, 4)
    yb = pool_mixer(proj3, pool_buf, lw['pool_w'], lw['pool_scale'].reshape(1, D_B), mix_dtype)
    pu = proj3[:, :, OFF_P:OFF_P + D_B]
    if sample:
        pool_new = jnp.concatenate([pool_buf, pu], axis=1)[:, -POOL_BUF:]
    else:
        pool_new = pu[:, -POOL_BUF:]
    yc, gstate_new = gdn_mixer(proj3, bag, lw['gdn_conv_w'], lw['alog_vec'], lw['dt_vec'],
                               lw['gdn_out_norm'].reshape(1, DH), gconv_buf, gstate, mix_dtype)
    gconv_new = proj3[:, -(GDN_CONV - 1):, OFF_C:OFF_C + 3 * D_C]
    m = batch * seq
    mix = (ya.reshape(m, D_A).astype(BF16), yb.reshape(m, D_B).astype(BF16),
           yc.reshape(m, D_C).astype(BF16))
    return mix, (kv_rows, pool_new, gconv_new, gstate_new)


def _layer(xp, xs, lw, gw, *, layer, prompt_shape, sample_shape, cache_hk, pool_buf, gconv_buf, gstate,
           fconv_state, prompt_bias, sample_tabs):
    bp, sp = prompt_shape
    bs, ss = sample_shape
    xnp = rmsnorm_cast(xp, lw['norm_mix'])
    xns = rmsnorm_cast(xs, lw['norm_mix'])
    proj_p, proj_s = proj_in(xnp, xns, gw['w_in'], layer)
    ba_p = proj_tail(xnp, gw['w_in_tail'], layer)
    ba_s = proj_tail(xns, gw['w_in_tail'], layer)
    mix_p, outs_p = _mixers(proj_p, ba_p, lw, batch=bp, seq=sp, layer=layer, bias_tabs=prompt_bias)
    mix_s, outs_s = _mixers(proj_s, ba_s, lw, batch=bs, seq=ss, layer=layer, cache_hk=cache_hk,
                            pool_buf=pool_buf, gconv_buf=gconv_buf, gstate=gstate, bias_tabs=sample_tabs)
    hp, hs = proj_out(mix_p, mix_s, gw['w_out'], layer, xp, xs)
    hnp = rmsnorm_cast(hp, lw['norm_ffn'])
    hns = rmsnorm_cast(hs, lw['norm_ffn'])
    act_p, tg, tv, act_s, ug, uv = ffn_up(hnp, hns, gw['ffn_up'], gw['ffn_conv_w'], gw['ffn_conv_b'],
                                          fconv_state, layer, sp, ss)
    tails = jnp.concatenate([tg, tv], axis=-1)
    tiles_per_seq = tails.shape[0] // bp
    fconv_p = tails[tiles_per_seq - 1::tiles_per_seq, -(FFN_CONV - 1):]
    up_s = jnp.concatenate([ug, uv], axis=-1).reshape(bs, ss, -1)
    fconv_s = up_s[:, -(FFN_CONV - 1):]
    yp = ffn_down_proj(act_p, gw['ffn_down'], layer, hp)
    ys = ffn_down_proj(act_s, gw['ffn_down'], layer, hs)
    return yp, ys, outs_p + (fconv_p,), outs_s + (fconv_s,)


def kernel(x_prompt, x_sample, cache_attn_kv, state_pool, state_gdn_conv, state_gdn, state_ffn_conv,
           rel_bias, norm_mix, w_in, a_q_norm, a_k_norm, a_out_norm, pool_w, pool_scale,
           gdn_conv_w, gdn_a_log, gdn_dt_bias, gdn_out_norm, w_out, norm_ffn,
           ffn_up, ffn_conv_w, ffn_conv_b, ffn_down):
    depth = w_in.shape[0]
    bp, sp, d_model = x_prompt.shape
    bs, ss, _ = x_sample.shape
    n_past = cache_attn_kv.shape[2]
    assert w_in.shape[2] == N_MAIN + N_TAIL and w_out.shape[1] == D_A + D_B + D_C

    prompt_bias = _prompt_bias_blocks(rel_bias)
    sample_tabs = _sample_bias_tables(rel_bias, ss, n_past)
    cache_hk = jnp.swapaxes(cache_attn_kv, 3, 4).reshape(depth, bs, n_past * H_A * 2, DH)

    xp = x_prompt.reshape(bp * sp, d_model)
    xs = x_sample.reshape(bs * ss, d_model)
    p_out = [[] for _ in range(5)]
    s_out = [[] for _ in range(5)]
    gw = {
        'w_in': w_in.astype(BF16),
        'w_in_tail': jnp.pad(w_in[:, :, N_MAIN:], ((0, 0), (0, 0), (0, LANE - N_TAIL))).astype(BF16),
        'w_out': w_out, 'ffn_up': ffn_up, 'ffn_down': ffn_down.astype(BF16),
        'ffn_conv_w': ffn_conv_w, 'ffn_conv_b': ffn_conv_b.reshape(depth, 1, -1),
    }
    for l in range(depth):
        lw = {
            'norm_mix': norm_mix[l], 'norm_ffn': norm_ffn[l],
            'a_q_norm': a_q_norm[l], 'a_k_norm': a_k_norm[l], 'a_out_norm': a_out_norm[l],
            'pool_w': pool_w[l].astype(BF16), 'pool_scale': pool_scale[l],
            'gdn_conv_w': gdn_conv_w[l],
            'alog_vec': gdn_group_vec(gdn_a_log[l]), 'dt_vec': gdn_group_vec(gdn_dt_bias[l]),
            'gdn_out_norm': gdn_out_norm[l],
        }
        xp, xs, outs_p, outs_s = _layer(
            xp, xs, lw, gw, layer=l, prompt_shape=(bp, sp), sample_shape=(bs, ss), cache_hk=cache_hk,
            pool_buf=state_pool[l], gconv_buf=state_gdn_conv[l], gstate=state_gdn[l],
            fconv_state=state_ffn_conv, prompt_bias=prompt_bias, sample_tabs=sample_tabs)
        for acc, o in zip(p_out, outs_p):
            acc.append(o)
        for acc, o in zip(s_out, outs_s):
            acc.append(o)
    res = [xp.reshape(bp, sp, d_model), xs.reshape(bs, ss, d_model)]
    for po, so in zip(p_out, s_out):
        res += [jnp.stack(po), jnp.stack(so)]
    return tuple(res)
```

```python
import functools
import math

import numpy as np
import jax
import jax.numpy as jnp
from jax import lax
from jax.experimental import pallas as pl
from jax.experimental.pallas import tpu as pltpu

F32 = jnp.float32
BF16 = jnp.bfloat16
HIGHEST = lax.Precision.HIGHEST

DH = 128
H_A = 12
H_C = 12
A_BRANCHES = ((128, 1), (512, 4), (2048, 16))
BLK = 128
REL_BUCKETS = 32
REL_MAX_DIST = 2048
POOL_WINDOWS = (2, 4, 8, 16)
CG = 256
POOL_BUF = 15
GDN_CONV = 4
GDN_CHUNK = 64
GDN_SUB = 16
FFN_CONV = 3
EPS = 1e-6
NEG_INF = -1e30

D_A = H_A * DH
D_B = len(POOL_WINDOWS) * CG
D_C = H_C * DH
OFF_Q, OFF_K, OFF_V = 0, D_A, 2 * D_A
OFF_P = 3 * D_A
OFF_C = OFF_P + D_B
OFF_G = OFF_C + 3 * D_C
N_MAIN = OFF_G + D_C
N_TAIL = 2 * H_C

LANE = 128
SUBLANE = 8
VMEM_CAP = 56 * 1024 * 1024


def _cparams(sem, vmem_bytes):
    limit = int(min(max(vmem_bytes * 5 // 4 + (2 << 20), 16 << 20), VMEM_CAP))
    return pltpu.CompilerParams(dimension_semantics=sem, vmem_limit_bytes=limit)


def _rms_rows(x):
    return x * lax.rsqrt(jnp.mean(x * x, axis=-1, keepdims=True) + EPS)


def _silu(x):
    return x * jax.nn.sigmoid(x)


def _dot(a, b):
    return jnp.dot(a, b, preferred_element_type=F32)


def _dot_nt(a, b):
    return lax.dot_general(a, b, (((1,), (1,)), ((), ())), preferred_element_type=F32)


def _hdot(a, b):
    return jnp.dot(a, b, preferred_element_type=F32, precision=HIGHEST)


def _hdot_nt(a, b):
    return lax.dot_general(a, b, (((1,), (1,)), ((), ())), preferred_element_type=F32,
                           precision=HIGHEST)


def _hdot_tn(a, b):
    return lax.dot_general(a, b, (((0,), (0,)), ((), ())), preferred_element_type=F32,
                           precision=HIGHEST)


def _rmsnorm_kernel(x_ref, g_ref, o_ref):
    o_ref[...] = (_rms_rows(x_ref[...]) * g_ref[...]).astype(o_ref.dtype)


def rmsnorm_cast(x, gain):
    m, d = x.shape
    tm = min(m, 256)
    return pl.pallas_call(
        _rmsnorm_kernel,
        grid=(m // tm,),
        in_specs=[pl.BlockSpec((tm, d), lambda i: (i, 0)),
                  pl.BlockSpec((1, d), lambda i: (0, 0))],
        out_specs=pl.BlockSpec((tm, d), lambda i: (i, 0)),
        out_shape=jax.ShapeDtypeStruct((m, d), BF16),
        compiler_params=_cparams(("parallel",), 2 * tm * d * 6),
        name="rmsnorm_cast",
    )(x, gain.reshape(1, d))


MM_TM = 1024
MM_TN = 512


def _proj_in_kernel(xp_ref, xs_ref, w_ref, op_ref, os_ref, wbf_ref):
    @pl.when(pl.program_id(1) == 0)
    def _():
        wbf_ref[...] = w_ref[...].astype(BF16)
        os_ref[...] = _dot_nt(xs_ref[...], wbf_ref[...])

    op_ref[...] = _dot_nt(xp_ref[...], wbf_ref[...])


def proj_in(xn_p, xn_s, w_t, layer):
    mp, k = xn_p.shape
    ms = xn_s.shape[0]
    tm, tn = MM_TM, MM_TN
    vmem = 2 * (tm * k * 2 + ms * k * 2 + k * tn * 4 + tm * tn * 4 + ms * tn * 4) + 3 * k * tn * 2
    return pl.pallas_call(
        _proj_in_kernel,
        grid=(N_MAIN // tn, mp // tm),
        in_specs=[pl.BlockSpec((tm, k), lambda j, i: (i, 0)),
                  pl.BlockSpec((ms, k), lambda j, i: (0, 0)),
                  pl.BlockSpec((None, tn, k), lambda j, i: (layer, j, 0))],
        out_specs=[pl.BlockSpec((tm, tn), lambda j, i: (i, j)),
                   pl.BlockSpec((ms, tn), lambda j, i: (0, j))],
        out_shape=[jax.ShapeDtypeStruct((mp, N_MAIN), F32),
                   jax.ShapeDtypeStruct((ms, N_MAIN), F32)],
        scratch_shapes=[pltpu.VMEM((tn, k), BF16)],
        compiler_params=_cparams(("arbitrary", "arbitrary"), vmem),
        name="proj_in",
    )(xn_p, xn_s, w_t)


def _proj_tail_kernel(x_ref, *refs):
    w_refs, o_ref = refs[:-1], refs[-1]
    k = x_ref.shape[1]
    pad = jnp.zeros((LANE - SUBLANE * len(w_refs), k), F32)
    w = jnp.concatenate([r[...] for r in w_refs] + [pad], axis=0).astype(BF16)
    o_ref[...] = _dot_nt(x_ref[...], w)


def proj_tail(xn, w_t, layer):
    m, k = xn.shape
    tm = min(m, MM_TM)
    assert N_TAIL % SUBLANE == 0 and N_MAIN % SUBLANE == 0
    nblk = N_TAIL // SUBLANE
    vmem = 2 * (tm * k * 2 + nblk * SUBLANE * k * 4 + tm * LANE * 4) + 2 * LANE * k * 4
    return pl.pallas_call(
        _proj_tail_kernel,
        grid=(m // tm,),
        in_specs=[pl.BlockSpec((tm, k), lambda i: (i, 0))]
        + [pl.BlockSpec((None, SUBLANE, k), (lambda i, r=r: (layer, N_MAIN // SUBLANE + r, 0)))
           for r in range(nblk)],
        out_specs=pl.BlockSpec((tm, LANE), lambda i: (i, 0)),
        out_shape=jax.ShapeDtypeStruct((m, LANE), F32),
        compiler_params=_cparams(("parallel",), vmem),
        name="proj_tail",
    )(xn, *([w_t] * nblk))


ATTN_BLOCKS_PER_TRIP = 4


def _strided_rows(start, dil):
    return pl.ds(start, BLK) if dil == 1 else pl.ds(start, BLK, stride=dil)


def _attn_prompt_kernel(q_ref, k_ref, v_ref, qg_ref, kg_ref, og_ref, bias_ref,
                        y_ref, kv_ref, qs_ref, ks_ref, acc_ref, m_ref, l_ref):
    seq = q_ref.shape[1]
    qs_ref[...] = _rms_rows(q_ref[0]) * qg_ref[...] * (DH ** -0.5)
    ks_ref[...] = _rms_rows(k_ref[0]) * kg_ref[...]
    kv_ref[0, :, 0, 0, :] = ks_ref[...]
    kv_ref[0, :, 0, 1, :] = v_ref[0]
    acc_ref[...] = jnp.zeros(acc_ref.shape, F32)
    l_ref[...] = jnp.zeros(l_ref.shape, F32)
    m_ref[...] = jnp.full(m_ref.shape, NEG_INF, F32)

    for bi, (window, dil) in enumerate(A_BRANCHES):
        assert window // dil == BLK and seq % (dil * BLK) == 0
        nb = seq // (dil * BLK)

        assert (dil * nb) % ATTN_BLOCKS_PER_TRIP == 0

        def body(it, carry, bi=bi, dil=dil, nb=nb):
            us = range(ATTN_BLOCKS_PER_TRIP)
            item = [it * ATTN_BLOCKS_PER_TRIP + u for u in us]
            r = [x // nb for x in item]
            n = [item[u] - r[u] * nb for u in us]
            rows = [_strided_rows(n[u] * (BLK * dil) + r[u], dil) for u in us]
            qb = [qs_ref[rw, :].astype(BF16) for rw in rows]
            kc = [ks_ref[rw, :] for rw in rows]
            vc = [v_ref[0, rw, :] for rw in rows]
            bias_cur = bias_ref[bi, 0, :, BLK:]
            if nb > 1:
                prows = [_strided_rows(jnp.maximum(n[u] - 1, 0) * (BLK * dil) + r[u], dil) for u in us]
                kk = [jnp.concatenate([ks_ref[prows[u], :], kc[u]], axis=0).astype(BF16) for u in us]
                vv = [jnp.concatenate([v_ref[0, prows[u], :], vc[u]], axis=0).astype(BF16) for u in us]
                bias_prev = bias_ref[bi, 0, :, :BLK]
                bias = [jnp.concatenate([jnp.where(n[u] > 0, bias_prev, NEG_INF), bias_cur], axis=1)
                        for u in us]
            else:
                kk = [x.astype(BF16) for x in kc]
                vv = [x.astype(BF16) for x in vc]
                bias = [bias_cur for _ in us]
            s = [_dot_nt(qb[u], kk[u]) + bias[u] for u in us]
            m_old = [m_ref[rw, :] for rw in rows]
            m_new = [jnp.maximum(m_old[u], jnp.max(s[u], axis=1, keepdims=True)) for u in us]
            p = [jnp.exp(s[u] - m_new[u][:, :1]) for u in us]
            alpha = [jnp.exp(m_old[u] - m_new[u]) for u in us]
            pv = [_dot(p[u].astype(BF16), vv[u]) for u in us]
            for u in us:
                l_ref[rows[u], :] = alpha[u] * l_ref[rows[u], :] + jnp.sum(p[u], axis=1, keepdims=True)
                acc_ref[rows[u], :] = alpha[u] * acc_ref[rows[u], :] + pv[u]
                m_ref[rows[u], :] = m_new[u]
            return carry

        lax.fori_loop(0, dil * nb // ATTN_BLOCKS_PER_TRIP, body, 0)

    o = acc_ref[...] / l_ref[...]
    y_ref[0] = (_rms_rows(o) * og_ref[0]).astype(y_ref.dtype)


def attn_prompt(proj3, q_gain, k_gain, o_gain, bias_blocks):
    b, s, _ = proj3.shape
    col = lambda off: (lambda i, h: (i, 0, off // DH + h))
    vmem = 2 * (3 * s * DH * 4 + s * DH * 2 + s * DH * 4 + 3 * BLK * 2 * BLK * 4) + 5 * s * DH * 4
    return pl.pallas_call(
        _attn_prompt_kernel,
        grid=(b, H_A),
        in_specs=[pl.BlockSpec((1, s, DH), col(OFF_Q)),
                  pl.BlockSpec((1, s, DH), col(OFF_K)),
                  pl.BlockSpec((1, s, DH), col(OFF_V)),
                  pl.BlockSpec((1, DH), lambda i, h: (0, 0)),
                  pl.BlockSpec((1, DH), lambda i, h: (0, 0)),
                  pl.BlockSpec((1, 1, DH), lambda i, h: (h, 0, 0)),
                  pl.BlockSpec((len(A_BRANCHES), 1, BLK, 2 * BLK), lambda i, h: (0, h, 0, 0))],
        out_specs=[pl.BlockSpec((1, s, DH), lambda i, h: (i, 0, h)),
                   pl.BlockSpec((1, s, 1, 2, DH), lambda i, h: (i, 0, h, 0, 0))],
        out_shape=[jax.ShapeDtypeStruct((b, s, D_A), BF16),
                   jax.ShapeDtypeStruct((b, s, H_A, 2, DH), F32)],
        scratch_shapes=[pltpu.VMEM((s, DH), F32)] * 5,
        compiler_params=_cparams(("parallel", "parallel"), vmem),
        name="attn_prompt",
    )(proj3, proj3, proj3, q_gain, k_gain, o_gain.reshape(H_A, 1, DH), bias_blocks)


ATTN_SAMPLE_TK = 256


def _attn_sample_kernel(q_ref, k_ref, v_ref, c_ref, qg_ref, kg_ref, og_ref,
                        bc_ref, cc_ref, bn_ref, cn_ref, y_ref, kv_ref, qn_ref, m_ref, l_ref, acc_ref):
    t = q_ref.shape[1]
    tk = c_ref.shape[2] // (2 * H_A)
    kt = pl.program_id(1)
    heads = range(H_A)
    cols = [slice(h * DH, (h + 1) * DH) for h in heads]

    @pl.when(kt == 0)
    def _():
        pad = jnp.zeros((BLK - t, DH), F32)
        for h in heads:
            qn = _rms_rows(q_ref[0, :, cols[h]]) * qg_ref[...] * (DH ** -0.5)
            kn = _rms_rows(k_ref[0, :, cols[h]]) * kg_ref[...]
            vn = v_ref[0, :, cols[h]]
            qn_ref[:, cols[h]] = qn
            kv_ref[0, :, h, 0, :] = kn
            kv_ref[0, :, h, 1, :] = vn
            kpad = jnp.concatenate([kn, pad], axis=0).astype(BF16)
            vpad = jnp.concatenate([vn, pad], axis=0).astype(BF16)
            s = _dot_nt(qn.astype(BF16), kpad) + bn_ref[h]
            m = jnp.max(s, axis=1, keepdims=True)
            p = jnp.exp(s - m) * cn_ref[...]
            m_ref[h] = jnp.broadcast_to(m, (t, DH))
            l_ref[h] = jnp.broadcast_to(jnp.sum(p, axis=1, keepdims=True), (t, DH))
            acc_ref[h] = _dot(p.astype(BF16), vpad)

    q16 = [qn_ref[:, cols[h]].astype(BF16) for h in heads]
    kc = [c_ref[0, 0, pl.ds(2 * h, tk, stride=2 * H_A), :].astype(BF16) for h in heads]
    vc = [c_ref[0, 0, pl.ds(2 * h + 1, tk, stride=2 * H_A), :].astype(BF16) for h in heads]
    s = [_dot_nt(q16[h], kc[h]) + bc_ref[h] for h in heads]
    m_old = [m_ref[h] for h in heads]
    m_new = [jnp.maximum(m_old[h], jnp.max(s[h], axis=1, keepdims=True)) for h in heads]
    p = [jnp.exp(s[h] - m_new[h][:, :1]) * cc_ref[...] for h in heads]
    alpha = [jnp.exp(m_old[h] - m_new[h]) for h in heads]
    pv = [_dot(p[h].astype(BF16), vc[h]) for h in heads]
    for h in heads:
        l_ref[h] = alpha[h] * l_ref[h] + jnp.sum(p[h], axis=1, keepdims=True)
        acc_ref[h] = alpha[h] * acc_ref[h] + pv[h]
        m_ref[h] = m_new[h]

    @pl.when(kt == pl.num_programs(1) - 1)
    def _():
        for h in heads:
            o = acc_ref[h] / l_ref[h]
            y_ref[0, :, cols[h]] = _rms_rows(o) * og_ref[:, cols[h]]


def attn_sample(proj3, cache_rows, layer, q_gain, k_gain, o_gain, bias_c, cnt_c, bias_n, cnt_n):
    b, t, _ = proj3.shape
    n_past = cache_rows.shape[2] // (2 * H_A)
    tk = ATTN_SAMPLE_TK
    col = lambda off: (lambda i, j: (i, 0, off // D_A))
    vmem = (2 * (tk * 2 * H_A * DH * 4 + 3 * t * D_A * 4 + H_A * t * tk * 4 + t * tk * 4
                 + 2 * t * 2 * D_A * 4) + 4 * H_A * t * DH * 4 + (2 << 20))
    return pl.pallas_call(
        _attn_sample_kernel,
        grid=(b, n_past // tk),
        in_specs=[pl.BlockSpec((1, t, D_A), col(OFF_Q)),
                  pl.BlockSpec((1, t, D_A), col(OFF_K)),
                  pl.BlockSpec((1, t, D_A), col(OFF_V)),
                  pl.BlockSpec((1, 1, tk * 2 * H_A, DH), lambda i, j: (layer, i, j, 0)),
                  pl.BlockSpec((1, DH), lambda i, j: (0, 0)),
                  pl.BlockSpec((1, DH), lambda i, j: (0, 0)),
                  pl.BlockSpec((1, D_A), lambda i, j: (0, 0)),
                  pl.BlockSpec((H_A, t, tk), lambda i, j: (0, 0, j)),
                  pl.BlockSpec((t, tk), lambda i, j: (0, j)),
                  pl.BlockSpec((H_A, t, BLK), lambda i, j: (0, 0, 0)),
                  pl.BlockSpec((t, BLK), lambda i, j: (0, 0))],
        out_specs=[pl.BlockSpec((1, t, D_A), lambda i, j: (i, 0, 0)),
                   pl.BlockSpec((1, t, H_A, 2, DH), lambda i, j: (i, 0, 0, 0, 0))],
        out_shape=[jax.ShapeDtypeStruct((b, t, D_A), F32),
                   jax.ShapeDtypeStruct((b, t, H_A, 2, DH), F32)],
        scratch_shapes=[pltpu.VMEM((t, D_A), F32)] + [pltpu.VMEM((H_A, t, DH), F32)] * 3,
        compiler_params=_cparams(("parallel", "arbitrary"), vmem),
        name="attn_sample",
    )(proj3, proj3, proj3, cache_rows, q_gain, k_gain, o_gain.reshape(1, D_A),
      bias_c, cnt_c, bias_n, cnt_n)


POOL_HALO = 16


def _pool_kernel(*refs, t, rows, n_valid, has_buf):
    ng = len(POOL_WINDOWS)
    u_refs = refs[:ng]
    refs = refs[ng:]
    if has_buf:
        buf_refs = refs[:ng]
        refs = refs[ng:]
    w_ref, scale_ref, y_ref, ext_ref = refs
    for g, win in enumerate(POOL_WINDOWS):
        ext_ref[0:POOL_HALO, :] = jnp.zeros((POOL_HALO, CG), F32)
        if has_buf:
            ext_ref[POOL_HALO - POOL_BUF:POOL_HALO, :] = buf_refs[g][0]
        ext_ref[POOL_HALO:, :] = u_refs[g][0]

        def chunk(ci, carry, g=g, win=win):
            base = pl.multiple_of(ci * rows, SUBLANE)
            w = ext_ref[pl.ds(base, rows + POOL_HALO), :]
            u = w[POOL_HALO:]
            tot = u
            for i in range(1, win):
                tot = tot + w[POOL_HALO - i:POOL_HALO - i + rows]
            pos = base + lax.broadcasted_iota(jnp.int32, (rows, 1), 0)
            cnt = jnp.minimum(win, n_valid + pos + 1).astype(F32)
            d = tot / cnt - u
            y = _rms_rows(_dot(d.astype(BF16), w_ref[g])) * scale_ref[:, g * CG:(g + 1) * CG]
            y_ref[0, pl.ds(base, rows), g * CG:(g + 1) * CG] = y.astype(y_ref.dtype)
            return carry

        lax.fori_loop(0, t // rows, chunk, 0)


def pool_mixer(proj3, bufs, w_pool, scale, out_dtype):
    b, t, _ = proj3.shape
    ng = len(POOL_WINDOWS)
    rows = min(t, 256)
    has_buf = bufs is not None
    in_specs = [pl.BlockSpec((1, t, CG), (lambda i, g=g: (i, 0, OFF_P // CG + g))) for g in range(ng)]
    args = [proj3] * ng
    if has_buf:
        in_specs += [pl.BlockSpec((1, POOL_BUF, CG), (lambda i, g=g: (i, 0, g))) for g in range(ng)]
        args += [bufs] * ng
    in_specs += [pl.BlockSpec((ng, CG, CG), lambda i: (0, 0, 0)),
                 pl.BlockSpec((1, D_B), lambda i: (0, 0))]
    args += [w_pool, scale]
    vmem = 2 * (ng * t * CG * 4 + t * D_B * 4 + ng * CG * CG * 2) + (t + POOL_HALO) * CG * 4
    return pl.pallas_call(
        functools.partial(_pool_kernel, t=t, rows=rows, n_valid=POOL_BUF if has_buf else 0,
                          has_buf=has_buf),
        grid=(b,),
        in_specs=in_specs,
        out_specs=pl.BlockSpec((1, t, D_B), lambda i: (i, 0, 0)),
        out_shape=jax.ShapeDtypeStruct((b, t, D_B), out_dtype),
        scratch_shapes=[pltpu.VMEM((t + POOL_HALO, CG), F32)],
        compiler_params=_cparams(("parallel",), vmem),
        name="pool_mixer",
    )(*args)


GDN_HEADS_PER_STEP = 4
GDN_CHUNKS_PER_TRIP = 4


def _split_bf16(a):
    hi = a.astype(BF16)
    return hi, (a - hi.astype(F32)).astype(BF16)


def _dot3(a, b):
    return _dot(a[0], b[0]) + (_dot(a[0], b[1]) + _dot(a[1], b[0]))


def _cumsum_rows(x):
    n = x.shape[0]
    row = lax.broadcasted_iota(jnp.int32, (n, 1), 0)
    k = 1
    while k < n:
        x = x + jnp.where(row >= k, pltpu.roll(x, k, axis=0), 0.0)
        k *= 2
    return x


def _unit_lower_solve(mats, rhss, eye, sub_diag):
    assert GDN_CHUNK // GDN_SUB == 4 and GDN_SUB == 16
    idx = range(len(mats))
    d = [jnp.where(sub_diag, m, 0.0) for m in mats]
    low = [_split_bf16(m - di) for m, di in zip(mats, d)]
    rs = [_split_bf16(r) for r in rhss]
    x = [eye - di for di in d]
    ps = [_split_bf16(di) for di in d]
    ps = [_split_bf16(_dot3(ps[i], ps[i])) for i in idx]
    for _ in range(2):
        xs = [_split_bf16(xi) for xi in x]
        x = [x[i] + _dot3(xs[i], ps[i]) for i in idx]
        ps = [_split_bf16(_dot3(ps[i], ps[i])) for i in idx]
    xs = [_split_bf16(xi) for xi in x]
    x = [x[i] + _dot3(xs[i], ps[i]) for i in idx]
    xs = [_split_bf16(xi) for xi in x]
    n = [_dot3(xs[i], low[i]) for i in idx]
    xr = [_dot3(xs[i], rs[i]) for i in idx]
    ns = [_split_bf16(ni) for ni in n]
    n2 = [_split_bf16(_dot3(ns[i], ns[i])) for i in idx]
    imn = [eye - ni for ni in n]
    y = [imn[i] + _dot3(_split_bf16(imn[i]), n2[i]) for i in idx]
    return [_dot3(_split_bf16(y[i]), _split_bf16(xr[i])) for i in idx]


def _gdn_kernel(*refs, t, has_state):
    c = GDN_CHUNK
    hg = GDN_HEADS_PER_STEP
    (q_ref, k_ref, v_ref, gate_ref, ba_ref, cwq_ref, cwk_ref, cwv_ref,
     alog_ref, dt_ref, gain_ref) = refs[:11]
    refs = refs[11:]
    if has_state:
        cbq_ref, cbk_ref, cbv_ref, s0_ref = refs[:4]
        refs = refs[4:]
    y_ref, s_ref, cwq_rows, cwk_rows, cwv_rows = refs
    for rows_ref, cw_in in ((cwq_rows, cwq_ref), (cwk_rows, cwk_ref), (cwv_rows, cwv_ref)):
        for j in range(GDN_CONV):
            rows_ref[j] = jnp.broadcast_to(cw_in[j:j + 1, :], rows_ref.shape[1:])
    n_chunks = -(-t // c)
    padded = t % c != 0
    assert (not padded) or n_chunks == 1
    wid = hg * DH
    heads = range(hg)

    if has_state:
        s_ref[0] = s0_ref[0]
    else:
        s_ref[0] = jnp.zeros(s_ref.shape[1:], F32)

    ii = lax.broadcasted_iota(jnp.int32, (c, c), 0)
    jj = lax.broadcasted_iota(jnp.int32, (c, c), 1)
    tril = ii >= jj
    strict = ii > jj
    eye = (ii == jj).astype(F32)
    sub_diag = (ii // GDN_SUB) == (jj // GDN_SUB)
    row_id = lax.broadcasted_iota(jnp.int32, (c, 1), 0)

    def halo_rows(ref, cb_ref, ci):
        if n_chunks > 1:
            prev = ref[0, pl.ds(pl.multiple_of(jnp.maximum(ci * c - SUBLANE, 0), SUBLANE), SUBLANE), :]
        else:
            prev = jnp.zeros((SUBLANE, wid), F32)
        if has_state:
            pad = jnp.zeros((SUBLANE - (GDN_CONV - 1), wid), F32)
            first = jnp.concatenate([pad, cb_ref[0]], axis=0)
        else:
            first = jnp.zeros((SUBLANE, wid), F32)
        return jnp.where(ci > 0, prev, first)

    def conv_silu(ref, cb_ref, cw_ref, ci):
        if padded:
            cur = jnp.concatenate([ref[0], jnp.zeros((c - t, wid), F32)], axis=0)
        else:
            cur = ref[0, pl.ds(pl.multiple_of(ci * c, c), c), :]
        w = jnp.concatenate([halo_rows(ref, cb_ref, ci), cur], axis=0)
        out = cur * cw_ref[GDN_CONV - 1]
        for i in range(1, GDN_CONV):
            out = out + w[SUBLANE - i:SUBLANE - i + c] * cw_ref[GDN_CONV - 1 - i]
        return _silu(out)

    cpt = GDN_CHUNKS_PER_TRIP if n_chunks % GDN_CHUNKS_PER_TRIP == 0 else 1
    cols = [slice(i * DH, (i + 1) * DH) for i in heads]

    def body(it, carry):
        cb = (cbq_ref, cbk_ref, cbv_ref) if has_state else (None, None, None)
        q, k, v, gate, beta, gcum, g_last, decay = [], [], [], [], [], [], [], []
        for u in range(cpt):
            ci = it * cpt + u
            qa = conv_silu(q_ref, cb[0], cwq_rows, ci)
            ka = conv_silu(k_ref, cb[1], cwk_rows, ci)
            va = conv_silu(v_ref, cb[2], cwv_rows, ci)
            if padded:
                ba = jnp.concatenate([ba_ref[0, 0], jnp.zeros((c - t, LANE), F32)], axis=0)
                gate.append(jnp.concatenate([gate_ref[0], jnp.zeros((c - t, wid), F32)], axis=0))
                live = row_id < t
            else:
                ba = ba_ref[0, 0, pl.ds(pl.multiple_of(ci * c, c), c), :]
                gate.append(gate_ref[0, pl.ds(pl.multiple_of(ci * c, c), c), :])
            beta_all = jax.nn.sigmoid(ba)
            z = ba + dt_ref[0]
            softplus = jnp.maximum(z, 0.0) + jnp.log1p(jnp.exp(-jnp.abs(z)))
            g_all = -jnp.exp(alog_ref[0]) * softplus
            if padded:
                beta_all = jnp.where(live, beta_all, 0.0)
                g_all = jnp.where(live, g_all, 0.0)
            gcum_all = _cumsum_rows(g_all)
            gcum_t = gcum_all.T
            for i in heads:
                qi = qa[:, cols[i]]
                ki = ka[:, cols[i]]
                vi = va[:, cols[i]]
                qi = qi * lax.rsqrt(jnp.sum(qi * qi, axis=-1, keepdims=True) + EPS) * (DH ** -0.5)
                ki = ki * lax.rsqrt(jnp.sum(ki * ki, axis=-1, keepdims=True) + EPS)
                if padded:
                    qi = jnp.where(live, qi, 0.0)
                    ki = jnp.where(live, ki, 0.0)
                    vi = jnp.where(live, vi, 0.0)
                q.append(qi)
                k.append(ki)
                v.append(vi)
                beta.append(beta_all[:, i:i + 1])
                gc = gcum_all[:, hg + i:hg + i + 1]
                gcum.append(gc)
                g_last.append(gcum_all[c - 1:c, hg + i:hg + i + 1])
                decay.append(jnp.where(
                    tril, jnp.exp(jnp.where(tril, gc - gcum_t[hg + i:hg + i + 1, :], 0.0)), 0.0))
        chains = range(cpt * hg)
        e_cum = [jnp.exp(g) for g in gcum]
        kb = [k[n] * beta[n] for n in chains]
        k16 = [x.astype(BF16) for x in k]
        kk = [_dot_nt(kb[n].astype(BF16), k16[n]) for n in chains]
        qk = [_dot_nt(q[n].astype(BF16), k16[n]) for n in chains]
        m_mat = [jnp.where(strict, kk[n] * decay[n], 0.0) for n in chains]
        a_qk = [(qk[n] * decay[n]).astype(BF16) for n in chains]
        rhs = [jnp.concatenate([v[n] * beta[n], kb[n] * e_cum[n]], axis=1) for n in chains]
        sol = _unit_lower_solve(m_mat, rhs, eye, sub_diag)
        q_dec = [(q[n] * e_cum[n]).astype(BF16) for n in chains]
        k_dec_t = [(k[n] * jnp.exp(g_last[n] - gcum[n])).T.astype(BF16) for n in chains]
        state = [s_ref[0, i] for i in heads]
        for u in range(cpt):
            ns = [u * hg + i for i in heads]
            s16 = [x.astype(BF16) for x in state]
            w_s = [_dot(sol[ns[i]][:, DH:].astype(BF16), s16[i]) for i in heads]
            q_s = [_dot(q_dec[ns[i]], s16[i]) for i in heads]
            vn16 = [(sol[ns[i]][:, :DH] - w_s[i]).astype(BF16) for i in heads]
            o = [q_s[i] + _dot(a_qk[ns[i]], vn16[i]) for i in heads]
            upd = [_dot(k_dec_t[ns[i]], vn16[i]) for i in heads]
            state = [state[i] * jnp.exp(g_last[ns[i]]) + upd[i] for i in heads]
            y = jnp.concatenate(
                [_rms_rows(o[i]) * gain_ref[...] * _silu(gate[u][:, cols[i]]) for i in heads], axis=1)
            if padded:
                y_ref[0] = y[:t].astype(y_ref.dtype)
            else:
                y_ref[0, pl.ds(pl.multiple_of((it * cpt + u) * c, c), c), :] = y.astype(y_ref.dtype)
        for i in heads:
            s_ref[0, i] = state[i]
        return carry

    lax.fori_loop(0, n_chunks // cpt, body, 0)


def gdn_group_tail(ba, batch, seq):
    hg = GDN_HEADS_PER_STEP
    groups = H_C // hg
    br = ba[:, :H_C].reshape(batch, seq, groups, hg)
    ar = ba[:, H_C:2 * H_C].reshape(batch, seq, groups, hg)
    cat = jnp.concatenate([br, ar], axis=-1).transpose(0, 2, 1, 3)
    return jnp.pad(cat, ((0, 0), (0, 0), (0, 0), (0, LANE - 2 * hg)))


def gdn_group_vec(v):
    hg = GDN_HEADS_PER_STEP
    return jnp.pad(v.reshape(H_C // hg, 1, hg), ((0, 0), (0, 0), (hg, LANE - 2 * hg)))


def gdn_mixer(proj3, bag, conv_w, alog_vec, dt_vec, out_gain, conv_buf, state0, out_dtype):
    b, t, _ = proj3.shape
    hg = GDN_HEADS_PER_STEP
    wid = hg * DH
    has_state = state0 is not None
    col = lambda off: (lambda i, j: (i, 0, off // wid + j))
    wcol = lambda off: (lambda i, j: (0, off // wid + j))
    in_specs = [pl.BlockSpec((1, t, wid), col(OFF_C)),
                pl.BlockSpec((1, t, wid), col(OFF_C + D_C)),
                pl.BlockSpec((1, t, wid), col(OFF_C + 2 * D_C)),
                pl.BlockSpec((1, t, wid), col(OFF_G)),
                pl.BlockSpec((1, 1, t, LANE), lambda i, j: (i, j, 0, 0)),
                pl.BlockSpec((GDN_CONV, wid), wcol(0)),
                pl.BlockSpec((GDN_CONV, wid), wcol(D_C)),
                pl.BlockSpec((GDN_CONV, wid), wcol(2 * D_C)),
                pl.BlockSpec((1, 1, LANE), lambda i, j: (j, 0, 0)),
                pl.BlockSpec((1, 1, LANE), lambda i, j: (j, 0, 0)),
                pl.BlockSpec((1, DH), lambda i, j: (0, 0))]
    args = [proj3, proj3, proj3, proj3, bag, conv_w, conv_w, conv_w, alog_vec, dt_vec, out_gain]
    if has_state:
        in_specs += [pl.BlockSpec((1, GDN_CONV - 1, wid), col(0)),
                     pl.BlockSpec((1, GDN_CONV - 1, wid), col(D_C)),
                     pl.BlockSpec((1, GDN_CONV - 1, wid), col(2 * D_C)),
                     pl.BlockSpec((1, hg, DH, DH), lambda i, j: (i, j, 0, 0))]
        args += [conv_buf, conv_buf, conv_buf, state0]
    vmem = 2 * (4 * t * wid * 4 + t * LANE * 4 + t * wid * 4 + 2 * hg * DH * DH * 4) + (4 << 20)
    return pl.pallas_call(
        functools.partial(_gdn_kernel, t=t, has_state=has_state),
        grid=(b, H_C // hg),
        in_specs=in_specs,
        out_specs=[pl.BlockSpec((1, t, wid), lambda i, j: (i, 0, j)),
                   pl.BlockSpec((1, hg, DH, DH), lambda i, j: (i, j, 0, 0))],
        out_shape=[jax.ShapeDtypeStruct((b, t, D_C), out_dtype),
                   jax.ShapeDtypeStruct((b, H_C, DH, DH), F32)],
        scratch_shapes=[pltpu.VMEM((GDN_CONV, GDN_CHUNK, wid), F32)] * 3,
        compiler_params=_cparams(("parallel", "parallel"), vmem),
        name="gdn_mixer",
    )(*args)


def _mix_dot(ya_ref, yb_ref, yc_ref, wbf_ref):
    acc = _dot(ya_ref[...], wbf_ref[0:D_A, :])
    acc = acc + _dot(yb_ref[...], wbf_ref[D_A:D_A + D_B, :])
    return acc + _dot(yc_ref[...], wbf_ref[D_A + D_B:, :])


def _proj_out_kernel(yap_ref, ybp_ref, ycp_ref, yas_ref, ybs_ref, ycs_ref, w_ref, xp_ref, xs_ref,
                     op_ref, os_ref, wbf_ref):
    @pl.when(pl.program_id(1) == 0)
    def _():
        wbf_ref[...] = w_ref[...].astype(BF16)
        os_ref[...] = xs_ref[...] + _mix_dot(yas_ref, ybs_ref, ycs_ref, wbf_ref)

    op_ref[...] = xp_ref[...] + _mix_dot(yap_ref, ybp_ref, ycp_ref, wbf_ref)


def proj_out(mix_p, mix_s, w_all, layer, x_p, x_s):
    mp, n = x_p.shape
    ms = x_s.shape[0]
    tm, tn = MM_TM, MM_TN
    kk = D_A + D_B + D_C
    vmem = 2 * ((tm + ms) * kk * 2 + kk * tn * 4 + 2 * (tm + ms) * tn * 4) + 3 * kk * tn * 2
    rowp = lambda j, i: (i, 0)
    rows = lambda j, i: (0, 0)
    widths = (D_A, D_B, D_C)
    return pl.pallas_call(
        _proj_out_kernel,
        grid=(n // tn, mp // tm),
        in_specs=([pl.BlockSpec((tm, w), rowp) for w in widths]
                  + [pl.BlockSpec((ms, w), rows) for w in widths]
                  + [pl.BlockSpec((None, kk, tn), lambda j, i: (layer, 0, j)),
                     pl.BlockSpec((tm, tn), lambda j, i: (i, j)),
                     pl.BlockSpec((ms, tn), lambda j, i: (0, j))]),
        out_specs=[pl.BlockSpec((tm, tn), lambda j, i: (i, j)),
                   pl.BlockSpec((ms, tn), lambda j, i: (0, j))],
        out_shape=[jax.ShapeDtypeStruct((mp, n), F32), jax.ShapeDtypeStruct((ms, n), F32)],
        scratch_shapes=[pltpu.VMEM((kk, tn), BF16)],
        compiler_params=_cparams(("arbitrary", "arbitrary"), vmem),
        name="proj_out",
    )(*mix_p, *mix_s, w_all, x_p, x_s)


FFN_TN = 256


def _ffn_conv(cur, ext_ref, cw_ref, b_ref, rows):
    out = cur * cw_ref[FFN_CONV - 1:FFN_CONV, :] + b_ref[...]
    for i in range(1, FFN_CONV):
        out = out + ext_ref[SUBLANE - i:SUBLANE - i + rows, :] * cw_ref[FFN_CONV - 1 - i:FFN_CONV - i, :]
    return out


def _shift_rows(u, halo, i):
    n, w = u.shape
    rot = pltpu.roll(u.reshape(n // SUBLANE, SUBLANE, w), i, axis=1)
    above = jnp.concatenate([pltpu.roll(halo, i, axis=0)[None], rot[:-1]], axis=0)
    sub = lax.broadcasted_iota(jnp.int32, (1, SUBLANE, 1), 1)
    return jnp.where(sub < i, above, rot).reshape(n, w)


def _ffn_conv_rows(u, halo, cw_ref, b_ref):
    out = u * cw_ref[FFN_CONV - 1:FFN_CONV, :] + b_ref[...]
    for i in range(1, FFN_CONV):
        out = out + _shift_rows(u, halo, i) * cw_ref[FFN_CONV - 1 - i:FFN_CONV - i, :]
    return out


FFN_ROW_SPLIT = 2


def _ffn_up_kernel(xp_ref, xs_ref, wg_ref, wv_ref, cwg_ref, cwv_ref, bg_ref, bv_ref, sg_ref, sv_ref,
                   wd_ref, actp_ref, tg_ref, tv_ref, acts_ref, ugs_ref, uvs_ref, wd16_ref,
                   wgbf_ref, wvbf_ref, halo_ref, ext_ref, *, tiles_per_seq, t):
    tm = xp_ref.shape[0]
    rows = tm // FFN_ROW_SPLIT
    mi = pl.program_id(1)

    @pl.when(mi == 0)
    def _():
        wgbf_ref[...] = wg_ref[...].astype(BF16)
        wvbf_ref[...] = wv_ref[...].astype(BF16)
        wd16_ref[...] = wd_ref[...].astype(BF16)
        assert t == SUBLANE
        xs = xs_ref[...]
        nb = xs.shape[0] // t
        us = _dot(xs, wgbf_ref[...])
        vs = _dot(xs, wvbf_ref[...])
        ugs_ref[...] = us
        uvs_ref[...] = vs

        def conv(u, st_ref, cw_ref, b_ref):
            outs = []
            for bi in range(nb):
                cur = u[bi * t:(bi + 1) * t]
                ext_ref[SUBLANE - (FFN_CONV - 1):SUBLANE, :] = st_ref[bi]
                ext_ref[SUBLANE:, :] = cur
                outs.append(_ffn_conv(cur, ext_ref, cw_ref, b_ref, t))
            return jnp.concatenate(outs, axis=0)

        gts = conv(us, sg_ref, cwg_ref, bg_ref)
        vls = conv(vs, sv_ref, cwv_ref, bv_ref)
        acts_ref[...] = (_silu(gts) * vls).astype(acts_ref.dtype)

    @pl.when(mi % tiles_per_seq == 0)
    def _():
        halo_ref[...] = jnp.zeros(halo_ref.shape, F32)

    halo_g = halo_ref[0]
    halo_v = halo_ref[1]
    ug = []
    uv = []
    for s in range(FFN_ROW_SPLIT):
        x = xp_ref[s * rows:(s + 1) * rows, :]
        ug.append(_dot(x, wgbf_ref[...]))
        uv.append(_dot(x, wvbf_ref[...]))
    for s in range(FFN_ROW_SPLIT):
        gt = _ffn_conv_rows(ug[s], halo_g, cwg_ref, bg_ref)
        vl = _ffn_conv_rows(uv[s], halo_v, cwv_ref, bv_ref)
        actp_ref[s * rows:(s + 1) * rows, :] = (_silu(gt) * vl).astype(actp_ref.dtype)
        halo_g = ug[s][rows - SUBLANE:]
        halo_v = uv[s][rows - SUBLANE:]
    halo_ref[0] = halo_g
    halo_ref[1] = halo_v
    tg_ref[0] = halo_g
    tv_ref[0] = halo_v


def ffn_up(xn_p, xn_s, w_up, conv_w, conv_b, conv_state, w_down, layer, seq, t):
    mp, k = xn_p.shape
    ms = xn_s.shape[0]
    d_ff = w_up.shape[2] // 2
    tn = FFN_TN
    nt = d_ff // tn
    tm = min(seq, MM_TM)
    mt = mp // tm
    nb = ms // t
    n_down = w_down.shape[2]
    lo = lambda j, i: (layer, 0, j)
    hi = lambda j, i: (layer, 0, nt + j)
    vmem = (2 * (tm * k * 2 + ms * k * 2 + 2 * k * tn * 4 + tm * tn * 2 + tn * n_down * 6)
            + 2 * k * tn * 2 + 8 * tm * tn * 4)
    return pl.pallas_call(
        functools.partial(_ffn_up_kernel, tiles_per_seq=seq // tm, t=t),
        grid=(nt, mt),
        in_specs=[pl.BlockSpec((tm, k), lambda j, i: (i, 0)),
                  pl.BlockSpec((ms, k), lambda j, i: (0, 0)),
                  pl.BlockSpec((None, k, tn), lo), pl.BlockSpec((None, k, tn), hi),
                  pl.BlockSpec((None, FFN_CONV, tn), lo), pl.BlockSpec((None, FFN_CONV, tn), hi),
                  pl.BlockSpec((None, 1, tn), lo), pl.BlockSpec((None, 1, tn), hi),
                  pl.BlockSpec((None, nb, FFN_CONV - 1, tn), lambda j, i: (layer, 0, 0, j)),
                  pl.BlockSpec((None, nb, FFN_CONV - 1, tn), lambda j, i: (layer, 0, 0, nt + j)),
                  pl.BlockSpec((None, tn, n_down), lambda j, i: (layer, j, 0))],
        out_specs=[pl.BlockSpec((tm, tn), lambda j, i: (i, j)),
                   pl.BlockSpec((1, SUBLANE, tn), lambda j, i: (i, 0, j)),
                   pl.BlockSpec((1, SUBLANE, tn), lambda j, i: (i, 0, j)),
                   pl.BlockSpec((ms, tn), lambda j, i: (0, j)),
                   pl.BlockSpec((ms, tn), lambda j, i: (0, j)),
                   pl.BlockSpec((ms, tn), lambda j, i: (0, j)),
                   pl.BlockSpec((tn, n_down), lambda j, i: (j, 0))],
        out_shape=[jax.ShapeDtypeStruct((mp, d_ff), BF16),
                   jax.ShapeDtypeStruct((mt, SUBLANE, d_ff), F32),
                   jax.ShapeDtypeStruct((mt, SUBLANE, d_ff), F32),
                   jax.ShapeDtypeStruct((ms, d_ff), BF16),
                   jax.ShapeDtypeStruct((ms, d_ff), F32),
                   jax.ShapeDtypeStruct((ms, d_ff), F32),
                   jax.ShapeDtypeStruct((d_ff, n_down), BF16)],
        scratch_shapes=[pltpu.VMEM((k, tn), BF16), pltpu.VMEM((k, tn), BF16),
                        pltpu.VMEM((2, SUBLANE, tn), F32), pltpu.VMEM((SUBLANE + t, tn), F32)],
        compiler_params=_cparams(("arbitrary", "arbitrary"), vmem),
        name="ffn_up",
    )(xn_p, xn_s, w_up, w_up, conv_w, conv_w, conv_b, conv_b, conv_state, conv_state, w_down)


def _ffn_down_kernel(a_ref, w_ref, h_ref, o_ref):
    o_ref[...] = h_ref[...] + _dot(a_ref[...], w_ref[...])


def ffn_down_proj(act, w_down, h):
    m, k = act.shape
    n = w_down.shape[1]
    tm = min(m, 512)
    tn = 512
    vmem = 2 * (tm * k * 2 + k * tn * 2 + 2 * tm * tn * 4)
    return pl.pallas_call(
        _ffn_down_kernel,
        grid=(m // tm, n // tn),
        in_specs=[pl.BlockSpec((tm, k), lambda i, j: (i, 0)),
                  pl.BlockSpec((k, tn), lambda i, j: (0, j)),
                  pl.BlockSpec((tm, tn), lambda i, j: (i, j))],
        out_specs=pl.BlockSpec((tm, tn), lambda i, j: (i, j)),
        out_shape=jax.ShapeDtypeStruct((m, n), F32),
        compiler_params=_cparams(("parallel", "parallel"), vmem),
        name="ffn_down",
    )(act, w_down, h)


def _t5_bucket(dist):
    max_exact = REL_BUCKETS // 2
    d = jnp.maximum(dist, 1).astype(F32)
    large = max_exact + (jnp.log(d / max_exact) / math.log(REL_MAX_DIST / max_exact)
                         * (REL_BUCKETS - max_exact)).astype(jnp.int32)
    large = jnp.minimum(large, REL_BUCKETS - 1)
    return jnp.where(dist < max_exact, dist, large)


def _bias_lookup(rel_bias, buckets):
    onehot = jax.nn.one_hot(buckets, REL_BUCKETS, dtype=F32)
    return jnp.einsum('...k,kh->...h', onehot, rel_bias.astype(F32), precision=HIGHEST)


def _prompt_bias_blocks(rel_bias):
    qi = np.arange(BLK)[:, None]
    ki = np.arange(2 * BLK)[None, :]
    rel = qi + BLK - ki
    out = []
    for window, dil in A_BRANCHES:
        nj = window // dil + 1
        valid = (rel >= 0) & (rel < nj)
        buckets = _t5_bucket(jnp.asarray(np.clip(rel, 0, nj - 1) * dil, jnp.int32))
        bias = _bias_lookup(rel_bias, buckets)
        out.append(jnp.where(jnp.asarray(valid)[..., None], bias, NEG_INF).transpose(2, 0, 1))
    return jnp.stack(out)


def _branch_count(dist):
    cnt = np.zeros(dist.shape, np.float32)
    for window, dil in A_BRANCHES:
        cnt += ((dist >= 0) & (dist % dil == 0) & (dist // dil <= window // dil)).astype(np.float32)
    return cnt


def _sample_bias_tables(rel_bias, t, n_past):
    tq = np.arange(t)[:, None]
    d_cache = n_past + tq - np.arange(n_past)[None, :]
    d_new = tq - np.arange(BLK)[None, :]
    d_new = np.where(np.arange(BLK)[None, :] < t, d_new, -1)
    tables = []
    for dist in (d_cache, d_new):
        cnt = _branch_count(dist)
        bias = _bias_lookup(rel_bias, _t5_bucket(jnp.asarray(np.maximum(dist, 0), jnp.int32)))
        bias = jnp.where(jnp.asarray(cnt > 0)[..., None], bias, NEG_INF).transpose(2, 0, 1)
        tables += [bias, jnp.asarray(cnt)]
    return tables


def _mixers(proj, ba, lw, *, batch, seq, layer, cache_hk=None, pool_buf=None, gconv_buf=None,
            gstate=None, bias_tabs=None):
    sample = cache_hk is not None
    mix_dtype = F32 if sample else BF16
    proj3 = proj.reshape(batch, seq, N_MAIN)
    bag = gdn_group_tail(ba, batch, seq)
    qg = lw['a_q_norm'].reshape(1, DH)
    kg = lw['a_k_norm'].reshape(1, DH)
    og = lw['a_out_norm'].reshape(H_A, DH)
    if sample:
        ya, kv = attn_sample(proj3, cache_hk, layer, qg, kg, og, *bias_tabs)
    else:
        ya, kv = attn_prompt(proj3, qg, kg, og, bias_tabs)
    kv_rows = kv.transpose(0, 1, 3, 2, 4)
    yb = pool_mixer(proj3, pool_buf, lw['pool_w'], lw['pool_scale'].reshape(1, D_B), mix_dtype)
    pu = proj3[:, :, OFF_P:OFF_P + D_B]
    if sample:
        pool_new = jnp.concatenate([pool_buf, pu], axis=1)[:, -POOL_BUF:]
    else:
        pool_new = pu[:, -POOL_BUF:]
    yc, gstate_new = gdn_mixer(proj3, bag, lw['gdn_conv_w'], lw['alog_vec'], lw['dt_vec'],
                               lw['gdn_out_norm'].reshape(1, DH), gconv_buf, gstate, mix_dtype)
    gconv_new = proj3[:, -(GDN_CONV - 1):, OFF_C:OFF_C + 3 * D_C]
    m = batch * seq
    mix = (ya.reshape(m, D_A).astype(BF16), yb.reshape(m, D_B).astype(BF16),
           yc.reshape(m, D_C).astype(BF16))
    return mix, (kv_rows, pool_new, gconv_new, gstate_new)


def _layer(xp, xs, lw, gw, *, layer, prompt_shape, sample_shape, cache_hk, pool_buf, gconv_buf, gstate,
           fconv_state, prompt_bias, sample_tabs):
    bp, sp = prompt_shape
    bs, ss = sample_shape
    xnp = rmsnorm_cast(xp, lw['norm_mix'])
    xns = rmsnorm_cast(xs, lw['norm_mix'])
    proj_p, proj_s = proj_in(xnp, xns, gw['w_in_t'], layer)
    ba_p = proj_tail(xnp, gw['w_in_t'], layer)
    ba_s = proj_tail(xns, gw['w_in_t'], layer)
    mix_p, outs_p = _mixers(proj_p, ba_p, lw, batch=bp, seq=sp, layer=layer, bias_tabs=prompt_bias)
    mix_s, outs_s = _mixers(proj_s, ba_s, lw, batch=bs, seq=ss, layer=layer, cache_hk=cache_hk,
                            pool_buf=pool_buf, gconv_buf=gconv_buf, gstate=gstate, bias_tabs=sample_tabs)
    hp, hs = proj_out(mix_p, mix_s, gw['w_out'], layer, xp, xs)
    hnp = rmsnorm_cast(hp, lw['norm_ffn'])
    hns = rmsnorm_cast(hs, lw['norm_ffn'])
    act_p, tg, tv, act_s, ug, uv, w_down16 = ffn_up(hnp, hns, gw['ffn_up'], gw['ffn_conv_w'],
                                                    gw['ffn_conv_b'], fconv_state, gw['ffn_down'],
                                                    layer, sp, ss)
    tails = jnp.concatenate([tg, tv], axis=-1)
    tiles_per_seq = tails.shape[0] // bp
    fconv_p = tails[tiles_per_seq - 1::tiles_per_seq, -(FFN_CONV - 1):]
    up_s = jnp.concatenate([ug, uv], axis=-1).reshape(bs, ss, -1)
    fconv_s = up_s[:, -(FFN_CONV - 1):]
    yp = ffn_down_proj(act_p, w_down16, hp)
    ys = ffn_down_proj(act_s, w_down16, hs)
    return yp, ys, outs_p + (fconv_p,), outs_s + (fconv_s,)


def kernel(x_prompt, x_sample, cache_attn_kv, state_pool, state_gdn_conv, state_gdn, state_ffn_conv,
           rel_bias, norm_mix, w_in, a_q_norm, a_k_norm, a_out_norm, pool_w, pool_scale,
           gdn_conv_w, gdn_a_log, gdn_dt_bias, gdn_out_norm, w_out, norm_ffn,
           ffn_up, ffn_conv_w, ffn_conv_b, ffn_down):
    depth = w_in.shape[0]
    bp, sp, d_model = x_prompt.shape
    bs, ss, _ = x_sample.shape
    n_past = cache_attn_kv.shape[2]
    assert w_in.shape[2] == N_MAIN + N_TAIL and w_out.shape[1] == D_A + D_B + D_C

    prompt_bias = _prompt_bias_blocks(rel_bias)
    sample_tabs = _sample_bias_tables(rel_bias, ss, n_past)
    cache_hk = jnp.swapaxes(cache_attn_kv, 3, 4).reshape(depth, bs, n_past * H_A * 2, DH)

    xp = x_prompt.reshape(bp * sp, d_model)
    xs = x_sample.reshape(bs * ss, d_model)
    p_out = [[] for _ in range(5)]
    s_out = [[] for _ in range(5)]
    gw = {
        'w_in_t': jnp.swapaxes(w_in, 1, 2),
        'w_out': w_out, 'ffn_up': ffn_up, 'ffn_down': ffn_down,
        'ffn_conv_w': ffn_conv_w, 'ffn_conv_b': ffn_conv_b.reshape(depth, 1, -1),
    }
    for l in range(depth):
        lw = {
            'norm_mix': norm_mix[l], 'norm_ffn': norm_ffn[l],
            'a_q_norm': a_q_norm[l], 'a_k_norm': a_k_norm[l], 'a_out_norm': a_out_norm[l],
            'pool_w': pool_w[l].astype(BF16), 'pool_scale': pool_scale[l],
            'gdn_conv_w': gdn_conv_w[l],
            'alog_vec': gdn_group_vec(gdn_a_log[l]), 'dt_vec': gdn_group_vec(gdn_dt_bias[l]),
            'gdn_out_norm': gdn_out_norm[l],
        }
        xp, xs, outs_p, outs_s = _layer(
            xp, xs, lw, gw, layer=l, prompt_shape=(bp, sp), sample_shape=(bs, ss), cache_hk=cache_hk,
            pool_buf=state_pool[l], gconv_buf=state_gdn_conv[l], gstate=state_gdn[l],
            fconv_state=state_ffn_conv, prompt_bias=prompt_bias, sample_tabs=sample_tabs)
        for acc, o in zip(p_out, outs_p):
            acc.append(o)
        for acc, o in zip(s_out, outs_s):
            acc.append(o)
    res = [xp.reshape(bp, sp, d_model), xs.reshape(bs, ss, d_model)]
    for po, so in zip(p_out, s_out):
        res += [jnp.stack(po), jnp.stack(so)]
    return tuple(res)
```

```python
import functools
import math

import numpy as np
import jax
import jax.numpy as jnp
from jax import lax
from jax.experimental import pallas as pl
from jax.experimental.pallas import tpu as pltpu

F32 = jnp.float32
BF16 = jnp.bfloat16
HIGHEST = lax.Precision.HIGHEST

DH = 128
H_A = 12
H_C = 12
A_BRANCHES = ((128, 1), (512, 4), (2048, 16))
BLK = 128
REL_BUCKETS = 32
REL_MAX_DIST = 2048
POOL_WINDOWS = (2, 4, 8, 16)
CG = 256
POOL_BUF = 15
GDN_CONV = 4
GDN_CHUNK = 64
GDN_SUB = 16
FFN_CONV = 3
EPS = 1e-6
NEG_INF = -1e30

D_A = H_A * DH
D_B = len(POOL_WINDOWS) * CG
D_C = H_C * DH
OFF_Q, OFF_K, OFF_V = 0, D_A, 2 * D_A
OFF_P = 3 * D_A
OFF_C = OFF_P + D_B
OFF_G = OFF_C + 3 * D_C
N_MAIN = OFF_G + D_C
N_TAIL = 2 * H_C

LANE = 128
SUBLANE = 8
VMEM_CAP = 56 * 1024 * 1024


def _cparams(sem, vmem_bytes):
    limit = int(min(max(vmem_bytes * 5 // 4 + (2 << 20), 16 << 20), VMEM_CAP))
    return pltpu.CompilerParams(dimension_semantics=sem, vmem_limit_bytes=limit)


def _rms_rows(x):
    return x * lax.rsqrt(jnp.mean(x * x, axis=-1, keepdims=True) + EPS)


def _silu(x):
    return x * jax.nn.sigmoid(x)


def _dot(a, b):
    return jnp.dot(a, b, preferred_element_type=F32)


def _dot_nt(a, b):
    return lax.dot_general(a, b, (((1,), (1,)), ((), ())), preferred_element_type=F32)


def _hdot(a, b):
    return jnp.dot(a, b, preferred_element_type=F32, precision=HIGHEST)


def _hdot_nt(a, b):
    return lax.dot_general(a, b, (((1,), (1,)), ((), ())), preferred_element_type=F32,
                           precision=HIGHEST)


def _hdot_tn(a, b):
    return lax.dot_general(a, b, (((0,), (0,)), ((), ())), preferred_element_type=F32,
                           precision=HIGHEST)


def _rmsnorm_kernel(x_ref, g_ref, o_ref):
    o_ref[...] = (_rms_rows(x_ref[...]) * g_ref[...]).astype(o_ref.dtype)


def rmsnorm_cast(x, gain):
    m, d = x.shape
    tm = min(m, 256)
    return pl.pallas_call(
        _rmsnorm_kernel,
        grid=(m // tm,),
        in_specs=[pl.BlockSpec((tm, d), lambda i: (i, 0)),
                  pl.BlockSpec((1, d), lambda i: (0, 0))],
        out_specs=pl.BlockSpec((tm, d), lambda i: (i, 0)),
        out_shape=jax.ShapeDtypeStruct((m, d), BF16),
        compiler_params=_cparams(("parallel",), 2 * tm * d * 6),
        name="rmsnorm_cast",
    )(x, gain.reshape(1, d))


MM_TM = 1024
MM_TN = 512


def _proj_in_kernel(xp_ref, xs_ref, w_ref, op_ref, os_ref, wbf_ref):
    @pl.when(pl.program_id(1) == 0)
    def _():
        wbf_ref[...] = w_ref[...].astype(BF16)
        os_ref[...] = _dot_nt(xs_ref[...], wbf_ref[...])

    op_ref[...] = _dot_nt(xp_ref[...], wbf_ref[...])


def proj_in(xn_p, xn_s, w_t, layer):
    mp, k = xn_p.shape
    ms = xn_s.shape[0]
    tm, tn = MM_TM, MM_TN
    vmem = 2 * (tm * k * 2 + ms * k * 2 + k * tn * 4 + tm * tn * 4 + ms * tn * 4) + 3 * k * tn * 2
    return pl.pallas_call(
        _proj_in_kernel,
        grid=(N_MAIN // tn, mp // tm),
        in_specs=[pl.BlockSpec((tm, k), lambda j, i: (i, 0)),
                  pl.BlockSpec((ms, k), lambda j, i: (0, 0)),
                  pl.BlockSpec((None, tn, k), lambda j, i: (layer, j, 0))],
        out_specs=[pl.BlockSpec((tm, tn), lambda j, i: (i, j)),
                   pl.BlockSpec((ms, tn), lambda j, i: (0, j))],
        out_shape=[jax.ShapeDtypeStruct((mp, N_MAIN), F32),
                   jax.ShapeDtypeStruct((ms, N_MAIN), F32)],
        scratch_shapes=[pltpu.VMEM((tn, k), BF16)],
        compiler_params=_cparams(("arbitrary", "arbitrary"), vmem),
        name="proj_in",
    )(xn_p, xn_s, w_t)


def _proj_tail_kernel(x_ref, *refs):
    w_refs, o_ref = refs[:-1], refs[-1]
    k = x_ref.shape[1]
    pad = jnp.zeros((LANE - SUBLANE * len(w_refs), k), F32)
    w = jnp.concatenate([r[...] for r in w_refs] + [pad], axis=0).astype(BF16)
    o_ref[...] = _dot_nt(x_ref[...], w)


def proj_tail(xn, w_t, layer):
    m, k = xn.shape
    tm = min(m, MM_TM)
    assert N_TAIL % SUBLANE == 0 and N_MAIN % SUBLANE == 0
    nblk = N_TAIL // SUBLANE
    vmem = 2 * (tm * k * 2 + nblk * SUBLANE * k * 4 + tm * LANE * 4) + 2 * LANE * k * 4
    return pl.pallas_call(
        _proj_tail_kernel,
        grid=(m // tm,),
        in_specs=[pl.BlockSpec((tm, k), lambda i: (i, 0))]
        + [pl.BlockSpec((None, SUBLANE, k), (lambda i, r=r: (layer, N_MAIN // SUBLANE + r, 0)))
           for r in range(nblk)],
        out_specs=pl.BlockSpec((tm, LANE), lambda i: (i, 0)),
        out_shape=jax.ShapeDtypeStruct((m, LANE), F32),
        compiler_params=_cparams(("parallel",), vmem),
        name="proj_tail",
    )(xn, *([w_t] * nblk))


ATTN_BLOCKS_PER_TRIP = 4


def _strided_rows(start, dil):
    return pl.ds(start, BLK) if dil == 1 else pl.ds(start, BLK, stride=dil)


def _attn_prompt_kernel(q_ref, k_ref, v_ref, qg_ref, kg_ref, og_ref, bias_ref,
                        y_ref, kv_ref, qs_ref, ks_ref, acc_ref, m_ref, l_ref):
    seq = q_ref.shape[1]
    qs_ref[...] = _rms_rows(q_ref[0]) * qg_ref[...] * (DH ** -0.5)
    ks_ref[...] = _rms_rows(k_ref[0]) * kg_ref[...]
    kv_ref[0, :, 0, 0, :] = ks_ref[...]
    kv_ref[0, :, 0, 1, :] = v_ref[0]
    acc_ref[...] = jnp.zeros(acc_ref.shape, F32)
    l_ref[...] = jnp.zeros(l_ref.shape, F32)
    m_ref[...] = jnp.full(m_ref.shape, NEG_INF, F32)

    for bi, (window, dil) in enumerate(A_BRANCHES):
        assert window // dil == BLK and seq % (dil * BLK) == 0
        nb = seq // (dil * BLK)

        assert (dil * nb) % ATTN_BLOCKS_PER_TRIP == 0

        def body(it, carry, bi=bi, dil=dil, nb=nb):
            us = range(ATTN_BLOCKS_PER_TRIP)
            item = [it * ATTN_BLOCKS_PER_TRIP + u for u in us]
            r = [x // nb for x in item]
            n = [item[u] - r[u] * nb for u in us]
            rows = [_strided_rows(n[u] * (BLK * dil) + r[u], dil) for u in us]
            qb = [qs_ref[rw, :].astype(BF16) for rw in rows]
            kc = [ks_ref[rw, :] for rw in rows]
            vc = [v_ref[0, rw, :] for rw in rows]
            bias_cur = bias_ref[bi, 0, :, BLK:]
            if nb > 1:
                prows = [_strided_rows(jnp.maximum(n[u] - 1, 0) * (BLK * dil) + r[u], dil) for u in us]
                kk = [jnp.concatenate([ks_ref[prows[u], :], kc[u]], axis=0).astype(BF16) for u in us]
                vv = [jnp.concatenate([v_ref[0, prows[u], :], vc[u]], axis=0).astype(BF16) for u in us]
                bias_prev = bias_ref[bi, 0, :, :BLK]
                bias = [jnp.concatenate([jnp.where(n[u] > 0, bias_prev, NEG_INF), bias_cur], axis=1)
                        for u in us]
            else:
                kk = [x.astype(BF16) for x in kc]
                vv = [x.astype(BF16) for x in vc]
                bias = [bias_cur for _ in us]
            s = [_dot_nt(qb[u], kk[u]) + bias[u] for u in us]
            m_old = [m_ref[rw, :] for rw in rows]
            m_new = [jnp.maximum(m_old[u], jnp.max(s[u], axis=1, keepdims=True)) for u in us]
            p = [jnp.exp(s[u] - m_new[u][:, :1]) for u in us]
            alpha = [jnp.exp(m_old[u] - m_new[u]) for u in us]
            pv = [_dot(p[u].astype(BF16), vv[u]) for u in us]
            for u in us:
                l_ref[rows[u], :] = alpha[u] * l_ref[rows[u], :] + jnp.sum(p[u], axis=1, keepdims=True)
                acc_ref[rows[u], :] = alpha[u] * acc_ref[rows[u], :] + pv[u]
                m_ref[rows[u], :] = m_new[u]
            return carry

        lax.fori_loop(0, dil * nb // ATTN_BLOCKS_PER_TRIP, body, 0)

    o = acc_ref[...] / l_ref[...]
    y_ref[0] = (_rms_rows(o) * og_ref[0]).astype(y_ref.dtype)


def attn_prompt(proj3, q_gain, k_gain, o_gain, bias_blocks):
    b, s, _ = proj3.shape
    col = lambda off: (lambda i, h: (i, 0, off // DH + h))
    vmem = 2 * (3 * s * DH * 4 + s * DH * 2 + s * DH * 4 + 3 * BLK * 2 * BLK * 4) + 5 * s * DH * 4
    return pl.pallas_call(
        _attn_prompt_kernel,
        grid=(b, H_A),
        in_specs=[pl.BlockSpec((1, s, DH), col(OFF_Q)),
                  pl.BlockSpec((1, s, DH), col(OFF_K)),
                  pl.BlockSpec((1, s, DH), col(OFF_V)),
                  pl.BlockSpec((1, DH), lambda i, h: (0, 0)),
                  pl.BlockSpec((1, DH), lambda i, h: (0, 0)),
                  pl.BlockSpec((1, 1, DH), lambda i, h: (h, 0, 0)),
                  pl.BlockSpec((len(A_BRANCHES), 1, BLK, 2 * BLK), lambda i, h: (0, h, 0, 0))],
        out_specs=[pl.BlockSpec((1, s, DH), lambda i, h: (i, 0, h)),
                   pl.BlockSpec((1, s, 1, 2, DH), lambda i, h: (i, 0, h, 0, 0))],
        out_shape=[jax.ShapeDtypeStruct((b, s, D_A), BF16),
                   jax.ShapeDtypeStruct((b, s, H_A, 2, DH), F32)],
        scratch_shapes=[pltpu.VMEM((s, DH), F32)] * 5,
        compiler_params=_cparams(("parallel", "parallel"), vmem),
        name="attn_prompt",
    )(proj3, proj3, proj3, q_gain, k_gain, o_gain.reshape(H_A, 1, DH), bias_blocks)


ATTN_SAMPLE_TK = 256


def _attn_sample_kernel(q_ref, k_ref, v_ref, c_ref, qg_ref, kg_ref, og_ref,
                        bc_ref, cc_ref, bn_ref, cn_ref, y_ref, kv_ref, qn_ref, m_ref, l_ref, acc_ref):
    t = q_ref.shape[1]
    tk = c_ref.shape[2] // (2 * H_A)
    kt = pl.program_id(1)
    heads = range(H_A)
    cols = [slice(h * DH, (h + 1) * DH) for h in heads]

    @pl.when(kt == 0)
    def _():
        pad = jnp.zeros((BLK - t, DH), F32)
        for h in heads:
            qn = _rms_rows(q_ref[0, :, cols[h]]) * qg_ref[...] * (DH ** -0.5)
            kn = _rms_rows(k_ref[0, :, cols[h]]) * kg_ref[...]
            vn = v_ref[0, :, cols[h]]
            qn_ref[:, cols[h]] = qn
            kv_ref[0, :, h, 0, :] = kn
            kv_ref[0, :, h, 1, :] = vn
            kpad = jnp.concatenate([kn, pad], axis=0).astype(BF16)
            vpad = jnp.concatenate([vn, pad], axis=0).astype(BF16)
            s = _dot_nt(qn.astype(BF16), kpad) + bn_ref[h]
            m = jnp.max(s, axis=1, keepdims=True)
            p = jnp.exp(s - m) * cn_ref[...]
            m_ref[h] = jnp.broadcast_to(m, (t, DH))
            l_ref[h] = jnp.broadcast_to(jnp.sum(p, axis=1, keepdims=True), (t, DH))
            acc_ref[h] = _dot(p.astype(BF16), vpad)

    q16 = [qn_ref[:, cols[h]].astype(BF16) for h in heads]
    kc = [c_ref[0, 0, pl.ds(2 * h, tk, stride=2 * H_A), :].astype(BF16) for h in heads]
    vc = [c_ref[0, 0, pl.ds(2 * h + 1, tk, stride=2 * H_A), :].astype(BF16) for h in heads]
    s = [_dot_nt(q16[h], kc[h]) + bc_ref[h] for h in heads]
    m_old = [m_ref[h] for h in heads]
    m_new = [jnp.maximum(m_old[h], jnp.max(s[h], axis=1, keepdims=True)) for h in heads]
    p = [jnp.exp(s[h] - m_new[h][:, :1]) * cc_ref[...] for h in heads]
    alpha = [jnp.exp(m_old[h] - m_new[h]) for h in heads]
    pv = [_dot(p[h].astype(BF16), vc[h]) for h in heads]
    for h in heads:
        l_ref[h] = alpha[h] * l_ref[h] + jnp.sum(p[h], axis=1, keepdims=True)
        acc_ref[h] = alpha[h] * acc_ref[h] + pv[h]
        m_ref[h] = m_new[h]

    @pl.when(kt == pl.num_programs(1) - 1)
    def _():
        for h in heads:
            o = acc_ref[h] / l_ref[h]
            y_ref[0, :, cols[h]] = _rms_rows(o) * og_ref[:, cols[h]]


def attn_sample(proj3, cache_rows, layer, q_gain, k_gain, o_gain, bias_c, cnt_c, bias_n, cnt_n):
    b, t, _ = proj3.shape
    n_past = cache_rows.shape[2] // (2 * H_A)
    tk = ATTN_SAMPLE_TK
    col = lambda off: (lambda i, j: (i, 0, off // D_A))
    vmem = (2 * (tk * 2 * H_A * DH * 4 + 3 * t * D_A * 4 + H_A * t * tk * 4 + t * tk * 4
                 + 2 * t * 2 * D_A * 4) + 4 * H_A * t * DH * 4 + (2 << 20))
    return pl.pallas_call(
        _attn_sample_kernel,
        grid=(b, n_past // tk),
        in_specs=[pl.BlockSpec((1, t, D_A), col(OFF_Q)),
                  pl.BlockSpec((1, t, D_A), col(OFF_K)),
                  pl.BlockSpec((1, t, D_A), col(OFF_V)),
                  pl.BlockSpec((1, 1, tk * 2 * H_A, DH), lambda i, j: (layer, i, j, 0)),
                  pl.BlockSpec((1, DH), lambda i, j: (0, 0)),
                  pl.BlockSpec((1, DH), lambda i, j: (0, 0)),
                  pl.BlockSpec((1, D_A), lambda i, j: (0, 0)),
                  pl.BlockSpec((H_A, t, tk), lambda i, j: (0, 0, j)),
                  pl.BlockSpec((t, tk), lambda i, j: (0, j)),
                  pl.BlockSpec((H_A, t, BLK), lambda i, j: (0, 0, 0)),
                  pl.BlockSpec((t, BLK), lambda i, j: (0, 0))],
        out_specs=[pl.BlockSpec((1, t, D_A), lambda i, j: (i, 0, 0)),
                   pl.BlockSpec((1, t, H_A, 2, DH), lambda i, j: (i, 0, 0, 0, 0))],
        out_shape=[jax.ShapeDtypeStruct((b, t, D_A), F32),
                   jax.ShapeDtypeStruct((b, t, H_A, 2, DH), F32)],
        scratch_shapes=[pltpu.VMEM((t, D_A), F32)] + [pltpu.VMEM((H_A, t, DH), F32)] * 3,
        compiler_params=_cparams(("parallel", "arbitrary"), vmem),
        name="attn_sample",
    )(proj3, proj3, proj3, cache_rows, q_gain, k_gain, o_gain.reshape(1, D_A),
      bias_c, cnt_c, bias_n, cnt_n)


POOL_HALO = 16


def _pool_kernel(*refs, t, rows, n_valid, has_buf):
    ng = len(POOL_WINDOWS)
    u_refs = refs[:ng]
    refs = refs[ng:]
    if has_buf:
        buf_refs = refs[:ng]
        refs = refs[ng:]
    w_ref, scale_ref, y_ref, ext_ref = refs
    for g, win in enumerate(POOL_WINDOWS):
        ext_ref[0:POOL_HALO, :] = jnp.zeros((POOL_HALO, CG), F32)
        if has_buf:
            ext_ref[POOL_HALO - POOL_BUF:POOL_HALO, :] = buf_refs[g][0]
        ext_ref[POOL_HALO:, :] = u_refs[g][0]

        def chunk(ci, carry, g=g, win=win):
            base = pl.multiple_of(ci * rows, SUBLANE)
            w = ext_ref[pl.ds(base, rows + POOL_HALO), :]
            u = w[POOL_HALO:]
            assert win & (win - 1) == 0
            acc = w
            k = 1
            while k < win:
                acc = acc[:acc.shape[0] - k] + acc[k:]
                k *= 2
            tot = acc[POOL_HALO + 1 - win:POOL_HALO + 1 - win + rows]
            pos = base + lax.broadcasted_iota(jnp.int32, (rows, 1), 0)
            cnt = jnp.minimum(win, n_valid + pos + 1).astype(F32)
            d = tot / cnt - u
            y = _rms_rows(_dot(d.astype(BF16), w_ref[g])) * scale_ref[:, g * CG:(g + 1) * CG]
            y_ref[0, pl.ds(base, rows), g * CG:(g + 1) * CG] = y.astype(y_ref.dtype)
            return carry

        lax.fori_loop(0, t // rows, chunk, 0)


def pool_mixer(proj3, bufs, w_pool, scale, out_dtype):
    b, t, _ = proj3.shape
    ng = len(POOL_WINDOWS)
    rows = min(t, 256)
    has_buf = bufs is not None
    in_specs = [pl.BlockSpec((1, t, CG), (lambda i, g=g: (i, 0, OFF_P // CG + g))) for g in range(ng)]
    args = [proj3] * ng
    if has_buf:
        in_specs += [pl.BlockSpec((1, POOL_BUF, CG), (lambda i, g=g: (i, 0, g))) for g in range(ng)]
        args += [bufs] * ng
    in_specs += [pl.BlockSpec((ng, CG, CG), lambda i: (0, 0, 0)),
                 pl.BlockSpec((1, D_B), lambda i: (0, 0))]
    args += [w_pool, scale]
    vmem = 2 * (ng * t * CG * 4 + t * D_B * 4 + ng * CG * CG * 2) + (t + POOL_HALO) * CG * 4
    return pl.pallas_call(
        functools.partial(_pool_kernel, t=t, rows=rows, n_valid=POOL_BUF if has_buf else 0,
                          has_buf=has_buf),
        grid=(b,),
        in_specs=in_specs,
        out_specs=pl.BlockSpec((1, t, D_B), lambda i: (i, 0, 0)),
        out_shape=jax.ShapeDtypeStruct((b, t, D_B), out_dtype),
        scratch_shapes=[pltpu.VMEM((t + POOL_HALO, CG), F32)],
        compiler_params=_cparams(("parallel",), vmem),
        name="pool_mixer",
    )(*args)


GDN_HEADS_PER_STEP = 4
GDN_CHUNKS_PER_TRIP = 4


def _split_bf16(a):
    hi = a.astype(BF16)
    return hi, (a - hi.astype(F32)).astype(BF16)


def _dot3(a, b):
    return _dot(a[0], b[0]) + (_dot(a[0], b[1]) + _dot(a[1], b[0]))


def _cumsum_rows(x):
    n = x.shape[0]
    row = lax.broadcasted_iota(jnp.int32, (n, 1), 0)
    k = 1
    while k < n:
        x = x + jnp.where(row >= k, pltpu.roll(x, k, axis=0), 0.0)
        k *= 2
    return x


def _unit_lower_solve(mats, rhss, eye, sub_diag):
    assert GDN_CHUNK // GDN_SUB == 4 and GDN_SUB == 16
    idx = range(len(mats))
    d = [jnp.where(sub_diag, m, 0.0) for m in mats]
    low = [_split_bf16(m - di) for m, di in zip(mats, d)]
    rs = [_split_bf16(r) for r in rhss]
    x = [eye - di for di in d]
    ps = [_split_bf16(di) for di in d]
    ps = [_split_bf16(_dot3(ps[i], ps[i])) for i in idx]
    for _ in range(2):
        xs = [_split_bf16(xi) for xi in x]
        x = [x[i] + _dot3(xs[i], ps[i]) for i in idx]
        ps = [_split_bf16(_dot3(ps[i], ps[i])) for i in idx]
    xs = [_split_bf16(xi) for xi in x]
    x = [x[i] + _dot3(xs[i], ps[i]) for i in idx]
    xs = [_split_bf16(xi) for xi in x]
    n = [_dot3(xs[i], low[i]) for i in idx]
    xr = [_dot3(xs[i], rs[i]) for i in idx]
    ns = [_split_bf16(ni) for ni in n]
    n2 = [_split_bf16(_dot3(ns[i], ns[i])) for i in idx]
    imn = [eye - ni for ni in n]
    y = [imn[i] + _dot3(_split_bf16(imn[i]), n2[i]) for i in idx]
    return [_dot3(_split_bf16(y[i]), _split_bf16(xr[i])) for i in idx]


def _gdn_kernel(*refs, t, has_state):
    c = GDN_CHUNK
    hg = GDN_HEADS_PER_STEP
    (q_ref, k_ref, v_ref, gate_ref, ba_ref, cwq_ref, cwk_ref, cwv_ref,
     alog_ref, dt_ref, gain_ref) = refs[:11]
    refs = refs[11:]
    if has_state:
        cbq_ref, cbk_ref, cbv_ref, s0_ref = refs[:4]
        refs = refs[4:]
    y_ref, s_ref, cwq_rows, cwk_rows, cwv_rows = refs
    for rows_ref, cw_in in ((cwq_rows, cwq_ref), (cwk_rows, cwk_ref), (cwv_rows, cwv_ref)):
        for j in range(GDN_CONV):
            rows_ref[j] = jnp.broadcast_to(cw_in[j:j + 1, :], rows_ref.shape[1:])
    n_chunks = -(-t // c)
    padded = t % c != 0
    assert (not padded) or n_chunks == 1
    wid = hg * DH
    heads = range(hg)

    if has_state:
        s_ref[0] = s0_ref[0]
    else:
        s_ref[0] = jnp.zeros(s_ref.shape[1:], F32)

    ii = lax.broadcasted_iota(jnp.int32, (c, c), 0)
    jj = lax.broadcasted_iota(jnp.int32, (c, c), 1)
    tril = ii >= jj
    strict = ii > jj
    eye = (ii == jj).astype(F32)
    sub_diag = (ii // GDN_SUB) == (jj // GDN_SUB)
    row_id = lax.broadcasted_iota(jnp.int32, (c, 1), 0)

    def halo_rows(ref, cb_ref, ci):
        if n_chunks > 1:
            prev = ref[0, pl.ds(pl.multiple_of(jnp.maximum(ci * c - SUBLANE, 0), SUBLANE), SUBLANE), :]
        else:
            prev = jnp.zeros((SUBLANE, wid), F32)
        if has_state:
            pad = jnp.zeros((SUBLANE - (GDN_CONV - 1), wid), F32)
            first = jnp.concatenate([pad, cb_ref[0]], axis=0)
        else:
            first = jnp.zeros((SUBLANE, wid), F32)
        return jnp.where(ci > 0, prev, first)

    def conv_silu(ref, cb_ref, cw_ref, ci):
        if padded:
            cur = jnp.concatenate([ref[0], jnp.zeros((c - t, wid), F32)], axis=0)
        else:
            cur = ref[0, pl.ds(pl.multiple_of(ci * c, c), c), :]
        w = jnp.concatenate([halo_rows(ref, cb_ref, ci), cur], axis=0)
        out = cur * cw_ref[GDN_CONV - 1]
        for i in range(1, GDN_CONV):
            out = out + w[SUBLANE - i:SUBLANE - i + c] * cw_ref[GDN_CONV - 1 - i]
        return _silu(out)

    cpt = GDN_CHUNKS_PER_TRIP if n_chunks % GDN_CHUNKS_PER_TRIP == 0 else 1
    cols = [slice(i * DH, (i + 1) * DH) for i in heads]

    def body(it, carry):
        cb = (cbq_ref, cbk_ref, cbv_ref) if has_state else (None, None, None)
        q, k, v, gate, beta, gcum, g_last, decay = [], [], [], [], [], [], [], []
        for u in range(cpt):
            ci = it * cpt + u
            qa = conv_silu(q_ref, cb[0], cwq_rows, ci)
            ka = conv_silu(k_ref, cb[1], cwk_rows, ci)
            va = conv_silu(v_ref, cb[2], cwv_rows, ci)
            if padded:
                ba = jnp.concatenate([ba_ref[0, 0], jnp.zeros((c - t, LANE), F32)], axis=0)
                gate.append(jnp.concatenate([gate_ref[0], jnp.zeros((c - t, wid), F32)], axis=0))
                live = row_id < t
            else:
                ba = ba_ref[0, 0, pl.ds(pl.multiple_of(ci * c, c), c), :]
                gate.append(gate_ref[0, pl.ds(pl.multiple_of(ci * c, c), c), :])
            beta_all = jax.nn.sigmoid(ba)
            z = ba + dt_ref[0]
            softplus = jnp.maximum(z, 0.0) + jnp.log1p(jnp.exp(-jnp.abs(z)))
            g_all = -jnp.exp(alog_ref[0]) * softplus
            if padded:
                beta_all = jnp.where(live, beta_all, 0.0)
                g_all = jnp.where(live, g_all, 0.0)
            gcum_all = _cumsum_rows(g_all)
            gcum_t = gcum_all.T
            for i in heads:
                qi = qa[:, cols[i]]
                ki = ka[:, cols[i]]
                vi = va[:, cols[i]]
                qi = qi * lax.rsqrt(jnp.sum(qi * qi, axis=-1, keepdims=True) + EPS) * (DH ** -0.5)
                ki = ki * lax.rsqrt(jnp.sum(ki * ki, axis=-1, keepdims=True) + EPS)
                if padded:
                    qi = jnp.where(live, qi, 0.0)
                    ki = jnp.where(live, ki, 0.0)
                    vi = jnp.where(live, vi, 0.0)
                q.append(qi)
                k.append(ki)
                v.append(vi)
                beta.append(beta_all[:, i:i + 1])
                gc = gcum_all[:, hg + i:hg + i + 1]
                gcum.append(gc)
                g_last.append(gcum_all[c - 1:c, hg + i:hg + i + 1])
                decay.append(jnp.where(
                    tril, jnp.exp(jnp.where(tril, gc - gcum_t[hg + i:hg + i + 1, :], 0.0)), 0.0))
        chains = range(cpt * hg)
        e_cum = [jnp.exp(g) for g in gcum]
        kb = [k[n] * beta[n] for n in chains]
        k16 = [x.astype(BF16) for x in k]
        kk = [_dot_nt(kb[n].astype(BF16), k16[n]) for n in chains]
        qk = [_dot_nt(q[n].astype(BF16), k16[n]) for n in chains]
        m_mat = [jnp.where(strict, kk[n] * decay[n], 0.0) for n in chains]
        a_qk = [(qk[n] * decay[n]).astype(BF16) for n in chains]
        rhs = [jnp.concatenate([v[n] * beta[n], kb[n] * e_cum[n]], axis=1) for n in chains]
        sol = _unit_lower_solve(m_mat, rhs, eye, sub_diag)
        q_dec = [(q[n] * e_cum[n]).astype(BF16) for n in chains]
        k_dec_t = [(k[n] * jnp.exp(g_last[n] - gcum[n])).T.astype(BF16) for n in chains]
        state = [s_ref[0, i] for i in heads]
        for u in range(cpt):
            ns = [u * hg + i for i in heads]
            s16 = [x.astype(BF16) for x in state]
            w_s = [_dot(sol[ns[i]][:, DH:].astype(BF16), s16[i]) for i in heads]
            q_s = [_dot(q_dec[ns[i]], s16[i]) for i in heads]
            vn16 = [(sol[ns[i]][:, :DH] - w_s[i]).astype(BF16) for i in heads]
            o = [q_s[i] + _dot(a_qk[ns[i]], vn16[i]) for i in heads]
            upd = [_dot(k_dec_t[ns[i]], vn16[i]) for i in heads]
            state = [state[i] * jnp.exp(g_last[ns[i]]) + upd[i] for i in heads]
            y = jnp.concatenate(
                [_rms_rows(o[i]) * gain_ref[...] * _silu(gate[u][:, cols[i]]) for i in heads], axis=1)
            if padded:
                y_ref[0] = y[:t].astype(y_ref.dtype)
            else:
                y_ref[0, pl.ds(pl.multiple_of((it * cpt + u) * c, c), c), :] = y.astype(y_ref.dtype)
        for i in heads:
            s_ref[0, i] = state[i]
        return carry

    lax.fori_loop(0, n_chunks // cpt, body, 0)


def gdn_group_tail(ba, batch, seq):
    hg = GDN_HEADS_PER_STEP
    groups = H_C // hg
    br = ba[:, :H_C].reshape(batch, seq, groups, hg)
    ar = ba[:, H_C:2 * H_C].reshape(batch, seq, groups, hg)
    cat = jnp.concatenate([br, ar], axis=-1).transpose(0, 2, 1, 3)
    return jnp.pad(cat, ((0, 0), (0, 0), (0, 0), (0, LANE - 2 * hg)))


def gdn_group_vec(v):
    hg = GDN_HEADS_PER_STEP
    return jnp.pad(v.reshape(H_C // hg, 1, hg), ((0, 0), (0, 0), (hg, LANE - 2 * hg)))


def gdn_mixer(proj3, bag, conv_w, alog_vec, dt_vec, out_gain, conv_buf, state0, out_dtype):
    b, t, _ = proj3.shape
    hg = GDN_HEADS_PER_STEP
    wid = hg * DH
    has_state = state0 is not None
    col = lambda off: (lambda i, j: (i, 0, off // wid + j))
    wcol = lambda off: (lambda i, j: (0, off // wid + j))
    in_specs = [pl.BlockSpec((1, t, wid), col(OFF_C)),
                pl.BlockSpec((1, t, wid), col(OFF_C + D_C)),
                pl.BlockSpec((1, t, wid), col(OFF_C + 2 * D_C)),
                pl.BlockSpec((1, t, wid), col(OFF_G)),
                pl.BlockSpec((1, 1, t, LANE), lambda i, j: (i, j, 0, 0)),
                pl.BlockSpec((GDN_CONV, wid), wcol(0)),
                pl.BlockSpec((GDN_CONV, wid), wcol(D_C)),
                pl.BlockSpec((GDN_CONV, wid), wcol(2 * D_C)),
                pl.BlockSpec((1, 1, LANE), lambda i, j: (j, 0, 0)),
                pl.BlockSpec((1, 1, LANE), lambda i, j: (j, 0, 0)),
                pl.BlockSpec((1, DH), lambda i, j: (0, 0))]
    args = [proj3, proj3, proj3, proj3, bag, conv_w, conv_w, conv_w, alog_vec, dt_vec, out_gain]
    if has_state:
        in_specs += [pl.BlockSpec((1, GDN_CONV - 1, wid), col(0)),
                     pl.BlockSpec((1, GDN_CONV - 1, wid), col(D_C)),
                     pl.BlockSpec((1, GDN_CONV - 1, wid), col(2 * D_C)),
                     pl.BlockSpec((1, hg, DH, DH), lambda i, j: (i, j, 0, 0))]
        args += [conv_buf, conv_buf, conv_buf, state0]
    vmem = 2 * (4 * t * wid * 4 + t * LANE * 4 + t * wid * 4 + 2 * hg * DH * DH * 4) + (4 << 20)
    return pl.pallas_call(
        functools.partial(_gdn_kernel, t=t, has_state=has_state),
        grid=(b, H_C // hg),
        in_specs=in_specs,
        out_specs=[pl.BlockSpec((1, t, wid), lambda i, j: (i, 0, j)),
                   pl.BlockSpec((1, hg, DH, DH), lambda i, j: (i, j, 0, 0))],
        out_shape=[jax.ShapeDtypeStruct((b, t, D_C), out_dtype),
                   jax.ShapeDtypeStruct((b, H_C, DH, DH), F32)],
        scratch_shapes=[pltpu.VMEM((GDN_CONV, GDN_CHUNK, wid), F32)] * 3,
        compiler_params=_cparams(("parallel", "parallel"), vmem),
        name="gdn_mixer",
    )(*args)


def _mix_dot(ya_ref, yb_ref, yc_ref, wbf_ref):
    acc = _dot(ya_ref[...], wbf_ref[0:D_A, :])
    acc = acc + _dot(yb_ref[...], wbf_ref[D_A:D_A + D_B, :])
    return acc + _dot(yc_ref[...], wbf_ref[D_A + D_B:, :])


def _proj_out_kernel(yap_ref, ybp_ref, ycp_ref, yas_ref, ybs_ref, ycs_ref, w_ref, xp_ref, xs_ref,
                     op_ref, os_ref, wbf_ref):
    @pl.when(pl.program_id(1) == 0)
    def _():
        wbf_ref[...] = w_ref[...].astype(BF16)
        os_ref[...] = xs_ref[...] + _mix_dot(yas_ref, ybs_ref, ycs_ref, wbf_ref)

    op_ref[...] = xp_ref[...] + _mix_dot(yap_ref, ybp_ref, ycp_ref, wbf_ref)


def proj_out(mix_p, mix_s, w_all, layer, x_p, x_s):
    mp, n = x_p.shape
    ms = x_s.shape[0]
    tm, tn = MM_TM, MM_TN
    kk = D_A + D_B + D_C
    vmem = 2 * ((tm + ms) * kk * 2 + kk * tn * 4 + 2 * (tm + ms) * tn * 4) + 3 * kk * tn * 2
    rowp = lambda j, i: (i, 0)
    rows = lambda j, i: (0, 0)
    widths = (D_A, D_B, D_C)
    return pl.pallas_call(
        _proj_out_kernel,
        grid=(n // tn, mp // tm),
        in_specs=([pl.BlockSpec((tm, w), rowp) for w in widths]
                  + [pl.BlockSpec((ms, w), rows) for w in widths]
                  + [pl.BlockSpec((None, kk, tn), lambda j, i: (layer, 0, j)),
                     pl.BlockSpec((tm, tn), lambda j, i: (i, j)),
                     pl.BlockSpec((ms, tn), lambda j, i: (0, j))]),
        out_specs=[pl.BlockSpec((tm, tn), lambda j, i: (i, j)),
                   pl.BlockSpec((ms, tn), lambda j, i: (0, j))],
        out_shape=[jax.ShapeDtypeStruct((mp, n), F32), jax.ShapeDtypeStruct((ms, n), F32)],
        scratch_shapes=[pltpu.VMEM((kk, tn), BF16)],
        compiler_params=_cparams(("arbitrary", "arbitrary"), vmem),
        name="proj_out",
    )(*mix_p, *mix_s, w_all, x_p, x_s)


FFN_TN = 256


def _ffn_conv(cur, ext_ref, cw_ref, b_ref, rows):
    out = cur * cw_ref[FFN_CONV - 1:FFN_CONV, :] + b_ref[...]
    for i in range(1, FFN_CONV):
        out = out + ext_ref[SUBLANE - i:SUBLANE - i + rows, :] * cw_ref[FFN_CONV - 1 - i:FFN_CONV - i, :]
    return out


def _shift_rows(u, halo, i):
    n, w = u.shape
    rot = pltpu.roll(u.reshape(n // SUBLANE, SUBLANE, w), i, axis=1)
    above = jnp.concatenate([pltpu.roll(halo, i, axis=0)[None], rot[:-1]], axis=0)
    sub = lax.broadcasted_iota(jnp.int32, (1, SUBLANE, 1), 1)
    return jnp.where(sub < i, above, rot).reshape(n, w)


def _ffn_conv_rows(u, halo, cw_ref, b_ref):
    out = u * cw_ref[FFN_CONV - 1:FFN_CONV, :] + b_ref[...]
    for i in range(1, FFN_CONV):
        out = out + _shift_rows(u, halo, i) * cw_ref[FFN_CONV - 1 - i:FFN_CONV - i, :]
    return out


FFN_ROW_SPLIT = 2


def _ffn_up_kernel(xp_ref, xs_ref, wg_ref, wv_ref, cwg_ref, cwv_ref, bg_ref, bv_ref, sg_ref, sv_ref,
                   actp_ref, tg_ref, tv_ref, acts_ref, ugs_ref, uvs_ref,
                   wgbf_ref, wvbf_ref, halo_ref, ext_ref, *, tiles_per_seq, t):
    tm = xp_ref.shape[0]
    rows = tm // FFN_ROW_SPLIT
    mi = pl.program_id(1)

    @pl.when(mi == 0)
    def _():
        wgbf_ref[...] = wg_ref[...].astype(BF16)
        wvbf_ref[...] = wv_ref[...].astype(BF16)
        assert t == SUBLANE
        xs = xs_ref[...]
        nb = xs.shape[0] // t
        us = _dot(xs, wgbf_ref[...])
        vs = _dot(xs, wvbf_ref[...])
        ugs_ref[...] = us
        uvs_ref[...] = vs

        def conv(u, st_ref, cw_ref, b_ref):
            outs = []
            for bi in range(nb):
                cur = u[bi * t:(bi + 1) * t]
                ext_ref[SUBLANE - (FFN_CONV - 1):SUBLANE, :] = st_ref[bi]
                ext_ref[SUBLANE:, :] = cur
                outs.append(_ffn_conv(cur, ext_ref, cw_ref, b_ref, t))
            return jnp.concatenate(outs, axis=0)

        gts = conv(us, sg_ref, cwg_ref, bg_ref)
        vls = conv(vs, sv_ref, cwv_ref, bv_ref)
        acts_ref[...] = (_silu(gts) * vls).astype(acts_ref.dtype)

    @pl.when(mi % tiles_per_seq == 0)
    def _():
        halo_ref[...] = jnp.zeros(halo_ref.shape, F32)

    halo_g = halo_ref[0]
    halo_v = halo_ref[1]
    ug = []
    uv = []
    for s in range(FFN_ROW_SPLIT):
        x = xp_ref[s * rows:(s + 1) * rows, :]
        ug.append(_dot(x, wgbf_ref[...]))
        uv.append(_dot(x, wvbf_ref[...]))
    for s in range(FFN_ROW_SPLIT):
        gt = _ffn_conv_rows(ug[s], halo_g, cwg_ref, bg_ref)
        vl = _ffn_conv_rows(uv[s], halo_v, cwv_ref, bv_ref)
        actp_ref[s * rows:(s + 1) * rows, :] = (_silu(gt) * vl).astype(actp_ref.dtype)
        halo_g = ug[s][rows - SUBLANE:]
        halo_v = uv[s][rows - SUBLANE:]
    halo_ref[0] = halo_g
    halo_ref[1] = halo_v
    tg_ref[0] = halo_g
    tv_ref[0] = halo_v


def ffn_up(xn_p, xn_s, w_up, conv_w, conv_b, conv_state, layer, seq, t):
    mp, k = xn_p.shape
    ms = xn_s.shape[0]
    d_ff = w_up.shape[2] // 2
    tn = FFN_TN
    nt = d_ff // tn
    tm = min(seq, MM_TM)
    mt = mp // tm
    nb = ms // t
    lo = lambda j, i: (layer, 0, j)
    hi = lambda j, i: (layer, 0, nt + j)
    vmem = (2 * (tm * k * 2 + ms * k * 2 + 2 * k * tn * 4 + tm * tn * 2) + 2 * k * tn * 2
            + 8 * tm * tn * 4)
    return pl.pallas_call(
        functools.partial(_ffn_up_kernel, tiles_per_seq=seq // tm, t=t),
        grid=(nt, mt),
        in_specs=[pl.BlockSpec((tm, k), lambda j, i: (i, 0)),
                  pl.BlockSpec((ms, k), lambda j, i: (0, 0)),
                  pl.BlockSpec((None, k, tn), lo), pl.BlockSpec((None, k, tn), hi),
                  pl.BlockSpec((None, FFN_CONV, tn), lo), pl.BlockSpec((None, FFN_CONV, tn), hi),
                  pl.BlockSpec((None, 1, tn), lo), pl.BlockSpec((None, 1, tn), hi),
                  pl.BlockSpec((None, nb, FFN_CONV - 1, tn), lambda j, i: (layer, 0, 0, j)),
                  pl.BlockSpec((None, nb, FFN_CONV - 1, tn), lambda j, i: (layer, 0, 0, nt + j))],
        out_specs=[pl.BlockSpec((tm, tn), lambda j, i: (i, j)),
                   pl.BlockSpec((1, SUBLANE, tn), lambda j, i: (i, 0, j)),
                   pl.BlockSpec((1, SUBLANE, tn), lambda j, i: (i, 0, j)),
                   pl.BlockSpec((ms, tn), lambda j, i: (0, j)),
                   pl.BlockSpec((ms, tn), lambda j, i: (0, j)),
                   pl.BlockSpec((ms, tn), lambda j, i: (0, j))],
        out_shape=[jax.ShapeDtypeStruct((mp, d_ff), BF16),
                   jax.ShapeDtypeStruct((mt, SUBLANE, d_ff), F32),
                   jax.ShapeDtypeStruct((mt, SUBLANE, d_ff), F32),
                   jax.ShapeDtypeStruct((ms, d_ff), BF16),
                   jax.ShapeDtypeStruct((ms, d_ff), F32),
                   jax.ShapeDtypeStruct((ms, d_ff), F32)],
        scratch_shapes=[pltpu.VMEM((k, tn), BF16), pltpu.VMEM((k, tn), BF16),
                        pltpu.VMEM((2, SUBLANE, tn), F32), pltpu.VMEM((SUBLANE + t, tn), F32)],
        compiler_params=_cparams(("arbitrary", "arbitrary"), vmem),
        name="ffn_up",
    )(xn_p, xn_s, w_up, w_up, conv_w, conv_w, conv_b, conv_b, conv_state, conv_state)


def _ffn_down_kernel(a_ref, w_ref, h_ref, o_ref):
    o_ref[...] = h_ref[...] + _dot(a_ref[...], w_ref[...])


def ffn_down_proj(act, w_down, layer, h):
    m, k = act.shape
    n = w_down.shape[2]
    tm = min(m, 512)
    tn = 512
    vmem = 2 * (tm * k * 2 + k * tn * 2 + 2 * tm * tn * 4)
    return pl.pallas_call(
        _ffn_down_kernel,
        grid=(m // tm, n // tn),
        in_specs=[pl.BlockSpec((tm, k), lambda i, j: (i, 0)),
                  pl.BlockSpec((None, k, tn), lambda i, j: (layer, 0, j)),
                  pl.BlockSpec((tm, tn), lambda i, j: (i, j))],
        out_specs=pl.BlockSpec((tm, tn), lambda i, j: (i, j)),
        out_shape=jax.ShapeDtypeStruct((m, n), F32),
        compiler_params=_cparams(("parallel", "parallel"), vmem),
        name="ffn_down",
    )(act, w_down, h)


def _t5_bucket(dist):
    max_exact = REL_BUCKETS // 2
    d = jnp.maximum(dist, 1).astype(F32)
    large = max_exact + (jnp.log(d / max_exact) / math.log(REL_MAX_DIST / max_exact)
                         * (REL_BUCKETS - max_exact)).astype(jnp.int32)
    large = jnp.minimum(large, REL_BUCKETS - 1)
    return jnp.where(dist < max_exact, dist, large)


def _bias_lookup(rel_bias, buckets):
    onehot = jax.nn.one_hot(buckets, REL_BUCKETS, dtype=F32)
    return jnp.einsum('...k,kh->...h', onehot, rel_bias.astype(F32), precision=HIGHEST)


def _prompt_bias_blocks(rel_bias):
    qi = np.arange(BLK)[:, None]
    ki = np.arange(2 * BLK)[None, :]
    rel = qi + BLK - ki
    out = []
    for window, dil in A_BRANCHES:
        nj = window // dil + 1
        valid = (rel >= 0) & (rel < nj)
        buckets = _t5_bucket(jnp.asarray(np.clip(rel, 0, nj - 1) * dil, jnp.int32))
        bias = _bias_lookup(rel_bias, buckets)
        out.append(jnp.where(jnp.asarray(valid)[..., None], bias, NEG_INF).transpose(2, 0, 1))
    return jnp.stack(out)


def _branch_count(dist):
    cnt = np.zeros(dist.shape, np.float32)
    for window, dil in A_BRANCHES:
        cnt += ((dist >= 0) & (dist % dil == 0) & (dist // dil <= window // dil)).astype(np.float32)
    return cnt


def _sample_bias_tables(rel_bias, t, n_past):
    tq = np.arange(t)[:, None]
    d_cache = n_past + tq - np.arange(n_past)[None, :]
    d_new = tq - np.arange(BLK)[None, :]
    d_new = np.where(np.arange(BLK)[None, :] < t, d_new, -1)
    tables = []
    for dist in (d_cache, d_new):
        cnt = _branch_count(dist)
        bias = _bias_lookup(rel_bias, _t5_bucket(jnp.asarray(np.maximum(dist, 0), jnp.int32)))
        bias = jnp.where(jnp.asarray(cnt > 0)[..., None], bias, NEG_INF).transpose(2, 0, 1)
        tables += [bias, jnp.asarray(cnt)]
    return tables


def _mixers(proj, ba, lw, *, batch, seq, layer, cache_hk=None, pool_buf=None, gconv_buf=None,
            gstate=None, bias_tabs=None):
    sample = cache_hk is not None
    mix_dtype = F32 if sample else BF16
    proj3 = proj.reshape(batch, seq, N_MAIN)
    bag = gdn_group_tail(ba, batch, seq)
    qg = lw['a_q_norm'].reshape(1, DH)
    kg = lw['a_k_norm'].reshape(1, DH)
    og = lw['a_out_norm'].reshape(H_A, DH)
    if sample:
        ya, kv = attn_sample(proj3, cache_hk, layer, qg, kg, og, *bias_tabs)
    else:
        ya, kv = attn_prompt(proj3, qg, kg, og, bias_tabs)
    kv_rows = kv.transpose(0, 1, 3, 2, 4)
    yb = pool_mixer(proj3, pool_buf, lw['pool_w'], lw['pool_scale'].reshape(1, D_B), mix_dtype)
    pu = proj3[:, :, OFF_P:OFF_P + D_B]
    if sample:
        pool_new = jnp.concatenate([pool_buf, pu], axis=1)[:, -POOL_BUF:]
    else:
        pool_new = pu[:, -POOL_BUF:]
    yc, gstate_new = gdn_mixer(proj3, bag, lw['gdn_conv_w'], lw['alog_vec'], lw['dt_vec'],
                               lw['gdn_out_norm'].reshape(1, DH), gconv_buf, gstate, mix_dtype)
    gconv_new = proj3[:, -(GDN_CONV - 1):, OFF_C:OFF_C + 3 * D_C]
    m = batch * seq
    mix = (ya.reshape(m, D_A).astype(BF16), yb.reshape(m, D_B).astype(BF16),
           yc.reshape(m, D_C).astype(BF16))
    return mix, (kv_rows, pool_new, gconv_new, gstate_new)


def _layer(xp, xs, lw, gw, *, layer, prompt_shape, sample_shape, cache_hk, pool_buf, gconv_buf, gstate,
           fconv_state, prompt_bias, sample_tabs):
    bp, sp = prompt_shape
    bs, ss = sample_shape
    xnp = rmsnorm_cast(xp, lw['norm_mix'])
    xns = rmsnorm_cast(xs, lw['norm_mix'])
    proj_p, proj_s = proj_in(xnp, xns, gw['w_in_t'], layer)
    ba_p = proj_tail(xnp, gw['w_in_t'], layer)
    ba_s = proj_tail(xns, gw['w_in_t'], layer)
    mix_p, outs_p = _mixers(proj_p, ba_p, lw, batch=bp, seq=sp, layer=layer, bias_tabs=prompt_bias)
    mix_s, outs_s = _mixers(proj_s, ba_s, lw, batch=bs, seq=ss, layer=layer, cache_hk=cache_hk,
                            pool_buf=pool_buf, gconv_buf=gconv_buf, gstate=gstate, bias_tabs=sample_tabs)
    hp, hs = proj_out(mix_p, mix_s, gw['w_out'], layer, xp, xs)
    hnp = rmsnorm_cast(hp, lw['norm_ffn'])
    hns = rmsnorm_cast(hs, lw['norm_ffn'])
    act_p, tg, tv, act_s, ug, uv = ffn_up(hnp, hns, gw['ffn_up'], gw['ffn_conv_w'], gw['ffn_conv_b'],
                                          fconv_state, layer, sp, ss)
    tails = jnp.concatenate([tg, tv], axis=-1)
    tiles_per_seq = tails.shape[0] // bp
    fconv_p = tails[tiles_per_seq - 1::tiles_per_seq, -(FFN_CONV - 1):]
    up_s = jnp.concatenate([ug, uv], axis=-1).reshape(bs, ss, -1)
    fconv_s = up_s[:, -(FFN_CONV - 1):]
    yp = ffn_down_proj(act_p, gw['ffn_down'], layer, hp)
    ys = ffn_down_proj(act_s, gw['ffn_down'], layer, hs)
    return yp, ys, outs_p + (fconv_p,), outs_s + (fconv_s,)


def kernel(x_prompt, x_sample, cache_attn_kv, state_pool, state_gdn_conv, state_gdn, state_ffn_conv,
           rel_bias, norm_mix, w_in, a_q_norm, a_k_norm, a_out_norm, pool_w, pool_scale,
           gdn_conv_w, gdn_a_log, gdn_dt_bias, gdn_out_norm, w_out, norm_ffn,
           ffn_up, ffn_conv_w, ffn_conv_b, ffn_down):
    depth = w_in.shape[0]
    bp, sp, d_model = x_prompt.shape
    bs, ss, _ = x_sample.shape
    n_past = cache_attn_kv.shape[2]
    assert w_in.shape[2] == N_MAIN + N_TAIL and w_out.shape[1] == D_A + D_B + D_C

    prompt_bias = _prompt_bias_blocks(rel_bias)
    sample_tabs = _sample_bias_tables(rel_bias, ss, n_past)
    cache_hk = jnp.swapaxes(cache_attn_kv, 3, 4).reshape(depth, bs, n_past * H_A * 2, DH)

    xp = x_prompt.reshape(bp * sp, d_model)
    xs = x_sample.reshape(bs * ss, d_model)
    p_out = [[] for _ in range(5)]
    s_out = [[] for _ in range(5)]
    gw = {
        'w_in_t': jnp.swapaxes(w_in, 1, 2),
        'w_out': w_out, 'ffn_up': ffn_up, 'ffn_down': ffn_down.astype(BF16),
        'ffn_conv_w': ffn_conv_w, 'ffn_conv_b': ffn_conv_b.reshape(depth, 1, -1),
    }
    for l in range(depth):
        lw = {
            'norm_mix': norm_mix[l], 'norm_ffn': norm_ffn[l],
            'a_q_norm': a_q_norm[l], 'a_k_norm': a_k_norm[l], 'a_out_norm': a_out_norm[l],
            'pool_w': pool_w[l].astype(BF16), 'pool_scale': pool_scale[l],
            'gdn_conv_w': gdn_conv_w[l],
            'alog_vec': gdn_group_vec(gdn_a_log[l]), 'dt_vec': gdn_group_vec(gdn_dt_bias[l]),
            'gdn_out_norm': gdn_out_norm[l],
        }
        xp, xs, outs_p, outs_s = _layer(
            xp, xs, lw, gw, layer=l, prompt_shape=(bp, sp), sample_shape=(bs, ss), cache_hk=cache_hk,
            pool_buf=state_pool[l], gconv_buf=state_gdn_conv[l], gstate=state_gdn[l],
            fconv_state=state_ffn_conv, prompt_bias=prompt_bias, sample_tabs=sample_tabs)
        for acc, o in zip(p_out, outs_p):
            acc.append(o)
        for acc, o in zip(s_out, outs_s):
            acc.append(o)
    res = [xp.reshape(bp, sp, d_model), xs.reshape(bs, ss, d_model)]
    for po, so in zip(p_out, s_out):
        res += [jnp.stack(po), jnp.stack(so)]
    return tuple(res)
```

```python
import functools
import math

import numpy as np
import jax
import jax.numpy as jnp
from jax import lax
from jax.experimental import pallas as pl
from jax.experimental.pallas import tpu as pltpu

F32 = jnp.float32
BF16 = jnp.bfloat16
HIGHEST = lax.Precision.HIGHEST

DH = 128
H_A = 12
H_C = 12
A_BRANCHES = ((128, 1), (512, 4), (2048, 16))
BLK = 128
REL_BUCKETS = 32
REL_MAX_DIST = 2048
POOL_WINDOWS = (2, 4, 8, 16)
CG = 256
POOL_BUF = 15
GDN_CONV = 4
GDN_CHUNK = 64
GDN_SUB = 16
FFN_CONV = 3
EPS = 1e-6
NEG_INF = -1e30

D_A = H_A * DH
D_B = len(POOL_WINDOWS) * CG
D_C = H_C * DH
OFF_Q, OFF_K, OFF_V = 0, D_A, 2 * D_A
OFF_P = 3 * D_A
OFF_C = OFF_P + D_B
OFF_G = OFF_C + 3 * D_C
N_MAIN = OFF_G + D_C
N_TAIL = 2 * H_C

LANE = 128
SUBLANE = 8
VMEM_CAP = 56 * 1024 * 1024


def _cparams(sem, vmem_bytes):
    limit = int(min(max(vmem_bytes * 5 // 4 + (2 << 20), 16 << 20), VMEM_CAP))
    return pltpu.CompilerParams(dimension_semantics=sem, vmem_limit_bytes=limit)


def _rms_rows(x):
    return x * lax.rsqrt(jnp.mean(x * x, axis=-1, keepdims=True) + EPS)


def _silu(x):
    return x * jax.nn.sigmoid(x)


def _dot(a, b):
    return jnp.dot(a, b, preferred_element_type=F32)


def _dot_nt(a, b):
    return lax.dot_general(a, b, (((1,), (1,)), ((), ())), preferred_element_type=F32)


def _hdot(a, b):
    return jnp.dot(a, b, preferred_element_type=F32, precision=HIGHEST)


def _hdot_nt(a, b):
    return lax.dot_general(a, b, (((1,), (1,)), ((), ())), preferred_element_type=F32,
                           precision=HIGHEST)


def _hdot_tn(a, b):
    return lax.dot_general(a, b, (((0,), (0,)), ((), ())), preferred_element_type=F32,
                           precision=HIGHEST)


def _rmsnorm_kernel(x_ref, g_ref, o_ref):
    o_ref[...] = (_rms_rows(x_ref[...]) * g_ref[...]).astype(o_ref.dtype)


def rmsnorm_cast(x, gain):
    m, d = x.shape
    tm = min(m, 512)
    return pl.pallas_call(
        _rmsnorm_kernel,
        grid=(m // tm,),
        in_specs=[pl.BlockSpec((tm, d), lambda i: (i, 0)),
                  pl.BlockSpec((1, d), lambda i: (0, 0))],
        out_specs=pl.BlockSpec((tm, d), lambda i: (i, 0)),
        out_shape=jax.ShapeDtypeStruct((m, d), BF16),
        compiler_params=_cparams(("parallel",), 2 * tm * d * 6),
        name="rmsnorm_cast",
    )(x, gain.reshape(1, d))


MM_TM = 1024
MM_TN = 512


def _proj_in_kernel(xp_ref, xs_ref, w_ref, op_ref, os_ref, wbf_ref):
    @pl.when(pl.program_id(1) == 0)
    def _():
        wbf_ref[...] = w_ref[...].astype(BF16)
        os_ref[...] = _dot_nt(xs_ref[...], wbf_ref[...])

    op_ref[...] = _dot_nt(xp_ref[...], wbf_ref[...])


def proj_in(xn_p, xn_s, w_t, layer):
    mp, k = xn_p.shape
    ms = xn_s.shape[0]
    tm, tn = MM_TM, MM_TN
    vmem = 2 * (tm * k * 2 + ms * k * 2 + k * tn * 4 + tm * tn * 4 + ms * tn * 4) + 3 * k * tn * 2
    return pl.pallas_call(
        _proj_in_kernel,
        grid=(N_MAIN // tn, mp // tm),
        in_specs=[pl.BlockSpec((tm, k), lambda j, i: (i, 0)),
                  pl.BlockSpec((ms, k), lambda j, i: (0, 0)),
                  pl.BlockSpec((None, tn, k), lambda j, i: (layer, j, 0))],
        out_specs=[pl.BlockSpec((tm, tn), lambda j, i: (i, j)),
                   pl.BlockSpec((ms, tn), lambda j, i: (0, j))],
        out_shape=[jax.ShapeDtypeStruct((mp, N_MAIN), F32),
                   jax.ShapeDtypeStruct((ms, N_MAIN), F32)],
        scratch_shapes=[pltpu.VMEM((tn, k), BF16)],
        compiler_params=_cparams(("arbitrary", "arbitrary"), vmem),
        name="proj_in",
    )(xn_p, xn_s, w_t)


def _proj_tail_kernel(x_ref, *refs):
    w_refs, o_ref = refs[:-1], refs[-1]
    k = x_ref.shape[1]
    pad = jnp.zeros((LANE - SUBLANE * len(w_refs), k), F32)
    w = jnp.concatenate([r[...] for r in w_refs] + [pad], axis=0).astype(BF16)
    o_ref[...] = _dot_nt(x_ref[...], w)


def proj_tail(xn, w_t, layer):
    m, k = xn.shape
    tm = min(m, MM_TM)
    assert N_TAIL % SUBLANE == 0 and N_MAIN % SUBLANE == 0
    nblk = N_TAIL // SUBLANE
    vmem = 2 * (tm * k * 2 + nblk * SUBLANE * k * 4 + tm * LANE * 4) + 2 * LANE * k * 4
    return pl.pallas_call(
        _proj_tail_kernel,
        grid=(m // tm,),
        in_specs=[pl.BlockSpec((tm, k), lambda i: (i, 0))]
        + [pl.BlockSpec((None, SUBLANE, k), (lambda i, r=r: (layer, N_MAIN // SUBLANE + r, 0)))
           for r in range(nblk)],
        out_specs=pl.BlockSpec((tm, LANE), lambda i: (i, 0)),
        out_shape=jax.ShapeDtypeStruct((m, LANE), F32),
        compiler_params=_cparams(("parallel",), vmem),
        name="proj_tail",
    )(xn, *([w_t] * nblk))


ATTN_BLOCKS_PER_TRIP = 4


def _strided_rows(start, dil):
    return pl.ds(start, BLK) if dil == 1 else pl.ds(start, BLK, stride=dil)


def _attn_prompt_kernel(q_ref, k_ref, v_ref, qg_ref, kg_ref, og_ref, bias_ref,
                        y_ref, kv_ref, qs_ref, ks_ref, acc_ref, m_ref, l_ref):
    seq = q_ref.shape[1]
    qs_ref[...] = _rms_rows(q_ref[0]) * qg_ref[...] * (DH ** -0.5)
    ks_ref[...] = _rms_rows(k_ref[0]) * kg_ref[...]
    kv_ref[0, :, 0, 0, :] = ks_ref[...]
    kv_ref[0, :, 0, 1, :] = v_ref[0]
    acc_ref[...] = jnp.zeros(acc_ref.shape, F32)
    l_ref[...] = jnp.zeros(l_ref.shape, F32)
    m_ref[...] = jnp.full(m_ref.shape, NEG_INF, F32)

    for bi, (window, dil) in enumerate(A_BRANCHES):
        assert window // dil == BLK and seq % (dil * BLK) == 0
        nb = seq // (dil * BLK)

        assert (dil * nb) % ATTN_BLOCKS_PER_TRIP == 0

        def body(it, carry, bi=bi, dil=dil, nb=nb):
            us = range(ATTN_BLOCKS_PER_TRIP)
            item = [it * ATTN_BLOCKS_PER_TRIP + u for u in us]
            r = [x // nb for x in item]
            n = [item[u] - r[u] * nb for u in us]
            rows = [_strided_rows(n[u] * (BLK * dil) + r[u], dil) for u in us]
            qb = [qs_ref[rw, :].astype(BF16) for rw in rows]
            kc = [ks_ref[rw, :] for rw in rows]
            vc = [v_ref[0, rw, :] for rw in rows]
            bias_cur = bias_ref[bi, 0, :, BLK:]
            if nb > 1:
                prows = [_strided_rows(jnp.maximum(n[u] - 1, 0) * (BLK * dil) + r[u], dil) for u in us]
                kk = [jnp.concatenate([ks_ref[prows[u], :], kc[u]], axis=0).astype(BF16) for u in us]
                vv = [jnp.concatenate([v_ref[0, prows[u], :], vc[u]], axis=0).astype(BF16) for u in us]
                bias_prev = bias_ref[bi, 0, :, :BLK]
                bias = [jnp.concatenate([jnp.where(n[u] > 0, bias_prev, NEG_INF), bias_cur], axis=1)
                        for u in us]
            else:
                kk = [x.astype(BF16) for x in kc]
                vv = [x.astype(BF16) for x in vc]
                bias = [bias_cur for _ in us]
            s = [_dot_nt(qb[u], kk[u]) + bias[u] for u in us]
            m_old = [m_ref[rw, :] for rw in rows]
            m_new = [jnp.maximum(m_old[u], jnp.max(s[u], axis=1, keepdims=True)) for u in us]
            p = [jnp.exp(s[u] - m_new[u][:, :1]) for u in us]
            alpha = [jnp.exp(m_old[u] - m_new[u]) for u in us]
            pv = [_dot(p[u].astype(BF16), vv[u]) for u in us]
            for u in us:
                l_ref[rows[u], :] = alpha[u] * l_ref[rows[u], :] + jnp.sum(p[u], axis=1, keepdims=True)
                acc_ref[rows[u], :] = alpha[u] * acc_ref[rows[u], :] + pv[u]
                m_ref[rows[u], :] = m_new[u]
            return carry

        lax.fori_loop(0, dil * nb // ATTN_BLOCKS_PER_TRIP, body, 0)

    o = acc_ref[...] / l_ref[...]
    y_ref[0] = (_rms_rows(o) * og_ref[0]).astype(y_ref.dtype)


def attn_prompt(proj3, q_gain, k_gain, o_gain, bias_blocks):
    b, s, _ = proj3.shape
    col = lambda off: (lambda i, h: (i, 0, off // DH + h))
    vmem = 2 * (3 * s * DH * 4 + s * DH * 2 + s * DH * 4 + 3 * BLK * 2 * BLK * 4) + 5 * s * DH * 4
    return pl.pallas_call(
        _attn_prompt_kernel,
        grid=(b, H_A),
        in_specs=[pl.BlockSpec((1, s, DH), col(OFF_Q)),
                  pl.BlockSpec((1, s, DH), col(OFF_K)),
                  pl.BlockSpec((1, s, DH), col(OFF_V)),
                  pl.BlockSpec((1, DH), lambda i, h: (0, 0)),
                  pl.BlockSpec((1, DH), lambda i, h: (0, 0)),
                  pl.BlockSpec((1, 1, DH), lambda i, h: (h, 0, 0)),
                  pl.BlockSpec((len(A_BRANCHES), 1, BLK, 2 * BLK), lambda i, h: (0, h, 0, 0))],
        out_specs=[pl.BlockSpec((1, s, DH), lambda i, h: (i, 0, h)),
                   pl.BlockSpec((1, s, 1, 2, DH), lambda i, h: (i, 0, h, 0, 0))],
        out_shape=[jax.ShapeDtypeStruct((b, s, D_A), BF16),
                   jax.ShapeDtypeStruct((b, s, H_A, 2, DH), F32)],
        scratch_shapes=[pltpu.VMEM((s, DH), F32)] * 5,
        compiler_params=_cparams(("parallel", "parallel"), vmem),
        name="attn_prompt",
    )(proj3, proj3, proj3, q_gain, k_gain, o_gain.reshape(H_A, 1, DH), bias_blocks)


ATTN_SAMPLE_TK = 512


def _attn_sample_kernel(q_ref, k_ref, v_ref, c_ref, qg_ref, kg_ref, og_ref,
                        bc_ref, cc_ref, bn_ref, cn_ref, y_ref, kv_ref, qn_ref, m_ref, l_ref, acc_ref):
    t = q_ref.shape[1]
    tk = c_ref.shape[2] // (2 * H_A)
    kt = pl.program_id(1)
    heads = range(H_A)
    cols = [slice(h * DH, (h + 1) * DH) for h in heads]

    @pl.when(kt == 0)
    def _():
        pad = jnp.zeros((BLK - t, DH), F32)
        for h in heads:
            qn = _rms_rows(q_ref[0, :, cols[h]]) * qg_ref[...] * (DH ** -0.5)
            kn = _rms_rows(k_ref[0, :, cols[h]]) * kg_ref[...]
            vn = v_ref[0, :, cols[h]]
            qn_ref[:, cols[h]] = qn
            kv_ref[0, :, h, 0, :] = kn
            kv_ref[0, :, h, 1, :] = vn
            kpad = jnp.concatenate([kn, pad], axis=0).astype(BF16)
            vpad = jnp.concatenate([vn, pad], axis=0).astype(BF16)
            s = _dot_nt(qn.astype(BF16), kpad) + bn_ref[h]
            m = jnp.max(s, axis=1, keepdims=True)
            p = jnp.exp(s - m) * cn_ref[...]
            m_ref[h] = jnp.broadcast_to(m, (t, DH))
            l_ref[h] = jnp.broadcast_to(jnp.sum(p, axis=1, keepdims=True), (t, DH))
            acc_ref[h] = _dot(p.astype(BF16), vpad)

    q16 = [qn_ref[:, cols[h]].astype(BF16) for h in heads]
    kc = [c_ref[0, 0, pl.ds(2 * h, tk, stride=2 * H_A), :].astype(BF16) for h in heads]
    vc = [c_ref[0, 0, pl.ds(2 * h + 1, tk, stride=2 * H_A), :].astype(BF16) for h in heads]
    s = [_dot_nt(q16[h], kc[h]) + bc_ref[h] for h in heads]
    m_old = [m_ref[h] for h in heads]
    m_new = [jnp.maximum(m_old[h], jnp.max(s[h], axis=1, keepdims=True)) for h in heads]
    p = [jnp.exp(s[h] - m_new[h][:, :1]) * cc_ref[...] for h in heads]
    alpha = [jnp.exp(m_old[h] - m_new[h]) for h in heads]
    pv = [_dot(p[h].astype(BF16), vc[h]) for h in heads]
    for h in heads:
        l_ref[h] = alpha[h] * l_ref[h] + jnp.sum(p[h], axis=1, keepdims=True)
        acc_ref[h] = alpha[h] * acc_ref[h] + pv[h]
        m_ref[h] = m_new[h]

    @pl.when(kt == pl.num_programs(1) - 1)
    def _():
        for h in heads:
            o = acc_ref[h] / l_ref[h]
            y_ref[0, :, cols[h]] = _rms_rows(o) * og_ref[:, cols[h]]


def attn_sample(proj3, cache_rows, layer, q_gain, k_gain, o_gain, bias_c, cnt_c, bias_n, cnt_n):
    b, t, _ = proj3.shape
    n_past = cache_rows.shape[2] // (2 * H_A)
    tk = ATTN_SAMPLE_TK
    col = lambda off: (lambda i, j: (i, 0, off // D_A))
    vmem = (2 * (tk * 2 * H_A * DH * 4 + 3 * t * D_A * 4 + H_A * t * tk * 4 + t * tk * 4
                 + 2 * t * 2 * D_A * 4) + 4 * H_A * t * DH * 4 + (2 << 20))
    return pl.pallas_call(
        _attn_sample_kernel,
        grid=(b, n_past // tk),
        in_specs=[pl.BlockSpec((1, t, D_A), col(OFF_Q)),
                  pl.BlockSpec((1, t, D_A), col(OFF_K)),
                  pl.BlockSpec((1, t, D_A), col(OFF_V)),
                  pl.BlockSpec((1, 1, tk * 2 * H_A, DH), lambda i, j: (layer, i, j, 0)),
                  pl.BlockSpec((1, DH), lambda i, j: (0, 0)),
                  pl.BlockSpec((1, DH), lambda i, j: (0, 0)),
                  pl.BlockSpec((1, D_A), lambda i, j: (0, 0)),
                  pl.BlockSpec((H_A, t, tk), lambda i, j: (0, 0, j)),
                  pl.BlockSpec((t, tk), lambda i, j: (0, j)),
                  pl.BlockSpec((H_A, t, BLK), lambda i, j: (0, 0, 0)),
                  pl.BlockSpec((t, BLK), lambda i, j: (0, 0))],
        out_specs=[pl.BlockSpec((1, t, D_A), lambda i, j: (i, 0, 0)),
                   pl.BlockSpec((1, t, H_A, 2, DH), lambda i, j: (i, 0, 0, 0, 0))],
        out_shape=[jax.ShapeDtypeStruct((b, t, D_A), F32),
                   jax.ShapeDtypeStruct((b, t, H_A, 2, DH), F32)],
        scratch_shapes=[pltpu.VMEM((t, D_A), F32)] + [pltpu.VMEM((H_A, t, DH), F32)] * 3,
        compiler_params=_cparams(("parallel", "arbitrary"), vmem),
        name="attn_sample",
    )(proj3, proj3, proj3, cache_rows, q_gain, k_gain, o_gain.reshape(1, D_A),
      bias_c, cnt_c, bias_n, cnt_n)


POOL_HALO = 16


def _pool_kernel(*refs, t, rows, n_valid, has_buf):
    ng = len(POOL_WINDOWS)
    u_refs = refs[:ng]
    refs = refs[ng:]
    if has_buf:
        buf_refs = refs[:ng]
        refs = refs[ng:]
    w_ref, scale_ref, y_ref, ext_ref = refs
    for g, win in enumerate(POOL_WINDOWS):
        ext_ref[0:POOL_HALO, :] = jnp.zeros((POOL_HALO, CG), F32)
        if has_buf:
            ext_ref[POOL_HALO - POOL_BUF:POOL_HALO, :] = buf_refs[g][0]
        ext_ref[POOL_HALO:, :] = u_refs[g][0]

        def chunk(ci, carry, g=g, win=win):
            base = pl.multiple_of(ci * rows, SUBLANE)
            w = ext_ref[pl.ds(base, rows + POOL_HALO), :]
            u = w[POOL_HALO:]
            assert win & (win - 1) == 0
            acc = w
            k = 1
            while k < win:
                acc = acc[:acc.shape[0] - k] + acc[k:]
                k *= 2
            tot = acc[POOL_HALO + 1 - win:POOL_HALO + 1 - win + rows]
            pos = base + lax.broadcasted_iota(jnp.int32, (rows, 1), 0)
            cnt = jnp.minimum(win, n_valid + pos + 1).astype(F32)
            d = tot / cnt - u
            y = _rms_rows(_dot(d.astype(BF16), w_ref[g])) * scale_ref[:, g * CG:(g + 1) * CG]
            y_ref[0, pl.ds(base, rows), g * CG:(g + 1) * CG] = y.astype(y_ref.dtype)
            return carry

        lax.fori_loop(0, t // rows, chunk, 0)


def pool_mixer(proj3, bufs, w_pool, scale, out_dtype):
    b, t, _ = proj3.shape
    ng = len(POOL_WINDOWS)
    rows = min(t, 256)
    has_buf = bufs is not None
    in_specs = [pl.BlockSpec((1, t, CG), (lambda i, g=g: (i, 0, OFF_P // CG + g))) for g in range(ng)]
    args = [proj3] * ng
    if has_buf:
        in_specs += [pl.BlockSpec((1, POOL_BUF, CG), (lambda i, g=g: (i, 0, g))) for g in range(ng)]
        args += [bufs] * ng
    in_specs += [pl.BlockSpec((ng, CG, CG), lambda i: (0, 0, 0)),
                 pl.BlockSpec((1, D_B), lambda i: (0, 0))]
    args += [w_pool, scale]
    vmem = 2 * (ng * t * CG * 4 + t * D_B * 4 + ng * CG * CG * 2) + (t + POOL_HALO) * CG * 4
    return pl.pallas_call(
        functools.partial(_pool_kernel, t=t, rows=rows, n_valid=POOL_BUF if has_buf else 0,
                          has_buf=has_buf),
        grid=(b,),
        in_specs=in_specs,
        out_specs=pl.BlockSpec((1, t, D_B), lambda i: (i, 0, 0)),
        out_shape=jax.ShapeDtypeStruct((b, t, D_B), out_dtype),
        scratch_shapes=[pltpu.VMEM((t + POOL_HALO, CG), F32)],
        compiler_params=_cparams(("parallel",), vmem),
        name="pool_mixer",
    )(*args)


GDN_HEADS_PER_STEP = 4
GDN_CHUNKS_PER_TRIP = 4


def _split_bf16(a):
    hi = a.astype(BF16)
    return hi, (a - hi.astype(F32)).astype(BF16)


def _dot3(a, b):
    return _dot(a[0], b[0]) + (_dot(a[0], b[1]) + _dot(a[1], b[0]))


def _cumsum_rows(x):
    n = x.shape[0]
    row = lax.broadcasted_iota(jnp.int32, (n, 1), 0)
    k = 1
    while k < n:
        x = x + jnp.where(row >= k, pltpu.roll(x, k, axis=0), 0.0)
        k *= 2
    return x


def _unit_lower_solve(mats, rhss, eye, sub_diag):
    assert GDN_CHUNK // GDN_SUB == 4 and GDN_SUB == 16
    idx = range(len(mats))
    d = [jnp.where(sub_diag, m, 0.0) for m in mats]
    low = [_split_bf16(m - di) for m, di in zip(mats, d)]
    rs = [_split_bf16(r) for r in rhss]
    x = [eye - di for di in d]
    ps = [_split_bf16(di) for di in d]
    ps = [_split_bf16(_dot3(ps[i], ps[i])) for i in idx]
    for _ in range(2):
        xs = [_split_bf16(xi) for xi in x]
        x = [x[i] + _dot3(xs[i], ps[i]) for i in idx]
        ps = [_split_bf16(_dot3(ps[i], ps[i])) for i in idx]
    xs = [_split_bf16(xi) for xi in x]
    x = [x[i] + _dot3(xs[i], ps[i]) for i in idx]
    xs = [_split_bf16(xi) for xi in x]
    n = [_dot3(xs[i], low[i]) for i in idx]
    xr = [_dot3(xs[i], rs[i]) for i in idx]
    ns = [_split_bf16(ni) for ni in n]
    n2 = [_split_bf16(_dot3(ns[i], ns[i])) for i in idx]
    imn = [eye - ni for ni in n]
    y = [imn[i] + _dot3(_split_bf16(imn[i]), n2[i]) for i in idx]
    return [_dot3(_split_bf16(y[i]), _split_bf16(xr[i])) for i in idx]


def _gdn_kernel(*refs, t, has_state):
    c = GDN_CHUNK
    hg = GDN_HEADS_PER_STEP
    (q_ref, k_ref, v_ref, gate_ref, ba_ref, cwq_ref, cwk_ref, cwv_ref,
     alog_ref, dt_ref, gain_ref) = refs[:11]
    refs = refs[11:]
    if has_state:
        cbq_ref, cbk_ref, cbv_ref, s0_ref = refs[:4]
        refs = refs[4:]
    y_ref, s_ref, cwq_rows, cwk_rows, cwv_rows = refs
    for rows_ref, cw_in in ((cwq_rows, cwq_ref), (cwk_rows, cwk_ref), (cwv_rows, cwv_ref)):
        for j in range(GDN_CONV):
            rows_ref[j] = jnp.broadcast_to(cw_in[j:j + 1, :], rows_ref.shape[1:])
    n_chunks = -(-t // c)
    padded = t % c != 0
    assert (not padded) or n_chunks == 1
    wid = hg * DH
    heads = range(hg)

    if has_state:
        s_ref[0] = s0_ref[0]
    else:
        s_ref[0] = jnp.zeros(s_ref.shape[1:], F32)

    ii = lax.broadcasted_iota(jnp.int32, (c, c), 0)
    jj = lax.broadcasted_iota(jnp.int32, (c, c), 1)
    tril = ii >= jj
    strict = ii > jj
    eye = (ii == jj).astype(F32)
    sub_diag = (ii // GDN_SUB) == (jj // GDN_SUB)
    row_id = lax.broadcasted_iota(jnp.int32, (c, 1), 0)

    def halo_rows(ref, cb_ref, ci):
        if n_chunks > 1:
            prev = ref[0, pl.ds(pl.multiple_of(jnp.maximum(ci * c - SUBLANE, 0), SUBLANE), SUBLANE), :]
        else:
            prev = jnp.zeros((SUBLANE, wid), F32)
        if has_state:
            pad = jnp.zeros((SUBLANE - (GDN_CONV - 1), wid), F32)
            first = jnp.concatenate([pad, cb_ref[0]], axis=0)
        else:
            first = jnp.zeros((SUBLANE, wid), F32)
        return jnp.where(ci > 0, prev, first)

    def conv_silu(ref, cb_ref, cw_ref, ci):
        if padded:
            cur = jnp.concatenate([ref[0], jnp.zeros((c - t, wid), F32)], axis=0)
        else:
            cur = ref[0, pl.ds(pl.multiple_of(ci * c, c), c), :]
        w = jnp.concatenate([halo_rows(ref, cb_ref, ci), cur], axis=0)
        out = cur * cw_ref[GDN_CONV - 1]
        for i in range(1, GDN_CONV):
            out = out + w[SUBLANE - i:SUBLANE - i + c] * cw_ref[GDN_CONV - 1 - i]
        return _silu(out)

    cpt = GDN_CHUNKS_PER_TRIP if n_chunks % GDN_CHUNKS_PER_TRIP == 0 else 1
    cols = [slice(i * DH, (i + 1) * DH) for i in heads]

    def body(it, carry):
        cb = (cbq_ref, cbk_ref, cbv_ref) if has_state else (None, None, None)
        q, k, v, gate, beta, gcum, g_last, decay = [], [], [], [], [], [], [], []
        for u in range(cpt):
            ci = it * cpt + u
            qa = conv_silu(q_ref, cb[0], cwq_rows, ci)
            ka = conv_silu(k_ref, cb[1], cwk_rows, ci)
            va = conv_silu(v_ref, cb[2], cwv_rows, ci)
            if padded:
                ba = jnp.concatenate([ba_ref[0, 0], jnp.zeros((c - t, LANE), F32)], axis=0)
                gate.append(jnp.concatenate([gate_ref[0], jnp.zeros((c - t, wid), F32)], axis=0))
                live = row_id < t
            else:
                ba = ba_ref[0, 0, pl.ds(pl.multiple_of(ci * c, c), c), :]
                gate.append(gate_ref[0, pl.ds(pl.multiple_of(ci * c, c), c), :])
            beta_all = jax.nn.sigmoid(ba)
            z = ba + dt_ref[0]
            softplus = jnp.maximum(z, 0.0) + jnp.log1p(jnp.exp(-jnp.abs(z)))
            g_all = -jnp.exp(alog_ref[0]) * softplus
            if padded:
                beta_all = jnp.where(live, beta_all, 0.0)
                g_all = jnp.where(live, g_all, 0.0)
            gcum_all = _cumsum_rows(g_all)
            gcum_t = gcum_all.T
            for i in heads:
                qi = qa[:, cols[i]]
                ki = ka[:, cols[i]]
                vi = va[:, cols[i]]
                qi = qi * lax.rsqrt(jnp.sum(qi * qi, axis=-1, keepdims=True) + EPS) * (DH ** -0.5)
                ki = ki * lax.rsqrt(jnp.sum(ki * ki, axis=-1, keepdims=True) + EPS)
                if padded:
                    qi = jnp.where(live, qi, 0.0)
                    ki = jnp.where(live, ki, 0.0)
                    vi = jnp.where(live, vi, 0.0)
                q.append(qi)
                k.append(ki)
                v.append(vi)
                beta.append(beta_all[:, i:i + 1])
                gc = gcum_all[:, hg + i:hg + i + 1]
                gcum.append(gc)
                g_last.append(gcum_all[c - 1:c, hg + i:hg + i + 1])
                decay.append(jnp.where(
                    tril, jnp.exp(jnp.where(tril, gc - gcum_t[hg + i:hg + i + 1, :], 0.0)), 0.0))
        chains = range(cpt * hg)
        e_cum = [jnp.exp(g) for g in gcum]
        kb = [k[n] * beta[n] for n in chains]
        k16 = [x.astype(BF16) for x in k]
        kk = [_dot_nt(kb[n].astype(BF16), k16[n]) for n in chains]
        qk = [_dot_nt(q[n].astype(BF16), k16[n]) for n in chains]
        m_mat = [jnp.where(strict, kk[n] * decay[n], 0.0) for n in chains]
        a_qk = [(qk[n] * decay[n]).astype(BF16) for n in chains]
        rhs = [jnp.concatenate([v[n] * beta[n], kb[n] * e_cum[n]], axis=1) for n in chains]
        sol = _unit_lower_solve(m_mat, rhs, eye, sub_diag)
        q_dec = [(q[n] * e_cum[n]).astype(BF16) for n in chains]
        k_dec_t = [(k[n] * jnp.exp(g_last[n] - gcum[n])).T.astype(BF16) for n in chains]
        state = [s_ref[0, i] for i in heads]
        for u in range(cpt):
            ns = [u * hg + i for i in heads]
            s16 = [x.astype(BF16) for x in state]
            w_s = [_dot(sol[ns[i]][:, DH:].astype(BF16), s16[i]) for i in heads]
            q_s = [_dot(q_dec[ns[i]], s16[i]) for i in heads]
            vn16 = [(sol[ns[i]][:, :DH] - w_s[i]).astype(BF16) for i in heads]
            o = [q_s[i] + _dot(a_qk[ns[i]], vn16[i]) for i in heads]
            upd = [_dot(k_dec_t[ns[i]], vn16[i]) for i in heads]
            state = [state[i] * jnp.exp(g_last[ns[i]]) + upd[i] for i in heads]
            y = jnp.concatenate(
                [_rms_rows(o[i]) * gain_ref[...] * _silu(gate[u][:, cols[i]]) for i in heads], axis=1)
            if padded:
                y_ref[0] = y[:t].astype(y_ref.dtype)
            else:
                y_ref[0, pl.ds(pl.multiple_of((it * cpt + u) * c, c), c), :] = y.astype(y_ref.dtype)
        for i in heads:
            s_ref[0, i] = state[i]
        return carry

    lax.fori_loop(0, n_chunks // cpt, body, 0)


def gdn_group_tail(ba, batch, seq):
    hg = GDN_HEADS_PER_STEP
    groups = H_C // hg
    br = ba[:, :H_C].reshape(batch, seq, groups, hg)
    ar = ba[:, H_C:2 * H_C].reshape(batch, seq, groups, hg)
    cat = jnp.concatenate([br, ar], axis=-1).transpose(0, 2, 1, 3)
    return jnp.pad(cat, ((0, 0), (0, 0), (0, 0), (0, LANE - 2 * hg)))


def gdn_group_vec(v):
    hg = GDN_HEADS_PER_STEP
    return jnp.pad(v.reshape(H_C // hg, 1, hg), ((0, 0), (0, 0), (hg, LANE - 2 * hg)))


def gdn_mixer(proj3, bag, conv_w, alog_vec, dt_vec, out_gain, conv_buf, state0, out_dtype):
    b, t, _ = proj3.shape
    hg = GDN_HEADS_PER_STEP
    wid = hg * DH
    has_state = state0 is not None
    col = lambda off: (lambda i, j: (i, 0, off // wid + j))
    wcol = lambda off: (lambda i, j: (0, off // wid + j))
    in_specs = [pl.BlockSpec((1, t, wid), col(OFF_C)),
                pl.BlockSpec((1, t, wid), col(OFF_C + D_C)),
                pl.BlockSpec((1, t, wid), col(OFF_C + 2 * D_C)),
                pl.BlockSpec((1, t, wid), col(OFF_G)),
                pl.BlockSpec((1, 1, t, LANE), lambda i, j: (i, j, 0, 0)),
                pl.BlockSpec((GDN_CONV, wid), wcol(0)),
                pl.BlockSpec((GDN_CONV, wid), wcol(D_C)),
                pl.BlockSpec((GDN_CONV, wid), wcol(2 * D_C)),
                pl.BlockSpec((1, 1, LANE), lambda i, j: (j, 0, 0)),
                pl.BlockSpec((1, 1, LANE), lambda i, j: (j, 0, 0)),
                pl.BlockSpec((1, DH), lambda i, j: (0, 0))]
    args = [proj3, proj3, proj3, proj3, bag, conv_w, conv_w, conv_w, alog_vec, dt_vec, out_gain]
    if has_state:
        in_specs += [pl.BlockSpec((1, GDN_CONV - 1, wid), col(0)),
                     pl.BlockSpec((1, GDN_CONV - 1, wid), col(D_C)),
                     pl.BlockSpec((1, GDN_CONV - 1, wid), col(2 * D_C)),
                     pl.BlockSpec((1, hg, DH, DH), lambda i, j: (i, j, 0, 0))]
        args += [conv_buf, conv_buf, conv_buf, state0]
    vmem = 2 * (4 * t * wid * 4 + t * LANE * 4 + t * wid * 4 + 2 * hg * DH * DH * 4) + (4 << 20)
    return pl.pallas_call(
        functools.partial(_gdn_kernel, t=t, has_state=has_state),
        grid=(b, H_C // hg),
        in_specs=in_specs,
        out_specs=[pl.BlockSpec((1, t, wid), lambda i, j: (i, 0, j)),
                   pl.BlockSpec((1, hg, DH, DH), lambda i, j: (i, j, 0, 0))],
        out_shape=[jax.ShapeDtypeStruct((b, t, D_C), out_dtype),
                   jax.ShapeDtypeStruct((b, H_C, DH, DH), F32)],
        scratch_shapes=[pltpu.VMEM((GDN_CONV, GDN_CHUNK, wid), F32)] * 3,
        compiler_params=_cparams(("parallel", "parallel"), vmem),
        name="gdn_mixer",
    )(*args)


def _mix_dot(ya_ref, yb_ref, yc_ref, wbf_ref):
    acc = _dot(ya_ref[...], wbf_ref[0:D_A, :])
    acc = acc + _dot(yb_ref[...], wbf_ref[D_A:D_A + D_B, :])
    return acc + _dot(yc_ref[...], wbf_ref[D_A + D_B:, :])


def _proj_out_kernel(yap_ref, ybp_ref, ycp_ref, yas_ref, ybs_ref, ycs_ref, w_ref, xp_ref, xs_ref,
                     op_ref, os_ref, wbf_ref):
    @pl.when(pl.program_id(1) == 0)
    def _():
        wbf_ref[...] = w_ref[...].astype(BF16)
        os_ref[...] = xs_ref[...] + _mix_dot(yas_ref, ybs_ref, ycs_ref, wbf_ref)

    op_ref[...] = xp_ref[...] + _mix_dot(yap_ref, ybp_ref, ycp_ref, wbf_ref)


def proj_out(mix_p, mix_s, w_all, layer, x_p, x_s):
    mp, n = x_p.shape
    ms = x_s.shape[0]
    tm, tn = MM_TM, MM_TN
    kk = D_A + D_B + D_C
    vmem = 2 * ((tm + ms) * kk * 2 + kk * tn * 4 + 2 * (tm + ms) * tn * 4) + 3 * kk * tn * 2
    rowp = lambda j, i: (i, 0)
    rows = lambda j, i: (0, 0)
    widths = (D_A, D_B, D_C)
    return pl.pallas_call(
        _proj_out_kernel,
        grid=(n // tn, mp // tm),
        in_specs=([pl.BlockSpec((tm, w), rowp) for w in widths]
                  + [pl.BlockSpec((ms, w), rows) for w in widths]
                  + [pl.BlockSpec((None, kk, tn), lambda j, i: (layer, 0, j)),
                     pl.BlockSpec((tm, tn), lambda j, i: (i, j)),
                     pl.BlockSpec((ms, tn), lambda j, i: (0, j))]),
        out_specs=[pl.BlockSpec((tm, tn), lambda j, i: (i, j)),
                   pl.BlockSpec((ms, tn), lambda j, i: (0, j))],
        out_shape=[jax.ShapeDtypeStruct((mp, n), F32), jax.ShapeDtypeStruct((ms, n), F32)],
        scratch_shapes=[pltpu.VMEM((kk, tn), BF16)],
        compiler_params=_cparams(("arbitrary", "arbitrary"), vmem),
        name="proj_out",
    )(*mix_p, *mix_s, w_all, x_p, x_s)


FFN_TN = 256


def _ffn_conv(cur, ext_ref, cw_ref, b_ref, rows):
    out = cur * cw_ref[FFN_CONV - 1:FFN_CONV, :] + b_ref[...]
    for i in range(1, FFN_CONV):
        out = out + ext_ref[SUBLANE - i:SUBLANE - i + rows, :] * cw_ref[FFN_CONV - 1 - i:FFN_CONV - i, :]
    return out


def _shift_rows(u, halo, i):
    n, w = u.shape
    rot = pltpu.roll(u.reshape(n // SUBLANE, SUBLANE, w), i, axis=1)
    above = jnp.concatenate([pltpu.roll(halo, i, axis=0)[None], rot[:-1]], axis=0)
    sub = lax.broadcasted_iota(jnp.int32, (1, SUBLANE, 1), 1)
    return jnp.where(sub < i, above, rot).reshape(n, w)


def _ffn_conv_rows(u, halo, cw_ref, b_ref):
    out = u * cw_ref[FFN_CONV - 1:FFN_CONV, :] + b_ref[...]
    for i in range(1, FFN_CONV):
        out = out + _shift_rows(u, halo, i) * cw_ref[FFN_CONV - 1 - i:FFN_CONV - i, :]
    return out


FFN_ROW_SPLIT = 2


def _ffn_up_kernel(xp_ref, xs_ref, wg_ref, wv_ref, cwg_ref, cwv_ref, bg_ref, bv_ref, sg_ref, sv_ref,
                   actp_ref, tg_ref, tv_ref, acts_ref, ugs_ref, uvs_ref,
                   wgbf_ref, wvbf_ref, halo_ref, ext_ref, *, tiles_per_seq, t):
    tm = xp_ref.shape[0]
    rows = tm // FFN_ROW_SPLIT
    mi = pl.program_id(1)

    @pl.when(mi == 0)
    def _():
        wgbf_ref[...] = wg_ref[...].astype(BF16)
        wvbf_ref[...] = wv_ref[...].astype(BF16)
        assert t == SUBLANE
        xs = xs_ref[...]
        nb = xs.shape[0] // t
        us = _dot(xs, wgbf_ref[...])
        vs = _dot(xs, wvbf_ref[...])
        ugs_ref[...] = us
        uvs_ref[...] = vs

        def conv(u, st_ref, cw_ref, b_ref):
            outs = []
            for bi in range(nb):
                cur = u[bi * t:(bi + 1) * t]
                ext_ref[SUBLANE - (FFN_CONV - 1):SUBLANE, :] = st_ref[bi]
                ext_ref[SUBLANE:, :] = cur
                outs.append(_ffn_conv(cur, ext_ref, cw_ref, b_ref, t))
            return jnp.concatenate(outs, axis=0)

        gts = conv(us, sg_ref, cwg_ref, bg_ref)
        vls = conv(vs, sv_ref, cwv_ref, bv_ref)
        acts_ref[...] = (_silu(gts) * vls).astype(acts_ref.dtype)

    @pl.when(mi % tiles_per_seq == 0)
    def _():
        halo_ref[...] = jnp.zeros(halo_ref.shape, F32)

    halo_g = halo_ref[0]
    halo_v = halo_ref[1]
    ug = []
    uv = []
    for s in range(FFN_ROW_SPLIT):
        x = xp_ref[s * rows:(s + 1) * rows, :]
        ug.append(_dot(x, wgbf_ref[...]))
        uv.append(_dot(x, wvbf_ref[...]))
    for s in range(FFN_ROW_SPLIT):
        gt = _ffn_conv_rows(ug[s], halo_g, cwg_ref, bg_ref)
        vl = _ffn_conv_rows(uv[s], halo_v, cwv_ref, bv_ref)
        actp_ref[s * rows:(s + 1) * rows, :] = (_silu(gt) * vl).astype(actp_ref.dtype)
        halo_g = ug[s][rows - SUBLANE:]
        halo_v = uv[s][rows - SUBLANE:]
    halo_ref[0] = halo_g
    halo_ref[1] = halo_v
    tg_ref[0] = halo_g
    tv_ref[0] = halo_v


def ffn_up(xn_p, xn_s, w_up, conv_w, conv_b, conv_state, layer, seq, t):
    mp, k = xn_p.shape
    ms = xn_s.shape[0]
    d_ff = w_up.shape[2] // 2
    tn = FFN_TN
    nt = d_ff // tn
    tm = min(seq, MM_TM)
    mt = mp // tm
    nb = ms // t
    lo = lambda j, i: (layer, 0, j)
    hi = lambda j, i: (layer, 0, nt + j)
    vmem = (2 * (tm * k * 2 + ms * k * 2 + 2 * k * tn * 4 + tm * tn * 2) + 2 * k * tn * 2
            + 8 * tm * tn * 4)
    return pl.pallas_call(
        functools.partial(_ffn_up_kernel, tiles_per_seq=seq // tm, t=t),
        grid=(nt, mt),
        in_specs=[pl.BlockSpec((tm, k), lambda j, i: (i, 0)),
                  pl.BlockSpec((ms, k), lambda j, i: (0, 0)),
                  pl.BlockSpec((None, k, tn), lo), pl.BlockSpec((None, k, tn), hi),
                  pl.BlockSpec((None, FFN_CONV, tn), lo), pl.BlockSpec((None, FFN_CONV, tn), hi),
                  pl.BlockSpec((None, 1, tn), lo), pl.BlockSpec((None, 1, tn), hi),
                  pl.BlockSpec((None, nb, FFN_CONV - 1, tn), lambda j, i: (layer, 0, 0, j)),
                  pl.BlockSpec((None, nb, FFN_CONV - 1, tn), lambda j, i: (layer, 0, 0, nt + j))],
        out_specs=[pl.BlockSpec((tm, tn), lambda j, i: (i, j)),
                   pl.BlockSpec((1, SUBLANE, tn), lambda j, i: (i, 0, j)),
                   pl.BlockSpec((1, SUBLANE, tn), lambda j, i: (i, 0, j)),
                   pl.BlockSpec((ms, tn), lambda j, i: (0, j)),
                   pl.BlockSpec((ms, tn), lambda j, i: (0, j)),
                   pl.BlockSpec((ms, tn), lambda j, i: (0, j))],
        out_shape=[jax.ShapeDtypeStruct((mp, d_ff), BF16),
                   jax.ShapeDtypeStruct((mt, SUBLANE, d_ff), F32),
                   jax.ShapeDtypeStruct((mt, SUBLANE, d_ff), F32),
                   jax.ShapeDtypeStruct((ms, d_ff), BF16),
                   jax.ShapeDtypeStruct((ms, d_ff), F32),
                   jax.ShapeDtypeStruct((ms, d_ff), F32)],
        scratch_shapes=[pltpu.VMEM((k, tn), BF16), pltpu.VMEM((k, tn), BF16),
                        pltpu.VMEM((2, SUBLANE, tn), F32), pltpu.VMEM((SUBLANE + t, tn), F32)],
        compiler_params=_cparams(("arbitrary", "arbitrary"), vmem),
        name="ffn_up",
    )(xn_p, xn_s, w_up, w_up, conv_w, conv_w, conv_b, conv_b, conv_state, conv_state)


def _ffn_down_kernel(a_ref, w_ref, h_ref, o_ref):
    o_ref[...] = h_ref[...] + _dot(a_ref[...], w_ref[...])


def ffn_down_proj(act, w_down, layer, h):
    m, k = act.shape
    n = w_down.shape[2]
    tm = min(m, 512)
    tn = 512
    vmem = 2 * (tm * k * 2 + k * tn * 2 + 2 * tm * tn * 4)
    return pl.pallas_call(
        _ffn_down_kernel,
        grid=(m // tm, n // tn),
        in_specs=[pl.BlockSpec((tm, k), lambda i, j: (i, 0)),
                  pl.BlockSpec((None, k, tn), lambda i, j: (layer, 0, j)),
                  pl.BlockSpec((tm, tn), lambda i, j: (i, j))],
        out_specs=pl.BlockSpec((tm, tn), lambda i, j: (i, j)),
        out_shape=jax.ShapeDtypeStruct((m, n), F32),
        compiler_params=_cparams(("parallel", "parallel"), vmem),
        name="ffn_down",
    )(act, w_down, h)


def _t5_bucket(dist):
    max_exact = REL_BUCKETS // 2
    d = jnp.maximum(dist, 1).astype(F32)
    large = max_exact + (jnp.log(d / max_exact) / math.log(REL_MAX_DIST / max_exact)
                         * (REL_BUCKETS - max_exact)).astype(jnp.int32)
    large = jnp.minimum(large, REL_BUCKETS - 1)
    return jnp.where(dist < max_exact, dist, large)


def _bias_lookup(rel_bias, buckets):
    onehot = jax.nn.one_hot(buckets, REL_BUCKETS, dtype=F32)
    return jnp.einsum('...k,kh->...h', onehot, rel_bias.astype(F32), precision=HIGHEST)


def _prompt_bias_blocks(rel_bias):
    qi = np.arange(BLK)[:, None]
    ki = np.arange(2 * BLK)[None, :]
    rel = qi + BLK - ki
    out = []
    for window, dil in A_BRANCHES:
        nj = window // dil + 1
        valid = (rel >= 0) & (rel < nj)
        buckets = _t5_bucket(jnp.asarray(np.clip(rel, 0, nj - 1) * dil, jnp.int32))
        bias = _bias_lookup(rel_bias, buckets)
        out.append(jnp.where(jnp.asarray(valid)[..., None], bias, NEG_INF).transpose(2, 0, 1))
    return jnp.stack(out)


def _branch_count(dist):
    cnt = np.zeros(dist.shape, np.float32)
    for window, dil in A_BRANCHES:
        cnt += ((dist >= 0) & (dist % dil == 0) & (dist // dil <= window // dil)).astype(np.float32)
    return cnt


def _sample_bias_tables(rel_bias, t, n_past):
    tq = np.arange(t)[:, None]
    d_cache = n_past + tq - np.arange(n_past)[None, :]
    d_new = tq - np.arange(BLK)[None, :]
    d_new = np.where(np.arange(BLK)[None, :] < t, d_new, -1)
    tables = []
    for dist in (d_cache, d_new):
        cnt = _branch_count(dist)
        bias = _bias_lookup(rel_bias, _t5_bucket(jnp.asarray(np.maximum(dist, 0), jnp.int32)))
        bias = jnp.where(jnp.asarray(cnt > 0)[..., None], bias, NEG_INF).transpose(2, 0, 1)
        tables += [bias, jnp.asarray(cnt)]
    return tables


def _mixers(proj, ba, lw, *, batch, seq, layer, cache_hk=None, pool_buf=None, gconv_buf=None,
            gstate=None, bias_tabs=None):
    sample = cache_hk is not None
    mix_dtype = F32 if sample else BF16
    proj3 = proj.reshape(batch, seq, N_MAIN)
    bag = gdn_group_tail(ba, batch, seq)
    qg = lw['a_q_norm'].reshape(1, DH)
    kg = lw['a_k_norm'].reshape(1, DH)
    og = lw['a_out_norm'].reshape(H_A, DH)
    if sample:
        ya, kv = attn_sample(proj3, cache_hk, layer, qg, kg, og, *bias_tabs)
    else:
        ya, kv = attn_prompt(proj3, qg, kg, og, bias_tabs)
    kv_rows = kv.transpose(0, 1, 3, 2, 4)
    yb = pool_mixer(proj3, pool_buf, lw['pool_w'], lw['pool_scale'].reshape(1, D_B), mix_dtype)
    pu = proj3[:, :, OFF_P:OFF_P + D_B]
    if sample:
        pool_new = jnp.concatenate([pool_buf, pu], axis=1)[:, -POOL_BUF:]
    else:
        pool_new = pu[:, -POOL_BUF:]
    yc, gstate_new = gdn_mixer(proj3, bag, lw['gdn_conv_w'], lw['alog_vec'], lw['dt_vec'],
                               lw['gdn_out_norm'].reshape(1, DH), gconv_buf, gstate, mix_dtype)
    gconv_new = proj3[:, -(GDN_CONV - 1):, OFF_C:OFF_C + 3 * D_C]
    m = batch * seq
    mix = (ya.reshape(m, D_A).astype(BF16), yb.reshape(m, D_B).astype(BF16),
           yc.reshape(m, D_C).astype(BF16))
    return mix, (kv_rows, pool_new, gconv_new, gstate_new)


def _layer(xp, xs, lw, gw, *, layer, prompt_shape, sample_shape, cache_hk, pool_buf, gconv_buf, gstate,
           fconv_state, prompt_bias, sample_tabs):
    bp, sp = prompt_shape
    bs, ss = sample_shape
    xnp = rmsnorm_cast(xp, lw['norm_mix'])
    xns = rmsnorm_cast(xs, lw['norm_mix'])
    proj_p, proj_s = proj_in(xnp, xns, gw['w_in_t'], layer)
    ba_p = proj_tail(xnp, gw['w_in_t'], layer)
    ba_s = proj_tail(xns, gw['w_in_t'], layer)
    mix_p, outs_p = _mixers(proj_p, ba_p, lw, batch=bp, seq=sp, layer=layer, bias_tabs=prompt_bias)
    mix_s, outs_s = _mixers(proj_s, ba_s, lw, batch=bs, seq=ss, layer=layer, cache_hk=cache_hk,
                            pool_buf=pool_buf, gconv_buf=gconv_buf, gstate=gstate, bias_tabs=sample_tabs)
    hp, hs = proj_out(mix_p, mix_s, gw['w_out'], layer, xp, xs)
    hnp = rmsnorm_cast(hp, lw['norm_ffn'])
    hns = rmsnorm_cast(hs, lw['norm_ffn'])
    act_p, tg, tv, act_s, ug, uv = ffn_up(hnp, hns, gw['ffn_up'], gw['ffn_conv_w'], gw['ffn_conv_b'],
                                          fconv_state, layer, sp, ss)
    tails = jnp.concatenate([tg, tv], axis=-1)
    tiles_per_seq = tails.shape[0] // bp
    fconv_p = tails[tiles_per_seq - 1::tiles_per_seq, -(FFN_CONV - 1):]
    up_s = jnp.concatenate([ug, uv], axis=-1).reshape(bs, ss, -1)
    fconv_s = up_s[:, -(FFN_CONV - 1):]
    yp = ffn_down_proj(act_p, gw['ffn_down'], layer, hp)
    ys = ffn_down_proj(act_s, gw['ffn_down'], layer, hs)
    return yp, ys, outs_p + (fconv_p,), outs_s + (fconv_s,)


def kernel(x_prompt, x_sample, cache_attn_kv, state_pool, state_gdn_conv, state_gdn, state_ffn_conv,
           rel_bias, norm_mix, w_in, a_q_norm, a_k_norm, a_out_norm, pool_w, pool_scale,
           gdn_conv_w, gdn_a_log, gdn_dt_bias, gdn_out_norm, w_out, norm_ffn,
           ffn_up, ffn_conv_w, ffn_conv_b, ffn_down):
    depth = w_in.shape[0]
    bp, sp, d_model = x_prompt.shape
    bs, ss, _ = x_sample.shape
    n_past = cache_attn_kv.shape[2]
    assert w_in.shape[2] == N_MAIN + N_TAIL and w_out.shape[1] == D_A + D_B + D_C

    prompt_bias = _prompt_bias_blocks(rel_bias)
    sample_tabs = _sample_bias_tables(rel_bias, ss, n_past)
    cache_hk = jnp.swapaxes(cache_attn_kv, 3, 4).reshape(depth, bs, n_past * H_A * 2, DH)

    xp = x_prompt.reshape(bp * sp, d_model)
    xs = x_sample.reshape(bs * ss, d_model)
    p_out = [[] for _ in range(5)]
    s_out = [[] for _ in range(5)]
    gw = {
        'w_in_t': jnp.swapaxes(w_in, 1, 2),
        'w_out': w_out, 'ffn_up': ffn_up, 'ffn_down': ffn_down.astype(BF16),
        'ffn_conv_w': ffn_conv_w, 'ffn_conv_b': ffn_conv_b.reshape(depth, 1, -1),
    }
    for l in range(depth):
        lw = {
            'norm_mix': norm_mix[l], 'norm_ffn': norm_ffn[l],
            'a_q_norm': a_q_norm[l], 'a_k_norm': a_k_norm[l], 'a_out_norm': a_out_norm[l],
            'pool_w': pool_w[l].astype(BF16), 'pool_scale': pool_scale[l],
            'gdn_conv_w': gdn_conv_w[l],
            'alog_vec': gdn_group_vec(gdn_a_log[l]), 'dt_vec': gdn_group_vec(gdn_dt_bias[l]),
            'gdn_out_norm': gdn_out_norm[l],
        }
        xp, xs, outs_p, outs_s = _layer(
            xp, xs, lw, gw, layer=l, prompt_shape=(bp, sp), sample_shape=(bs, ss), cache_hk=cache_hk,
            pool_buf=state_pool[l], gconv_buf=state_gdn_conv[l], gstate=state_gdn[l],
            fconv_state=state_ffn_conv, prompt_bias=prompt_bias, sample_tabs=sample_tabs)
        for acc, o in zip(p_out, outs_p):
            acc.append(o)
        for acc, o in zip(s_out, outs_s):
            acc.append(o)
    res = [xp.reshape(bp, sp, d_model), xs.reshape(bs, ss, d_model)]
    for po, so in zip(p_out, s_out):
        res += [jnp.stack(po), jnp.stack(so)]
    return tuple(res)
```
